```python
import math
import jax, jax.numpy as jnp
from jax import lax
import numpy as np

D_MODEL = 1024
BATCH = 2
SEQ = 8192
DEPTH = 4
DEC_BATCH = 16
DEC_SEQ = 64
PAST_LEN = 1024

CHUNK = 64
N_MIXERS = 3
N_A = (DEPTH + 2) // 3
N_B = (DEPTH + 1) // 3
N_C = DEPTH // 3
EPS = 1e-6
F32 = jnp.float32

GDN_HEADS = 8
GDN_DK = D_MODEL // 8
GDN_DV = D_MODEL // 8
GDN_KDIM = GDN_HEADS * GDN_DK
GDN_VDIM = GDN_HEADS * GDN_DV
GDN_QKV = 2 * GDN_KDIM + GDN_VDIM
CONV_W = 4
GDN_IN = GDN_QKV + GDN_VDIM + 2 * GDN_HEADS

ML_HEADS = 4
ML_DQK = D_MODEL // 8
ML_DV = D_MODEL // 4
ML_QK = ML_HEADS * ML_DQK
ML_VD = ML_HEADS * ML_DV
ML_IN = 2 * ML_QK + 2 * ML_VD + 2 * ML_HEADS
GATE_CAP = 15.0

FOX_HEADS = 16
FOX_DH = D_MODEL // 16
FOX_DIM = FOX_HEADS * FOX_DH
FOX_IN = 4 * FOX_DIM + FOX_HEADS
FOX_QBLOCK = 128

N_MEM = 256
X_HEADS = 4
X_DH = D_MODEL // X_HEADS

D_FF = ((8 * D_MODEL + 3 * 256 - 1) // (3 * 256)) * 256

kernel_name = 'hybrid_gdn_mlstm_fox_streaming_step'


def rmsnorm(x, gain):
    xf = x.astype(F32)
    y = xf * lax.rsqrt(jnp.mean(xf * xf, axis=-1, keepdims=True) + EPS)
    return (y * gain.astype(F32)).astype(x.dtype)


def l2norm(x):
    xf = x.astype(F32)
    return xf * lax.rsqrt(jnp.sum(xf * xf, axis=-1, keepdims=True) + EPS)


def softcap(x):
    return GATE_CAP * jnp.tanh(x / GATE_CAP)


def causal_conv(x, buf, w):
    T = x.shape[1]
    xp = jnp.concatenate([buf.astype(x.dtype), x], axis=1)
    y = xp[:, 0:T] * w[:, 0]
    for j in range(1, CONV_W):
        y = y + xp[:, j:j + T] * w[:, j]
    return jax.nn.silu(y), xp[:, T:]


def to_chunks(a, L):
    B, T = a.shape[:2]
    a = a.reshape((B, T // L, L) + a.shape[2:])
    return jnp.moveaxis(jnp.moveaxis(a, 1, 0), 3, 2)


def from_chunks(a):
    n, B, H, L, d = a.shape
    return jnp.moveaxis(jnp.moveaxis(a, 2, 3), 0, 1).reshape(B, n * L, H, d)


def gated_delta_rule(q, k, v, g, beta, S0):
    T = q.shape[1]
    L = min(CHUNK, T)
    incl = jnp.tril(jnp.ones((L, L), bool))
    strict = jnp.tril(jnp.ones((L, L), bool), -1)
    eye = jnp.eye(L, dtype=F32)

    def step(S, blk):
        qb, kb, vb, gb, bb = blk
        G = jnp.cumsum(gb, axis=-1)
        dec = jnp.exp(jnp.where(incl, G[..., :, None] - G[..., None, :], -jnp.inf))
        A = jnp.where(strict, bb[..., :, None] * jnp.einsum('bhik,bhjk->bhij', kb, kb) * dec, 0.0)
        rhs = jnp.concatenate([bb[..., None] * vb, (bb * jnp.exp(G))[..., None] * kb], axis=-1)
        sol = lax.linalg.triangular_solve(A + eye, rhs, left_side=True, lower=True, unit_diagonal=True)
        U = sol[..., :GDN_DV] - jnp.einsum('bhlk,bhkv->bhlv', sol[..., GDN_DV:], S)
        o = (jnp.exp(G)[..., None] * jnp.einsum('bhlk,bhkv->bhlv', qb, S)
             + jnp.einsum('bhij,bhjv->bhiv', jnp.einsum('bhik,bhjk->bhij', qb, kb) * dec, U))
        GL = G[..., -1:]
        S = jnp.exp(GL)[..., None] * S + jnp.einsum('bhjk,bhjv->bhkv', kb * jnp.exp(GL - G)[..., None], U)
        return S, o

    blocks = tuple(to_chunks(a.astype(F32), L) for a in (q, k, v, g, beta))
    S, o = lax.scan(step, S0.astype(F32), blocks)
    return from_chunks(o), S


def mlstm_recurrence(q, k, v, ig, lf, C0, n0, m0):
    T = q.shape[1]
    L = min(CHUNK, T)
    incl = jnp.tril(jnp.ones((L, L), bool))

    def step(carry, blk):
        C, n, m = carry
        qb, kb, vb, ib, fb = blk
        b = jnp.cumsum(fb, axis=-1)
        D = jnp.where(incl, b[..., :, None] - b[..., None, :] + ib[..., None, :], -jnp.inf)
        inter = b + m[..., None]
        mt = jnp.maximum(inter, jnp.max(D, axis=-1))
        W = jnp.exp(D - mt[..., None]) * jnp.einsum('bhik,bhjk->bhij', qb, kb)
        s_inter = jnp.exp(inter - mt)
        num = s_inter[..., None] * jnp.einsum('bhik,bhkv->bhiv', qb, C) + jnp.einsum('bhij,bhjv->bhiv', W, vb)
        den = s_inter * jnp.einsum('bhik,bhk->bhi', qb, n) + jnp.sum(W, axis=-1)
        h = num / jnp.maximum(jnp.abs(den), jnp.exp(-mt))[..., None]
        mL = mt[..., -1]
        wk = jnp.exp(b[..., -1:] - b + ib - mL[..., None])[..., None] * kb
        fL = jnp.exp(b[..., -1] + m - mL)
        C = fL[..., None, None] * C + jnp.einsum('bhjk,bhjv->bhkv', wk, vb)
        n = fL[..., None] * n + jnp.sum(wk, axis=-2)
        return (C, n, mL), h

    blocks = tuple(to_chunks(a.astype(F32), L) for a in (q, k, v, ig, lf))
    (C, n, m), h = lax.scan(step, (C0.astype(F32), n0.astype(F32), m0.astype(F32)), blocks)
    return from_chunks(h), C, n, m


def forgetting_attention(q, k, v, lf, k_past, v_past, lf_past):
    B, T, H, DH = q.shape
    P = k_past.shape[1]
    k_all = jnp.concatenate([k_past.astype(k.dtype), k], axis=1)
    v_all = jnp.concatenate([v_past.astype(v.dtype), v], axis=1)
    F = jnp.cumsum(jnp.concatenate([lf_past.astype(F32), lf.astype(F32)], axis=1), axis=1)
    Fk = jnp.transpose(F, (0, 2, 1))
    QB = min(FOX_QBLOCK, T)
    nb = T // QB
    q_blocks = jnp.moveaxis(q.reshape(B, nb, QB, H, DH), 1, 0)
    F_blocks = jnp.transpose(F[:, P:].reshape(B, nb, QB, H), (1, 0, 3, 2))
    q_pos = (P + jnp.arange(T, dtype=jnp.int32)).reshape(nb, QB)
    k_pos = jnp.arange(P + T, dtype=jnp.int32)
    scale = DH ** -0.5

    def block(args):
        qb, fb, pb = args
        s = jnp.einsum('bqhd,bkhd->bhqk', qb, k_all).astype(F32) * scale
        s = s + fb[..., :, None] - Fk[:, :, None, :]
        s = jnp.where(k_pos[None, :] <= pb[:, None], s, -jnp.inf)
        p = jax.nn.softmax(s, axis=-1).astype(v_all.dtype)
        return jnp.einsum('bhqk,bkhd->bqhd', p, v_all)

    o = lax.map(block, (q_blocks, F_blocks, q_pos))
    return jnp.moveaxis(o, 0, 1).reshape(B, T, H, DH)


def gdn_mixer(h, conv_buf, S0, w_in, w_conv, a_log, dt_bias, onorm, w_out):
    B, T, _ = h.shape
    proj = h @ w_in
    o0 = GDN_QKV + GDN_VDIM
    qkv, new_buf = causal_conv(proj[..., :GDN_QKV], conv_buf, w_conv)
    gate = proj[..., GDN_QKV:o0].reshape(B, T, GDN_HEADS, GDN_DV)
    b_pre = proj[..., o0:o0 + GDN_HEADS]
    a_pre = proj[..., o0 + GDN_HEADS:]
    q = l2norm(qkv[..., :GDN_KDIM].reshape(B, T, GDN_HEADS, GDN_DK)) * (GDN_DK ** -0.5)
    k = l2norm(qkv[..., GDN_KDIM:2 * GDN_KDIM].reshape(B, T, GDN_HEADS, GDN_DK))
    v = qkv[..., 2 * GDN_KDIM:].reshape(B, T, GDN_HEADS, GDN_DV)
    beta = jax.nn.sigmoid(b_pre.astype(F32))
    g = -jnp.exp(a_log.astype(F32)) * jax.nn.softplus(a_pre.astype(F32) + dt_bias.astype(F32))
    o, S = gated_delta_rule(q, k, v, g, beta, S0)
    o = rmsnorm(o, onorm) * jax.nn.silu(gate.astype(F32))
    out = o.astype(h.dtype).reshape(B, T, GDN_VDIM) @ w_out
    return out, new_buf.astype(h.dtype), S.astype(h.dtype)


def mlstm_mixer(h, C0, n0, m0, w_in, b_i, b_f, onorm, w_out):
    B, T, _ = h.shape
    proj = h @ w_in
    q = proj[..., :ML_QK].reshape(B, T, ML_HEADS, ML_DQK)
    k = proj[..., ML_QK:2 * ML_QK].reshape(B, T, ML_HEADS, ML_DQK) * (ML_DQK ** -0.5)
    v = proj[..., 2 * ML_QK:2 * ML_QK + ML_VD].reshape(B, T, ML_HEADS, ML_DV)
    o_pre = proj[..., 2 * ML_QK + ML_VD:2 * ML_QK + 2 * ML_VD].reshape(B, T, ML_HEADS, ML_DV)
    g0 = 2 * ML_QK + 2 * ML_VD
    ig = softcap(proj[..., g0:g0 + ML_HEADS].astype(F32) + b_i.astype(F32))
    lf = jax.nn.log_sigmoid(softcap(proj[..., g0 + ML_HEADS:].astype(F32) + b_f.astype(F32)))
    hh, C, n, m = mlstm_recurrence(q, k, v, ig, lf, C0, n0, m0)
    o = jax.nn.sigmoid(o_pre.astype(F32)) * rmsnorm(hh, onorm)
    out = o.astype(h.dtype).reshape(B, T, ML_VD) @ w_out
    return out, C.astype(h.dtype), n.astype(h.dtype), m.astype(h.dtype)


def fox_mixer(h, k_past, v_past, lf_past, w_in, b_f, qk_norm, w_out):
    B, T, _ = h.shape
    proj = h @ w_in
    q = rmsnorm(proj[..., :FOX_DIM].reshape(B, T, FOX_HEADS, FOX_DH), qk_norm[0])
    k = rmsnorm(proj[..., FOX_DIM:2 * FOX_DIM].reshape(B, T, FOX_HEADS, FOX_DH), qk_norm[1])
    v = proj[..., 2 * FOX_DIM:3 * FOX_DIM].reshape(B, T, FOX_HEADS, FOX_DH)
    gate = proj[..., 3 * FOX_DIM:4 * FOX_DIM]
    lf = jax.nn.log_sigmoid(proj[..., 4 * FOX_DIM:].astype(F32) + b_f.astype(F32))
    o = forgetting_attention(q, k, v, lf, k_past, v_past, lf_past).reshape(B, T, FOX_DIM)
    o = o.astype(F32) * jax.nn.sigmoid(gate.astype(F32))
    out = o.astype(h.dtype) @ w_out
    return out, k, v, lf.astype(h.dtype)


def memory_kv(mem, mem_gain, w_kv):
    B, N, _ = mem.shape
    mf = mem.astype(F32)
    mn = mf * lax.rsqrt(jnp.mean(mf * mf, axis=-1, keepdims=True) + EPS)
    mn = (mn[None] * mem_gain.astype(F32)[:, None, None, :]).astype(mem.dtype)
    kv = jnp.einsum('lbnd,lde->lbne', mn, w_kv).reshape(DEPTH, B, N, 2, X_HEADS, X_DH)
    return kv[:, :, :, 0], kv[:, :, :, 1]


def memory_attention(h, w_q, mk, mv, w_o):
    B, T, _ = h.shape
    q = (h @ w_q).reshape(B, T, X_HEADS, X_DH)
    s = jnp.einsum('bqhd,bkhd->bhqk', q, mk.astype(q.dtype)).astype(F32) * (X_DH ** -0.5)
    p = jax.nn.softmax(s, axis=-1).astype(h.dtype)
    o = jnp.einsum('bhqk,bkhd->bqhd', p, mv.astype(h.dtype))
    return o.reshape(B, T, X_HEADS * X_DH) @ w_o


def swiglu(h, w_gu, w_down):
    gu = h @ w_gu
    return (jax.nn.silu(gu[..., :D_FF]) * gu[..., D_FF:]) @ w_down


def trunk(x, gdn_conv, gdn_S, ml_C, ml_n, ml_m, fox_k, fox_v, fox_lf, mem_k, mem_v, weights):
    (norm_gains, gdn_w_in, gdn_w_conv, gdn_a_log, gdn_dt_bias, gdn_onorm, gdn_w_out,
     ml_w_in, ml_b_i, ml_b_f, ml_onorm, ml_w_out,
     fox_w_in, fox_b_f, fox_qk_norm, fox_w_out,
     x_w_q, x_w_o, ffn_w_gu, ffn_w_down) = weights
    conv_o, S_o, C_o, n_o, m_o, k_o, v_o, lf_o = [], [], [], [], [], [], [], []
    for layer in range(DEPTH):
        kind, j = layer % N_MIXERS, layer // N_MIXERS
        gains = norm_gains[layer]
        h = rmsnorm(x, gains[0])
        if kind == 0:
            mix, buf, S = gdn_mixer(h, gdn_conv[j], gdn_S[j], gdn_w_in[j], gdn_w_conv[j],
                                    gdn_a_log[j], gdn_dt_bias[j], gdn_onorm[j], gdn_w_out[j])
            conv_o.append(buf)
            S_o.append(S)
        elif kind == 1:
            mix, C, n, m = mlstm_mixer(h, ml_C[j], ml_n[j], ml_m[j], ml_w_in[j], ml_b_i[j],
                                       ml_b_f[j], ml_onorm[j], ml_w_out[j])
            C_o.append(C)
            n_o.append(n)
            m_o.append(m)
        else:
            mix, k, v, lf = fox_mixer(h, fox_k[j], fox_v[j], fox_lf[j], fox_w_in[j], fox_b_f[j],
                                      fox_qk_norm[j], fox_w_out[j])
            k_o.append(k)
            v_o.append(v)
            lf_o.append(lf)
        x = x + rmsnorm(mix, gains[1])
        x = x + rmsnorm(memory_attention(rmsnorm(x, gains[2]), x_w_q[layer], mem_k[layer],
                                         mem_v[layer], x_w_o[layer]), gains[3])
        x = x + rmsnorm(swiglu(rmsnorm(x, gains[4]), ffn_w_gu[layer], ffn_w_down[layer]), gains[5])
    return (x, jnp.stack(conv_o), jnp.stack(S_o), jnp.stack(C_o), jnp.stack(n_o), jnp.stack(m_o),
            jnp.stack(k_o), jnp.stack(v_o), jnp.stack(lf_o))


def setup_inputs(seed: int = 0) -> dict:
    key = jax.random.key(seed)
    ks = iter(jax.random.split(key, 48))

    def nrm(shape, scale=1.0):
        return scale * jax.random.normal(next(ks), shape, F32)

    def unif(shape, lo, hi):
        return jax.random.uniform(next(ks), shape, F32, lo, hi)

    dt = jnp.exp(unif((N_A, GDN_HEADS), math.log(1e-3), math.log(1e-1)))
    return {
        'x_prompt': nrm((BATCH, SEQ, D_MODEL)),
        'x_sample': nrm((DEC_BATCH, DEC_SEQ, D_MODEL)),
        'state_gdn_conv': nrm((N_A, DEC_BATCH, CONV_W - 1, GDN_QKV)),
        'state_gdn_S': nrm((N_A, DEC_BATCH, GDN_HEADS, GDN_DK, GDN_DV), 0.1),
        'state_mlstm_C': nrm((N_B, DEC_BATCH, ML_HEADS, ML_DQK, ML_DV), 0.1),
        'state_mlstm_n': nrm((N_B, DEC_BATCH, ML_HEADS, ML_DQK), 0.1),
        'state_mlstm_m': nrm((N_B, DEC_BATCH, ML_HEADS)),
        'cache_fox_k': nrm((N_C, DEC_BATCH, PAST_LEN, FOX_HEADS, FOX_DH)),
        'cache_fox_v': nrm((N_C, DEC_BATCH, PAST_LEN, FOX_HEADS, FOX_DH)),
        'cache_fox_logf': jax.nn.log_sigmoid(nrm((N_C, DEC_BATCH, PAST_LEN, FOX_HEADS)) + 3.0),
        'cache_mem_k': nrm((DEPTH, DEC_BATCH, N_MEM, X_HEADS, X_DH)),
        'cache_mem_v': nrm((DEPTH, DEC_BATCH, N_MEM, X_HEADS, X_DH)),
        'mem_prompt': nrm((BATCH, N_MEM, D_MODEL)),
        'norm_gains': 1.0 + nrm((DEPTH, 7, D_MODEL), 0.02),
        'gdn_w_in': nrm((N_A, D_MODEL, GDN_IN), D_MODEL ** -0.5),
        'gdn_w_conv': nrm((N_A, GDN_QKV, CONV_W), CONV_W ** -0.5),
        'gdn_a_log': jnp.log(unif((N_A, GDN_HEADS), 1.0, 16.0)),
        'gdn_dt_bias': dt + jnp.log(-jnp.expm1(-dt)),
        'gdn_onorm': 1.0 + nrm((N_A, GDN_DV), 0.02),
        'gdn_w_out': nrm((N_A, GDN_VDIM, D_MODEL), GDN_VDIM ** -0.5),
        'ml_w_in': nrm((N_B, D_MODEL, ML_IN), D_MODEL ** -0.5),
        'ml_b_i': nrm((N_B, ML_HEADS), 0.1),
        'ml_b_f': jnp.linspace(3.0, 6.0, ML_HEADS, dtype=F32)[None, :] + nrm((N_B, ML_HEADS), 0.1),
        'ml_onorm': 1.0 + nrm((N_B, ML_DV), 0.02),
        'ml_w_out': nrm((N_B, ML_VD, D_MODEL), ML_VD ** -0.5),
        'fox_w_in': nrm((N_C, D_MODEL, FOX_IN), D_MODEL ** -0.5),
        'fox_b_f': unif((N_C, FOX_HEADS), 1.0, 4.0),
        'fox_qk_norm': 1.0 + nrm((N_C, 2, FOX_DH), 0.02),
        'fox_w_out': nrm((N_C, FOX_DIM, D_MODEL), FOX_DIM ** -0.5),
        'x_w_q': nrm((DEPTH, D_MODEL, X_HEADS * X_DH), D_MODEL ** -0.5),
        'x_w_kv': nrm((DEPTH, D_MODEL, 2 * X_HEADS * X_DH), D_MODEL ** -0.5),
        'x_w_o': nrm((DEPTH, X_HEADS * X_DH, D_MODEL), (X_HEADS * X_DH) ** -0.5),
        'ffn_w_gu': nrm((DEPTH, D_MODEL, 2 * D_FF), D_MODEL ** -0.5),
        'ffn_w_down': nrm((DEPTH, D_FF, D_MODEL), D_FF ** -0.5),
    }


def reference(x_prompt, x_sample, state_gdn_conv, state_gdn_S, state_mlstm_C, state_mlstm_n,
              state_mlstm_m, cache_fox_k, cache_fox_v, cache_fox_logf, cache_mem_k, cache_mem_v,
              mem_prompt, norm_gains, gdn_w_in, gdn_w_conv, gdn_a_log, gdn_dt_bias, gdn_onorm,
              gdn_w_out, ml_w_in, ml_b_i, ml_b_f, ml_onorm, ml_w_out, fox_w_in, fox_b_f,
              fox_qk_norm, fox_w_out, x_w_q, x_w_kv, x_w_o, ffn_w_gu, ffn_w_down):
    weights = (norm_gains, gdn_w_in, gdn_w_conv, gdn_a_log, gdn_dt_bias, gdn_onorm, gdn_w_out,
               ml_w_in, ml_b_i, ml_b_f, ml_onorm, ml_w_out,
               fox_w_in, fox_b_f, fox_qk_norm, fox_w_out,
               x_w_q, x_w_o, ffn_w_gu, ffn_w_down)
    B = x_prompt.shape[0]
    dt = x_prompt.dtype
    p_mem_k, p_mem_v = memory_kv(mem_prompt, norm_gains[:, 6], x_w_kv)
    (y_prompt, p_gdn_conv, p_gdn_S, p_mlstm_C, p_mlstm_n, p_mlstm_m,
     p_fox_k, p_fox_v, p_fox_logf) = trunk(
        x_prompt,
        jnp.zeros((N_A, B, CONV_W - 1, GDN_QKV), dt),
        jnp.zeros((N_A, B, GDN_HEADS, GDN_DK, GDN_DV), dt),
        jnp.zeros((N_B, B, ML_HEADS, ML_DQK, ML_DV), dt),
        jnp.zeros((N_B, B, ML_HEADS, ML_DQK), dt),
        jnp.zeros((N_B, B, ML_HEADS), dt),
        jnp.zeros((N_C, B, 0, FOX_HEADS, FOX_DH), dt),
        jnp.zeros((N_C, B, 0, FOX_HEADS, FOX_DH), dt),
        jnp.zeros((N_C, B, 0, FOX_HEADS), dt),
        p_mem_k, p_mem_v, weights)
    (y_sample, s_gdn_conv, s_gdn_S, s_mlstm_C, s_mlstm_n, s_mlstm_m,
     s_fox_k, s_fox_v, s_fox_logf) = trunk(
        x_sample, state_gdn_conv, state_gdn_S, state_mlstm_C, state_mlstm_n, state_mlstm_m,
        cache_fox_k, cache_fox_v, cache_fox_logf, cache_mem_k, cache_mem_v, weights)
    return (y_prompt, y_sample,
            p_gdn_conv, p_gdn_S, p_mlstm_C, p_mlstm_n, p_mlstm_m, p_fox_k, p_fox_v, p_fox_logf,
            p_mem_k, p_mem_v,
            s_gdn_conv, s_gdn_S, s_mlstm_C, s_mlstm_n, s_mlstm_m, s_fox_k, s_fox_v, s_fox_logf)
```

```python
import functools

import numpy as np
import jax
import jax.numpy as jnp
from jax import lax
from jax.experimental import pallas as pl
from jax.experimental.pallas import tpu as pltpu

F32 = jnp.float32
BF16 = jnp.bfloat16

D_MODEL = 1024
DEPTH = 4
N_MIXERS = 3
EPS = 1e-6
CONV_W = 4

GDN_HEADS = 8
GDN_DK = 128
GDN_DV = 128
GDN_QKV = 3072
GDN_MAIN = 4096

ML_HEADS = 4
ML_DQK = 128
ML_DV = 256
ML_MAIN = 3072
GATE_CAP = 15.0

FOX_HEADS = 16
FOX_DH = 64
FOX_MAIN = 4096

N_MEM = 256
X_HEADS = 4
X_DH = 256
D_FF = 2816

LANES = 128
NEG_BIG = -1e30

VMEM_LIMIT = 48 * 1024 * 1024


def _cparams(sem):
    return pltpu.CompilerParams(dimension_semantics=sem, vmem_limit_bytes=VMEM_LIMIT)


def _rms(x, g):
    return x * lax.rsqrt(jnp.mean(x * x, axis=-1, keepdims=True) + EPS) * g


def _sigmoid(x):
    return 1.0 / (1.0 + jnp.exp(-x))


def _dot(a, b):
    return jnp.dot(a.astype(BF16), b.astype(BF16), preferred_element_type=F32)


def _dot_nt(a, b):
    return lax.dot_general(a.astype(BF16), b.astype(BF16), (((1,), (1,)), ((), ())),
                           preferred_element_type=F32)


def _dot_tn(a, b):
    return lax.dot_general(a.astype(BF16), b.astype(BF16), (((0,), (0,)), ((), ())),
                           preferred_element_type=F32)


def _split2(x):
    hi = x.astype(BF16)
    lo = (x - hi.astype(F32)).astype(BF16)
    return hi, lo


def _dot_x3(a, b):
    ah, al = _split2(a)
    bh, bl = _split2(b)
    d = lambda u, v: jnp.dot(u, v, preferred_element_type=F32)
    return d(ah, bh) + (d(ah, bl) + d(al, bh))


def _cumsum_rows(tril_b, x):
    h = x.astype(BF16)
    r = x - h.astype(F32)
    m = r.astype(BF16)
    l = (r - m.astype(F32)).astype(BF16)
    d = lambda v: jnp.dot(tril_b, v, preferred_element_type=F32)
    return d(h) + (d(m) + d(l))


def _tri_masks(L):
    row = lax.broadcasted_iota(jnp.int32, (L, L), 0)
    col = lax.broadcasted_iota(jnp.int32, (L, L), 1)
    return row >= col, row > col, row == col


def _prenorm_mm_body(has_small, *refs):
    if has_small:
        x_ref, g_ref, w_ref, ws_ref, o_ref, os_ref, xn_ref = refs
    else:
        x_ref, g_ref, w_ref, o_ref, xn_ref = refs

    @pl.when(pl.program_id(1) == 0)
    def _():
        xb = _rms(x_ref[...], g_ref[...]).astype(BF16)
        xn_ref[...] = xb
        if has_small:
            os_ref[...] = jnp.dot(xb, ws_ref[...], preferred_element_type=F32)

    o_ref[...] = jnp.dot(xn_ref[...], w_ref[...], preferred_element_type=F32)


def _prenorm_mm(x, gain, w_main, w_small=None, tm=1024, tn=512):
    M, D = x.shape
    N = w_main.shape[1]
    tm = min(tm, M)
    tn = min(tn, N)
    assert M % tm == 0 and N % tn == 0
    has_small = w_small is not None
    in_specs = [pl.BlockSpec((tm, D), lambda i, j: (i, 0)),
                pl.BlockSpec((1, D), lambda i, j: (0, 0)),
                pl.BlockSpec((D, tn), lambda i, j: (0, j))]
    out_specs = [pl.BlockSpec((tm, tn), lambda i, j: (i, j))]
    out_shape = [jax.ShapeDtypeStruct((M, N), F32)]
    args = [x, gain.reshape(1, D), w_main]
    if has_small:
        in_specs.append(pl.BlockSpec((D, LANES), lambda i, j: (0, 0)))
        out_specs.append(pl.BlockSpec((tm, LANES), lambda i, j: (i, 0)))
        out_shape.append(jax.ShapeDtypeStruct((M, LANES), F32))
        args.append(w_small)
    res = pl.pallas_call(
        functools.partial(_prenorm_mm_body, has_small),
        grid=(M // tm, N // tn),
        in_specs=in_specs, out_specs=out_specs, out_shape=out_shape,
        scratch_shapes=[pltpu.VMEM((tm, D), BF16)],
        compiler_params=_cparams(("parallel", "arbitrary")),
        name="prenorm_mm",
    )(*args)
    return res if has_small else res[0]


def _out_resid_body(o_ref, w_ref, g_ref, x_ref, y_ref):
    y = jnp.dot(o_ref[...], w_ref[...], preferred_element_type=F32)
    y_ref[...] = x_ref[...] + _rms(y, g_ref[...])


def _out_resid(o, w, gain, x, tm=512):
    M, K = o.shape
    D = w.shape[1]
    tm = min(tm, M)
    assert M % tm == 0
    return pl.pallas_call(
        _out_resid_body,
        grid=(M // tm,),
        in_specs=[pl.BlockSpec((tm, K), lambda i: (i, 0)),
                  pl.BlockSpec((K, D), lambda i: (0, 0)),
                  pl.BlockSpec((1, D), lambda i: (0, 0)),
                  pl.BlockSpec((tm, D), lambda i: (i, 0))],
        out_specs=pl.BlockSpec((tm, D), lambda i: (i, 0)),
        out_shape=jax.ShapeDtypeStruct((M, D), F32),
        compiler_params=_cparams(("parallel",)),
        name="out_resid",
    )(o, w, gain.reshape(1, D), x)


def _memattn_body(x_ref, g2_ref, wq_ref, mk_ref, mv_ref, wo_ref, g3_ref, y_ref):
    x = x_ref[...]
    h = _rms(x, g2_ref[...]).astype(BF16)
    q = jnp.dot(h, wq_ref[...], preferred_element_type=F32).astype(BF16)
    mk = mk_ref[...].astype(BF16)
    mv = mv_ref[...].astype(BF16)
    outs = []
    for hd in range(X_HEADS):
        sl = slice(hd * X_DH, (hd + 1) * X_DH)
        s = _dot_nt(q[:, sl], mk[:, sl]) * (X_DH ** -0.5)
        p = jnp.exp(s - jnp.max(s, axis=-1, keepdims=True))
        p = p * (1.0 / jnp.sum(p, axis=-1, keepdims=True))
        outs.append(jnp.dot(p.astype(BF16), mv[:, sl], preferred_element_type=F32))
    o = jnp.concatenate(outs, axis=-1).astype(BF16)
    y = jnp.dot(o, wo_ref[...], preferred_element_type=F32)
    y_ref[...] = x + _rms(y, g3_ref[...])


def _memattn(x, g2, wq, mk_arr, kcol, mv_arr, vcol, wo, g3, tq=512):
    B, T, D = x.shape
    tq = min(tq, T)
    assert T % tq == 0
    HD = X_HEADS * X_DH
    return pl.pallas_call(
        _memattn_body,
        grid=(B, T // tq),
        in_specs=[pl.BlockSpec((None, tq, D), lambda b, i: (b, i, 0)),
                  pl.BlockSpec((1, D), lambda b, i: (0, 0)),
                  pl.BlockSpec((D, HD), lambda b, i: (0, 0)),
                  pl.BlockSpec((None, N_MEM, HD), lambda b, i: (b, 0, kcol)),
                  pl.BlockSpec((None, N_MEM, HD), lambda b, i: (b, 0, vcol)),
                  pl.BlockSpec((HD, D), lambda b, i: (0, 0)),
                  pl.BlockSpec((1, D), lambda b, i: (0, 0))],
        out_specs=pl.BlockSpec((None, tq, D), lambda b, i: (b, i, 0)),
        out_shape=jax.ShapeDtypeStruct((B, T, D), F32),
        compiler_params=_cparams(("parallel", "parallel")),
        name="memattn",
    )(x, g2.reshape(1, D), wq, mk_arr, mv_arr, wo, g3.reshape(1, D))


def _ffn_body(nf, x_ref, g4_ref, wg_ref, wu_ref, wd_ref, g5_ref, y_ref, xn_ref, acc_ref):
    f = pl.program_id(1)

    @pl.when(f == 0)
    def _():
        xn_ref[...] = _rms(x_ref[...], g4_ref[...]).astype(BF16)
        acc_ref[...] = jnp.zeros_like(acc_ref)

    xn = xn_ref[...]
    g = jnp.dot(xn, wg_ref[...], preferred_element_type=F32)
    u = jnp.dot(xn, wu_ref[...], preferred_element_type=F32)
    a = (g * _sigmoid(g)) * u
    acc_ref[...] += jnp.dot(a.astype(BF16), wd_ref[...], preferred_element_type=F32)

    @pl.when(f == nf - 1)
    def _():
        y_ref[...] = x_ref[...] + _rms(acc_ref[...], g5_ref[...])


def _ffn(x, g4, w_gu, w_down, g5, tm=1024, tf=256):
    M, D = x.shape
    tm = min(tm, M)
    assert M % tm == 0 and D_FF % tf == 0
    nf = D_FF // tf
    return pl.pallas_call(
        functools.partial(_ffn_body, nf),
        grid=(M // tm, nf),
        in_specs=[pl.BlockSpec((tm, D), lambda i, f: (i, 0)),
                  pl.BlockSpec((1, D), lambda i, f: (0, 0)),
                  pl.BlockSpec((D, tf), lambda i, f: (0, f)),
                  pl.BlockSpec((D, tf), lambda i, f: (0, nf + f)),
                  pl.BlockSpec((tf, D), lambda i, f: (f, 0)),
                  pl.BlockSpec((1, D), lambda i, f: (0, 0))],
        out_specs=pl.BlockSpec((tm, D), lambda i, f: (i, 0)),
        out_shape=jax.ShapeDtypeStruct((M, D), F32),
        scratch_shapes=[pltpu.VMEM((tm, D), BF16), pltpu.VMEM((tm, D), F32)],
        compiler_params=_cparams(("parallel", "arbitrary")),
        name="ffn",
    )(x, g4.reshape(1, D), w_gu, w_gu, w_down, g5.reshape(1, D))


def _neumann_inverse(A, eye, L):
    P = -A
    T = eye + P
    k = 1
    while k < L // 2:
        P = _dot_x3(P, P)
        T = T + _dot_x3(T, P)
        k *= 2
    return T


def _gdn_body(L, NC, qkv_ref, gate_ref, small_ref, buf_ref, S0_ref, wc_ref, par_ref, on_ref,
              o_ref, nbuf_ref, Sout_ref, xbuf, S_scr):
    c = pl.program_id(1)

    @pl.when(c == 0)
    def _():
        xbuf[0:8, :] = jnp.zeros((8, GDN_QKV), F32)
        xbuf[5:8, :] = buf_ref[...]
        S_scr[...] = S0_ref[...]

    xbuf[8:8 + L, :] = qkv_ref[...]
    y = xbuf[5:5 + L, :] * wc_ref[0:1, :]
    for j in range(1, CONV_W):
        y = y + xbuf[5 + j:5 + j + L, :] * wc_ref[j:j + 1, :]
    y = y * _sigmoid(y)

    @pl.when(c == NC - 1)
    def _():
        nbuf_ref[...] = xbuf[L + 5:L + 8, :]

    xbuf[0:8, :] = xbuf[L:L + 8, :]

    small = small_ref[...]
    beta_all = _sigmoid(small)
    g_all = -jnp.exp(par_ref[0:1, :]) * jax.nn.softplus(small + par_ref[1:2, :])
    incl, strict, diag = _tri_masks(L)
    tril_b = jnp.where(incl, 1.0, 0.0).astype(BF16)
    eye = jnp.where(diag, 1.0, 0.0).astype(F32)
    G_all = _cumsum_rows(tril_b, g_all)
    GT = G_all.T
    gate = gate_ref[...]
    onorm = on_ref[...]

    for h in range(GDN_HEADS):
        sl = slice(h * GDN_DK, (h + 1) * GDN_DK)
        qh = y[:, sl]
        kh = y[:, GDN_HEADS * GDN_DK + h * GDN_DK:GDN_HEADS * GDN_DK + (h + 1) * GDN_DK]
        vh = y[:, 2 * GDN_HEADS * GDN_DK + h * GDN_DV:2 * GDN_HEADS * GDN_DK + (h + 1) * GDN_DV]
        qh = qh * lax.rsqrt(jnp.sum(qh * qh, axis=-1, keepdims=True) + EPS) * (GDN_DK ** -0.5)
        kh = kh * lax.rsqrt(jnp.sum(kh * kh, axis=-1, keepdims=True) + EPS)
        bc = beta_all[:, h:h + 1]
        Gc = G_all[:, GDN_HEADS + h:GDN_HEADS + h + 1]
        Gr = GT[GDN_HEADS + h:GDN_HEADS + h + 1, :]
        eG = jnp.exp(Gc)
        dec = jnp.exp(jnp.where(incl, Gc - Gr, NEG_BIG))
        kb = kh.astype(BF16)
        kk = _dot_nt(kb, kb)
        qk = _dot_nt(qh, kb)
        A = jnp.where(strict, bc * kk * dec, 0.0)
        T = _neumann_inverse(A, eye, L)
        rhs = jnp.concatenate([bc * vh, (bc * eG) * kh], axis=-1)
        sol = _dot_x3(T, rhs)
        S = S_scr[h]
        U = sol[:, :GDN_DV] - _dot(sol[:, GDN_DV:], S)
        o = eG * _dot(qh, S) + _dot(qk * dec, U)
        GL = Gc[L - 1:L, :]
        S_scr[h] = jnp.exp(GL) * S + _dot_tn(kh * jnp.exp(GL - Gc), U)
        on = _rms(o, onorm)
        gh = gate[:, h * GDN_DV:(h + 1) * GDN_DV]
        o_ref[:, h * GDN_DV:(h + 1) * GDN_DV] = (on * (gh * _sigmoid(gh))).astype(BF16)

    @pl.when(c == NC - 1)
    def _():
        Sout_ref[...] = S_scr[...]


def _gdn_mixer(main, small, conv_buf, S0, wc_t, par, onorm, L):
    B, T, _ = main.shape
    L = min(L, T)
    assert T % L == 0
    NC = T // L
    return pl.pallas_call(
        functools.partial(_gdn_body, L, NC),
        grid=(B, NC),
        in_specs=[pl.BlockSpec((None, L, GDN_QKV), lambda b, c: (b, c, 0)),
                  pl.BlockSpec((None, L, 1024), lambda b, c: (b, c, 3)),
                  pl.BlockSpec((None, L, LANES), lambda b, c: (b, c, 0)),
                  pl.BlockSpec((None, CONV_W - 1, GDN_QKV), lambda b, c: (b, 0, 0)),
                  pl.BlockSpec((None, GDN_HEADS, GDN_DK, GDN_DV), lambda b, c: (b, 0, 0, 0)),
                  pl.BlockSpec((CONV_W, GDN_QKV), lambda b, c: (0, 0)),
                  pl.BlockSpec((2, LANES), lambda b, c: (0, 0)),
                  pl.BlockSpec((1, GDN_DV), lambda b, c: (0, 0))],
        out_specs=[pl.BlockSpec((None, L, 1024), lambda b, c: (b, c, 0)),
                   pl.BlockSpec((None, CONV_W - 1, GDN_QKV), lambda b, c: (b, 0, 0)),
                   pl.BlockSpec((None, GDN_HEADS, GDN_DK, GDN_DV), lambda b, c: (b, 0, 0, 0))],
        out_shape=[jax.ShapeDtypeStruct((B, T, 1024), BF16),
                   jax.ShapeDtypeStruct((B, CONV_W - 1, GDN_QKV), F32),
                   jax.ShapeDtypeStruct((B, GDN_HEADS, GDN_DK, GDN_DV), F32)],
        scratch_shapes=[pltpu.VMEM((L + 8, GDN_QKV), F32),
                        pltpu.VMEM((GDN_HEADS, GDN_DK, GDN_DV), F32)],
        compiler_params=_cparams(("parallel", "arbitrary")),
        name="gdn_mixer",
    )(main, main, small, conv_buf, S0, wc_t, par, onorm.reshape(1, GDN_DV))


def _mlstm_body(L, NC, qk_ref, v_ref, op_ref, small_ref, C0_ref, n0_ref, m0_ref, bias_ref, on_ref,
                o_ref, Cout_ref, nout_ref, mout_ref, C_scr, n_scr, m_scr):
    c = pl.program_id(1)

    @pl.when(c == 0)
    def _():
        C_scr[...] = C0_ref[...]
        n_scr[...] = n0_ref[...]
        m_scr[...] = m0_ref[...]

    z = small_ref[...] + bias_ref[...]
    sc = GATE_CAP * jnp.tanh(z / GATE_CAP)
    ig_all = sc
    lf_all = jax.nn.log_sigmoid(sc)
    incl, _, _ = _tri_masks(L)
    tril_b = jnp.where(incl, 1.0, 0.0).astype(BF16)
    b_all = _cumsum_rows(tril_b, lf_all)
    igT = ig_all.T
    bT = b_all.T
    qkx = qk_ref[...]
    vx = v_ref[...]
    opre = op_ref[...]
    onorm = on_ref[...]

    for h in range(ML_HEADS):
        qh = qkx[:, h * ML_DQK:(h + 1) * ML_DQK]
        kh = qkx[:, ML_HEADS * ML_DQK + h * ML_DQK:ML_HEADS * ML_DQK + (h + 1) * ML_DQK] * (ML_DQK ** -0.5)
        vh = vx[:, h * ML_DV:(h + 1) * ML_DV]
        ig_c = ig_all[:, h:h + 1]
        b_c = b_all[:, ML_HEADS + h:ML_HEADS + h + 1]
        r_r = igT[h:h + 1, :] - bT[ML_HEADS + h:ML_HEADS + h + 1, :]
        m_prev = m_scr[h:h + 1, 0:1]
        n_prev = n_scr[h:h + 1, :]
        C = C_scr[h]
        D = jnp.where(incl, b_c + r_r, NEG_BIG)
        inter = b_c + m_prev
        mt = jnp.maximum(inter, jnp.max(D, axis=-1, keepdims=True))
        W = jnp.exp(D - mt) * _dot_nt(qh, kh)
        s_inter = jnp.exp(inter - mt)
        num = s_inter * _dot(qh, C) + _dot(W, vh)
        den = s_inter * jnp.sum(qh * n_prev, axis=-1, keepdims=True) + jnp.sum(W, axis=-1, keepdims=True)
        hh = num / jnp.maximum(jnp.abs(den), jnp.exp(-mt))
        mL = mt[L - 1:L, :]
        bL = b_c[L - 1:L, :]
        wk = jnp.exp(bL - b_c + ig_c - mL) * kh
        fL = jnp.exp(bL + m_prev - mL)
        C_scr[h] = fL * C + _dot_tn(wk, vh)
        n_scr[h:h + 1, :] = fL * n_prev + jnp.sum(wk, axis=0, keepdims=True)
        m_scr[h:h + 1, :] = jnp.broadcast_to(mL, (1, LANES))
        hn = _rms(hh, onorm)
        og = opre[:, h * ML_DV:(h + 1) * ML_DV]
        o_ref[:, h * ML_DV:(h + 1) * ML_DV] = (_sigmoid(og) * hn).astype(BF16)

    @pl.when(c == NC - 1)
    def _():
        Cout_ref[...] = C_scr[...]
        nout_ref[...] = n_scr[...]
        mout_ref[...] = m_scr[...]


def _mlstm_mixer(main, small, C0, n0, m0, bias, onorm, L):
    B, T, _ = main.shape
    L = min(L, T)
    assert T % L == 0
    NC = T // L
    m0b = jnp.broadcast_to(m0[:, :, None], (B, ML_HEADS, LANES))
    return pl.pallas_call(
        functools.partial(_mlstm_body, L, NC),
        grid=(B, NC),
        in_specs=[pl.BlockSpec((None, L, 1024), lambda b, c: (b, c, 0)),
                  pl.BlockSpec((None, L, 1024), lambda b, c: (b, c, 1)),
                  pl.BlockSpec((None, L, 1024), lambda b, c: (b, c, 2)),
                  pl.BlockSpec((None, L, LANES), lambda b, c: (b, c, 0)),
                  pl.BlockSpec((None, ML_HEADS, ML_DQK, ML_DV), lambda b, c: (b, 0, 0, 0)),
                  pl.BlockSpec((None, ML_HEADS, ML_DQK), lambda b, c: (b, 0, 0)),
                  pl.BlockSpec((None, ML_HEADS, LANES), lambda b, c: (b, 0, 0)),
                  pl.BlockSpec((1, LANES), lambda b, c: (0, 0)),
                  pl.BlockSpec((1, ML_DV), lambda b, c: (0, 0))],
        out_specs=[pl.BlockSpec((None, L, 1024), lambda b, c: (b, c, 0)),
                   pl.BlockSpec((None, ML_HEADS, ML_DQK, ML_DV), lambda b, c: (b, 0, 0, 0)),
                   pl.BlockSpec((None, ML_HEADS, ML_DQK), lambda b, c: (b, 0, 0)),
                   pl.BlockSpec((None, ML_HEADS, LANES), lambda b, c: (b, 0, 0))],
        out_shape=[jax.ShapeDtypeStruct((B, T, 1024), BF16),
                   jax.ShapeDtypeStruct((B, ML_HEADS, ML_DQK, ML_DV), F32),
                   jax.ShapeDtypeStruct((B, ML_HEADS, ML_DQK), F32),
                   jax.ShapeDtypeStruct((B, ML_HEADS, LANES), F32)],
        scratch_shapes=[pltpu.VMEM((ML_HEADS, ML_DQK, ML_DV), F32),
                        pltpu.VMEM((ML_HEADS, ML_DQK), F32),
                        pltpu.VMEM((ML_HEADS, LANES), F32)],
        compiler_params=_cparams(("parallel", "arbitrary")),
        name="mlstm_mixer",
    )(main, main, main, small, C0, n0, m0b, bias, onorm.reshape(1, ML_DV))


def _fox_prep_body(q_ref, k_ref, v_ref, small_ref, bf_ref, qn_ref, kn_ref,
                   qs_ref, k32_ref, kb_ref, v32_ref, vb_ref, lf_ref):
    r = lax.broadcasted_iota(jnp.int32, (LANES, LANES), 0) // FOX_DH
    cc = lax.broadcasted_iota(jnp.int32, (LANES, LANES), 1) // FOX_DH
    avg = jnp.where(r == cc, 1.0 / FOX_DH, 0.0).astype(BF16)

    def head_rms(x, gain):
        parts = []
        for t in range(x.shape[1] // LANES):
            xt = x[:, t * LANES:(t + 1) * LANES]
            hi, lo = _split2(xt * xt)
            ms = (jnp.dot(hi, avg, preferred_element_type=F32)
                  + jnp.dot(lo, avg, preferred_element_type=F32))
            parts.append(xt * lax.rsqrt(ms + EPS))
        return jnp.concatenate(parts, axis=-1) * gain

    qn = head_rms(q_ref[...], qn_ref[...])
    qs_ref[...] = (qn * (FOX_DH ** -0.5)).astype(BF16)
    kn = head_rms(k_ref[...], kn_ref[...])
    k32_ref[...] = kn
    kb_ref[...] = kn.astype(BF16)
    v = v_ref[...]
    v32_ref[...] = v
    vb_ref[...] = v.astype(BF16)
    lf_ref[...] = jax.nn.log_sigmoid(small_ref[...] + bf_ref[...])


def _fox_prep(main, small, bf_row, qn_row, kn_row, tm=512):
    B, T, _ = main.shape
    tm = min(tm, T)
    assert T % tm == 0
    blk = lambda j: pl.BlockSpec((None, tm, 1024), lambda b, i: (b, i, j))
    row = lambda n: pl.BlockSpec((1, n), lambda b, i: (0, 0))
    o1024 = pl.BlockSpec((None, tm, 1024), lambda b, i: (b, i, 0))
    return pl.pallas_call(
        _fox_prep_body,
        grid=(B, T // tm),
        in_specs=[blk(0), blk(1), blk(2),
                  pl.BlockSpec((None, tm, LANES), lambda b, i: (b, i, 0)),
                  row(LANES), row(1024), row(1024)],
        out_specs=[o1024, o1024, o1024, o1024, o1024,
                   pl.BlockSpec((None, tm, LANES), lambda b, i: (b, i, 0))],
        out_shape=[jax.ShapeDtypeStruct((B, T, 1024), BF16),
                   jax.ShapeDtypeStruct((B, T, 1024), F32),
                   jax.ShapeDtypeStruct((B, T, 1024), BF16),
                   jax.ShapeDtypeStruct((B, T, 1024), F32),
                   jax.ShapeDtypeStruct((B, T, 1024), BF16),
                   jax.ShapeDtypeStruct((B, T, LANES), F32)],
        compiler_params=_cparams(("parallel", "parallel")),
        name="fox_prep",
    )(main, main, main, small, bf_row, qn_row, kn_row)


def _cumsum_t_body(tc, lf_ref, ft_ref, carry):
    @pl.when(pl.program_id(1) == 0)
    def _():
        carry[...] = jnp.zeros_like(carry)

    incl, _, _ = _tri_masks(tc)
    tril_b = jnp.where(incl, 1.0, 0.0).astype(BF16)
    F = _cumsum_rows(tril_b, lf_ref[...]) + carry[0:1, :]
    carry[0:1, :] = F[tc - 1:tc, :]
    ft_ref[...] = F.T


def _cumsum_t(lf_all, tc=128):
    B, N, _ = lf_all.shape
    assert N % tc == 0
    return pl.pallas_call(
        functools.partial(_cumsum_t_body, tc),
        grid=(B, N // tc),
        in_specs=[pl.BlockSpec((None, tc, LANES), lambda b, i: (b, i, 0))],
        out_specs=pl.BlockSpec((None, LANES, tc), lambda b, i: (b, 0, i)),
        out_shape=jax.ShapeDtypeStruct((B, LANES, N), F32),
        scratch_shapes=[pltpu.VMEM((8, LANES), F32)],
        compiler_params=_cparams(("parallel", "arbitrary")),
        name="fox_cumsum",
    )(lf_all)


def _flash_body(P, tq, tk, qi_tab, kj_tab, q_ref, k_ref, v_ref, fk_ref, fq_ref, gate_ref,
                o_ref, m_scr, l_scr, acc_scr):
    s_idx = pl.program_id(2)
    qi = qi_tab[s_idx]
    kj = kj_tab[s_idx]
    last_kj = (P + (qi + 1) * tq - 1) // tk
    lane_lo = lax.broadcasted_iota(jnp.int32, (1, LANES), 1) < FOX_DH

    @pl.when(kj == 0)
    def _():
        m_scr[...] = jnp.full(m_scr.shape, NEG_BIG, F32)
        l_scr[...] = jnp.zeros_like(l_scr)
        acc_scr[...] = jnp.zeros_like(acc_scr)

    def step(masked):
        q = q_ref[...]
        k = k_ref[...]
        v = v_ref[...]
        fk = fk_ref[...]
        cq = fq_ref[:, 0:1]
        if masked:
            qpos = P + qi * tq + lax.broadcasted_iota(jnp.int32, (tq, tk), 0)
            kpos = kj * tk + lax.broadcasted_iota(jnp.int32, (tq, tk), 1)
            ok = kpos <= qpos
        for hh in range(2):
            qm = jnp.where(lane_lo if hh == 0 else jnp.logical_not(lane_lo), q, jnp.zeros_like(q))
            s = _dot_nt(qm, k) + (cq[hh:hh + 1, :] - fk[hh:hh + 1, :])
            if masked:
                s = jnp.where(ok, s, NEG_BIG)
            m_prev = m_scr[hh][:, 0:1]
            l_prev = l_scr[hh][:, 0:1]
            m_new = jnp.maximum(m_prev, jnp.max(s, axis=-1, keepdims=True))
            alpha = jnp.exp(m_prev - m_new)
            p = jnp.exp(s - m_new)
            l_new = alpha * l_prev + jnp.sum(p, axis=-1, keepdims=True)
            acc_scr[hh] = alpha * acc_scr[hh] + jnp.dot(p.astype(BF16), v, preferred_element_type=F32)
            m_scr[hh] = jnp.broadcast_to(m_new, (tq, LANES))
            l_scr[hh] = jnp.broadcast_to(l_new, (tq, LANES))

    needs_mask = (kj + 1) * tk - 1 > P + qi * tq

    @pl.when(needs_mask)
    def _():
        step(True)

    @pl.when(jnp.logical_not(needs_mask))
    def _():
        step(False)

    @pl.when(kj == last_kj)
    def _():
        o = jnp.where(lane_lo, acc_scr[0] / l_scr[0], acc_scr[1] / l_scr[1])
        g = gate_ref[...]
        o_ref[...] = (o * _sigmoid(g)).astype(BF16)


def _fox_flash(qs, k_all, v_all, ft, gate_src, P, tq, tk):
    B, T, _ = qs.shape
    N = k_all.shape[1]
    assert T % tq == 0 and N % tk == 0 and P % LANES == 0 and (tq % LANES == 0 or T == tq)
    nq = T // tq
    pairs = [(qi, kj) for qi in range(nq) for kj in range((P + (qi + 1) * tq - 1) // tk + 1)]
    qi_tab = jnp.asarray(np.array([p[0] for p in pairs], np.int32))
    kj_tab = jnp.asarray(np.array([p[1] for p in pairs], np.int32))
    HP = FOX_HEADS // 2
    gate_blk0 = 3 * (1024 // LANES)
    grid_spec = pltpu.PrefetchScalarGridSpec(
        num_scalar_prefetch=2,
        grid=(B, HP, len(pairs)),
        in_specs=[pl.BlockSpec((None, tq, LANES), lambda b, p, s, qt, kt: (b, qt[s], p)),
                  pl.BlockSpec((None, tk, LANES), lambda b, p, s, qt, kt: (b, kt[s], p)),
                  pl.BlockSpec((None, tk, LANES), lambda b, p, s, qt, kt: (b, kt[s], p)),
                  pl.BlockSpec((None, None, 2, tk), lambda b, p, s, qt, kt: (b, p, 0, kt[s])),
                  pl.BlockSpec((None, None, 2, LANES),
                               lambda b, p, s, qt, kt: (b, p, 0, (P + qt[s] * tq) // LANES)),
                  pl.BlockSpec((None, tq, LANES), lambda b, p, s, qt, kt: (b, qt[s], gate_blk0 + p))],
        out_specs=pl.BlockSpec((None, tq, LANES), lambda b, p, s, qt, kt: (b, qt[s], p)),
        scratch_shapes=[pltpu.VMEM((2, tq, LANES), F32),
                        pltpu.VMEM((2, tq, LANES), F32),
                        pltpu.VMEM((2, tq, LANES), F32)],
    )
    return pl.pallas_call(
        functools.partial(_flash_body, P, tq, tk),
        grid_spec=grid_spec,
        out_shape=jax.ShapeDtypeStruct((B, T, 1024), BF16),
        compiler_params=_cparams(("parallel", "parallel", "arbitrary")),
        name="fox_flash",
    )(qi_tab, kj_tab, qs, k_all, v_all, ft, ft, gate_src)


def _pad_lanes(a, offset=0):
    n = a.shape[-1]
    pads = [(0, 0)] * (a.ndim - 1) + [(offset, LANES - n - offset)]
    return jnp.pad(a, pads)


def _trunk(x, gdn_conv, gdn_S, ml_C, ml_n, ml_m, fox_k, fox_v, fox_lf, mem_k, mem_v, W, chunk):
    B, T, D = x.shape
    M = B * T
    x2 = x.reshape(M, D)
    conv_o, S_o, C_o, n_o, m_o, k_o, v_o, lf_o = [], [], [], [], [], [], [], []
    for layer in range(DEPTH):
        kind, j = layer % N_MIXERS, layer // N_MIXERS
        gains = W["norm_gains"][layer]
        if kind == 0:
            main, small = _prenorm_mm(x2, gains[0], W["gdn_w_main"][j], W["gdn_w_small"][j])
            o, buf, S = _gdn_mixer(main.reshape(B, T, GDN_MAIN), small.reshape(B, T, LANES),
                                   gdn_conv[j], gdn_S[j], W["gdn_wc_t"][j], W["gdn_par"][j],
                                   W["gdn_onorm"][j], chunk)
            conv_o.append(buf)
            S_o.append(S)
            w_out = W["gdn_w_out"][j]
        elif kind == 1:
            main, small = _prenorm_mm(x2, gains[0], W["ml_w_main"][j], W["ml_w_small"][j])
            o, C, n, m = _mlstm_mixer(main.reshape(B, T, ML_MAIN), small.reshape(B, T, LANES),
                                      ml_C[j], ml_n[j], ml_m[j], W["ml_bias"][j], W["ml_onorm"][j], chunk)
            C_o.append(C)
            n_o.append(n)
            m_o.append(m[:, :, 0])
            w_out = W["ml_w_out"][j]
        else:
            main, small = _prenorm_mm(x2, gains[0], W["fox_w_main"][j], W["fox_w_small"][j])
            main3 = main.reshape(B, T, FOX_MAIN)
            qs, k32, kb, v32, vb, lfp = _fox_prep(main3, small.reshape(B, T, LANES), W["fox_bf"][j],
                                                  W["fox_qn"][j], W["fox_kn"][j])
            P = fox_k[j].shape[1]
            if P == 0:
                k_all, v_all, lf_all, tq, tk = kb, vb, lfp, 512, 512
            else:
                npad = -(P + T) % LANES
                zkv = jnp.zeros((B, npad, 1024), BF16)
                k_all = jnp.concatenate([fox_k[j].reshape(B, P, 1024).astype(BF16), kb, zkv], axis=1)
                v_all = jnp.concatenate([fox_v[j].reshape(B, P, 1024).astype(BF16), vb, zkv], axis=1)
                lf_all = jnp.concatenate([_pad_lanes(fox_lf[j]), lfp, jnp.zeros((B, npad, LANES), F32)], axis=1)
                tq, tk = T, P + T + npad
            ft = _cumsum_t(lf_all)
            ft = ft[:, :FOX_HEADS, :].reshape(B, FOX_HEADS // 2, 2, ft.shape[-1])
            o = _fox_flash(qs, k_all, v_all, ft, main3, P, tq, tk)
            k_o.append(k32.reshape(B, T, FOX_HEADS, FOX_DH))
            v_o.append(v32.reshape(B, T, FOX_HEADS, FOX_DH))
            lf_o.append(lfp[:, :, :FOX_HEADS])
            w_out = W["fox_w_out"][j]
        x2 = _out_resid(o.reshape(M, 1024), w_out, gains[1], x2)
        (mk_arr, kcol), (mv_arr, vcol) = mem_k[layer], mem_v[layer]
        x2 = _memattn(x2.reshape(B, T, D), gains[2], W["x_w_q"][layer], mk_arr, kcol, mv_arr, vcol,
                      W["x_w_o"][layer], gains[3]).reshape(M, D)
        x2 = _ffn(x2, gains[4], W["ffn_w_gu"][layer], W["ffn_w_down"][layer], gains[5])
    return (x2.reshape(B, T, D), jnp.stack(conv_o), jnp.stack(S_o), jnp.stack(C_o), jnp.stack(n_o),
            jnp.stack(m_o), jnp.stack(k_o), jnp.stack(v_o), jnp.stack(lf_o))


def kernel(x_prompt, x_sample, state_gdn_conv, state_gdn_S, state_mlstm_C, state_mlstm_n, state_mlstm_m, cache_fox_k, cache_fox_v, cache_fox_logf, cache_mem_k, cache_mem_v, mem_prompt, norm_gains, gdn_w_in, gdn_w_conv, gdn_a_log, gdn_dt_bias, gdn_onorm, gdn_w_out, ml_w_in, ml_b_i, ml_b_f, ml_onorm, ml_w_out, fox_w_in, fox_b_f, fox_qk_norm, fox_w_out, x_w_q, x_w_kv, x_w_o, ffn_w_gu, ffn_w_down):
    B = x_prompt.shape[0]
    DB = x_sample.shape[0]
    n_a, n_b, n_c = gdn_w_in.shape[0], ml_w_in.shape[0], fox_w_in.shape[0]
    W = {
        "norm_gains": norm_gains,
        "gdn_w_main": gdn_w_in[:, :, :GDN_MAIN].astype(BF16),
        "gdn_w_small": _pad_lanes(gdn_w_in[:, :, GDN_MAIN:]).astype(BF16),
        "gdn_wc_t": jnp.transpose(gdn_w_conv, (0, 2, 1)),
        "gdn_par": jnp.stack([_pad_lanes(gdn_a_log, GDN_HEADS), _pad_lanes(gdn_dt_bias, GDN_HEADS)], axis=1),
        "gdn_onorm": gdn_onorm,
        "gdn_w_out": gdn_w_out.astype(BF16),
        "ml_w_main": ml_w_in[:, :, :ML_MAIN].astype(BF16),
        "ml_w_small": _pad_lanes(ml_w_in[:, :, ML_MAIN:]).astype(BF16),
        "ml_bias": _pad_lanes(jnp.concatenate([ml_b_i, ml_b_f], axis=-1))[:, None, :],
        "ml_onorm": ml_onorm,
        "ml_w_out": ml_w_out.astype(BF16),
        "fox_w_main": fox_w_in[:, :, :FOX_MAIN].astype(BF16),
        "fox_w_small": _pad_lanes(fox_w_in[:, :, FOX_MAIN:]).astype(BF16),
        "fox_bf": _pad_lanes(fox_b_f)[:, None, :],
        "fox_qn": jnp.tile(fox_qk_norm[:, 0, :], (1, FOX_HEADS))[:, None, :],
        "fox_kn": jnp.tile(fox_qk_norm[:, 1, :], (1, FOX_HEADS))[:, None, :],
        "fox_w_out": fox_w_out.astype(BF16),
        "x_w_q": x_w_q.astype(BF16),
        "x_w_o": x_w_o.astype(BF16),
        "ffn_w_gu": ffn_w_gu.astype(BF16),
        "ffn_w_down": ffn_w_down.astype(BF16),
    }

    mem2 = mem_prompt.reshape(B * N_MEM, D_MODEL)
    x_w_kv_b = x_w_kv.astype(BF16)
    kv = [_prenorm_mm(mem2, norm_gains[l, 6], x_w_kv_b[l]).reshape(B, N_MEM, 2 * D_MODEL)
          for l in range(DEPTH)]
    p_mem_k = jnp.stack([a[:, :, :D_MODEL].reshape(B, N_MEM, X_HEADS, X_DH) for a in kv])
    p_mem_v = jnp.stack([a[:, :, D_MODEL:].reshape(B, N_MEM, X_HEADS, X_DH) for a in kv])

    zeros = lambda *s: jnp.zeros(s, F32)
    prompt = _trunk(
        x_prompt,
        zeros(n_a, B, CONV_W - 1, GDN_QKV), zeros(n_a, B, GDN_HEADS, GDN_DK, GDN_DV),
        zeros(n_b, B, ML_HEADS, ML_DQK, ML_DV), zeros(n_b, B, ML_HEADS, ML_DQK), zeros(n_b, B, ML_HEADS),
        zeros(n_c, B, 0, FOX_HEADS, FOX_DH), zeros(n_c, B, 0, FOX_HEADS, FOX_DH), zeros(n_c, B, 0, FOX_HEADS),
        [(a, 0) for a in kv], [(a, 1) for a in kv], W, 64)
    sample = _trunk(
        x_sample, state_gdn_conv, state_gdn_S, state_mlstm_C, state_mlstm_n, state_mlstm_m,
        cache_fox_k, cache_fox_v, cache_fox_logf,
        [(cache_mem_k[l].reshape(DB, N_MEM, D_MODEL), 0) for l in range(DEPTH)],
        [(cache_mem_v[l].reshape(DB, N_MEM, D_MODEL), 0) for l in range(DEPTH)], W, 64)
    (y_p, p_conv, p_S, p_C, p_n, p_m, p_k, p_v, p_lf) = prompt
    (y_s, s_conv, s_S, s_C, s_n, s_m, s_k, s_v, s_lf) = sample
    return (y_p, y_s, p_conv, p_S, p_C, p_n, p_m, p_k, p_v, p_lf, p_mem_k, p_mem_v,
            s_conv, s_S, s_C, s_n, s_m, s_k, s_v, s_lf)
```

```python
import functools

import numpy as np
import jax
import jax.numpy as jnp
from jax import lax
from jax.experimental import pallas as pl
from jax.experimental.pallas import tpu as pltpu

F32 = jnp.float32
BF16 = jnp.bfloat16

D_MODEL = 1024
DEPTH = 4
N_MIXERS = 3
EPS = 1e-6
CONV_W = 4

GDN_HEADS = 8
GDN_DK = 128
GDN_DV = 128
GDN_QKV = 3072
GDN_MAIN = 4096

ML_HEADS = 4
ML_DQK = 128
ML_DV = 256
ML_MAIN = 3072
GATE_CAP = 15.0

FOX_HEADS = 16
FOX_DH = 64
FOX_MAIN = 4096

N_MEM = 256
X_HEADS = 4
X_DH = 256
D_FF = 2816

LANES = 128
NEG_BIG = -1e30

VMEM_LIMIT = 48 * 1024 * 1024


def _cparams(sem):
    return pltpu.CompilerParams(dimension_semantics=sem, vmem_limit_bytes=VMEM_LIMIT)


def _rms(x, g):
    return x * lax.rsqrt(jnp.mean(x * x, axis=-1, keepdims=True) + EPS) * g


def _sigmoid(x):
    return 1.0 / (1.0 + jnp.exp(-x))


def _dot(a, b):
    return jnp.dot(a.astype(BF16), b.astype(BF16), preferred_element_type=F32)


def _dot_nt(a, b):
    return lax.dot_general(a.astype(BF16), b.astype(BF16), (((1,), (1,)), ((), ())),
                           preferred_element_type=F32)


def _dot_tn(a, b):
    return lax.dot_general(a.astype(BF16), b.astype(BF16), (((0,), (0,)), ((), ())),
                           preferred_element_type=F32)


def _split2(x):
    hi = x.astype(BF16)
    lo = (x - hi.astype(F32)).astype(BF16)
    return hi, lo


def _dot_x3(a, b):
    ah, al = _split2(a)
    bh, bl = _split2(b)
    d = lambda u, v: jnp.dot(u, v, preferred_element_type=F32)
    return d(ah, bh) + (d(ah, bl) + d(al, bh))


def _cumsum_rows(tril_b, x):
    h = x.astype(BF16)
    r = x - h.astype(F32)
    m = r.astype(BF16)
    l = (r - m.astype(F32)).astype(BF16)
    d = lambda v: jnp.dot(tril_b, v, preferred_element_type=F32)
    return d(h) + (d(m) + d(l))


def _tri_masks(L):
    row = lax.broadcasted_iota(jnp.int32, (L, L), 0)
    col = lax.broadcasted_iota(jnp.int32, (L, L), 1)
    return row >= col, row > col, row == col


def _prenorm_mm_body(has_small, *refs):
    if has_small:
        x_ref, g_ref, w_ref, ws_ref, o_ref, os_ref, xn_ref = refs
    else:
        x_ref, g_ref, w_ref, o_ref, xn_ref = refs

    @pl.when(pl.program_id(1) == 0)
    def _():
        xb = _rms(x_ref[...], g_ref[...]).astype(BF16)
        xn_ref[...] = xb
        if has_small:
            os_ref[...] = jnp.dot(xb, ws_ref[...], preferred_element_type=F32)

    o_ref[...] = jnp.dot(xn_ref[...], w_ref[...], preferred_element_type=F32)


def _prenorm_mm(x, gain, w_main, w_small=None, tm=1024, tn=512):
    M, D = x.shape
    N = w_main.shape[1]
    tm = min(tm, M)
    tn = min(tn, N)
    assert M % tm == 0 and N % tn == 0
    has_small = w_small is not None
    in_specs = [pl.BlockSpec((tm, D), lambda i, j: (i, 0)),
                pl.BlockSpec((1, D), lambda i, j: (0, 0)),
                pl.BlockSpec((D, tn), lambda i, j: (0, j))]
    out_specs = [pl.BlockSpec((tm, tn), lambda i, j: (i, j))]
    out_shape = [jax.ShapeDtypeStruct((M, N), F32)]
    args = [x, gain.reshape(1, D), w_main]
    if has_small:
        in_specs.append(pl.BlockSpec((D, LANES), lambda i, j: (0, 0)))
        out_specs.append(pl.BlockSpec((tm, LANES), lambda i, j: (i, 0)))
        out_shape.append(jax.ShapeDtypeStruct((M, LANES), F32))
        args.append(w_small)
    res = pl.pallas_call(
        functools.partial(_prenorm_mm_body, has_small),
        grid=(M // tm, N // tn),
        in_specs=in_specs, out_specs=out_specs, out_shape=out_shape,
        scratch_shapes=[pltpu.VMEM((tm, D), BF16)],
        compiler_params=_cparams(("parallel", "arbitrary")),
        name="prenorm_mm",
    )(*args)
    return res if has_small else res[0]


def _out_resid_body(o_ref, w_ref, g_ref, x_ref, y_ref):
    y = jnp.dot(o_ref[...], w_ref[...], preferred_element_type=F32)
    y_ref[...] = x_ref[...] + _rms(y, g_ref[...])


def _out_resid(o, w, gain, x, tm=512):
    M, K = o.shape
    D = w.shape[1]
    tm = min(tm, M)
    assert M % tm == 0
    return pl.pallas_call(
        _out_resid_body,
        grid=(M // tm,),
        in_specs=[pl.BlockSpec((tm, K), lambda i: (i, 0)),
                  pl.BlockSpec((K, D), lambda i: (0, 0)),
                  pl.BlockSpec((1, D), lambda i: (0, 0)),
                  pl.BlockSpec((tm, D), lambda i: (i, 0))],
        out_specs=pl.BlockSpec((tm, D), lambda i: (i, 0)),
        out_shape=jax.ShapeDtypeStruct((M, D), F32),
        compiler_params=_cparams(("parallel",)),
        name="out_resid",
    )(o, w, gain.reshape(1, D), x)


def _memattn_body(x_ref, g2_ref, wq_ref, mk_ref, mv_ref, wo_ref, g3_ref, y_ref):
    x = x_ref[...]
    h = _rms(x, g2_ref[...]).astype(BF16)
    q = jnp.dot(h, wq_ref[...], preferred_element_type=F32).astype(BF16)
    mk = mk_ref[...].astype(BF16)
    mv = mv_ref[...].astype(BF16)
    outs = []
    for hd in range(X_HEADS):
        sl = slice(hd * X_DH, (hd + 1) * X_DH)
        s = _dot_nt(q[:, sl], mk[:, sl]) * (X_DH ** -0.5)
        p = jnp.exp(s - jnp.max(s, axis=-1, keepdims=True))
        p = p * (1.0 / jnp.sum(p, axis=-1, keepdims=True))
        outs.append(jnp.dot(p.astype(BF16), mv[:, sl], preferred_element_type=F32))
    o = jnp.concatenate(outs, axis=-1).astype(BF16)
    y = jnp.dot(o, wo_ref[...], preferred_element_type=F32)
    y_ref[...] = x + _rms(y, g3_ref[...])


def _memattn(x, g2, wq, mk_arr, kcol, mv_arr, vcol, wo, g3, tq=512):
    B, T, D = x.shape
    tq = min(tq, T)
    assert T % tq == 0
    HD = X_HEADS * X_DH
    return pl.pallas_call(
        _memattn_body,
        grid=(B, T // tq),
        in_specs=[pl.BlockSpec((None, tq, D), lambda b, i: (b, i, 0)),
                  pl.BlockSpec((1, D), lambda b, i: (0, 0)),
                  pl.BlockSpec((D, HD), lambda b, i: (0, 0)),
                  pl.BlockSpec((None, N_MEM, HD), lambda b, i: (b, 0, kcol)),
                  pl.BlockSpec((None, N_MEM, HD), lambda b, i: (b, 0, vcol)),
                  pl.BlockSpec((HD, D), lambda b, i: (0, 0)),
                  pl.BlockSpec((1, D), lambda b, i: (0, 0))],
        out_specs=pl.BlockSpec((None, tq, D), lambda b, i: (b, i, 0)),
        out_shape=jax.ShapeDtypeStruct((B, T, D), F32),
        compiler_params=_cparams(("parallel", "parallel")),
        name="memattn",
    )(x, g2.reshape(1, D), wq, mk_arr, mv_arr, wo, g3.reshape(1, D))


def _ffn_body(nf, x_ref, g4_ref, wg_ref, wu_ref, wd_ref, g5_ref, y_ref, xn_ref, acc_ref):
    f = pl.program_id(1)

    @pl.when(f == 0)
    def _():
        xn_ref[...] = _rms(x_ref[...], g4_ref[...]).astype(BF16)
        acc_ref[...] = jnp.zeros_like(acc_ref)

    xn = xn_ref[...]
    g = jnp.dot(xn, wg_ref[...], preferred_element_type=F32)
    u = jnp.dot(xn, wu_ref[...], preferred_element_type=F32)
    a = (g * _sigmoid(g)) * u
    acc_ref[...] += jnp.dot(a.astype(BF16), wd_ref[...], preferred_element_type=F32)

    @pl.when(f == nf - 1)
    def _():
        y_ref[...] = x_ref[...] + _rms(acc_ref[...], g5_ref[...])


def _ffn(x, g4, w_gu, w_down, g5, tm=1024, tf=256):
    M, D = x.shape
    tm = min(tm, M)
    assert M % tm == 0 and D_FF % tf == 0
    nf = D_FF // tf
    return pl.pallas_call(
        functools.partial(_ffn_body, nf),
        grid=(M // tm, nf),
        in_specs=[pl.BlockSpec((tm, D), lambda i, f: (i, 0)),
                  pl.BlockSpec((1, D), lambda i, f: (0, 0)),
                  pl.BlockSpec((D, tf), lambda i, f: (0, f)),
                  pl.BlockSpec((D, tf), lambda i, f: (0, nf + f)),
                  pl.BlockSpec((tf, D), lambda i, f: (f, 0)),
                  pl.BlockSpec((1, D), lambda i, f: (0, 0))],
        out_specs=pl.BlockSpec((tm, D), lambda i, f: (i, 0)),
        out_shape=jax.ShapeDtypeStruct((M, D), F32),
        scratch_shapes=[pltpu.VMEM((tm, D), BF16), pltpu.VMEM((tm, D), F32)],
        compiler_params=_cparams(("parallel", "arbitrary")),
        name="ffn",
    )(x, g4.reshape(1, D), w_gu, w_gu, w_down, g5.reshape(1, D))


GDN_CHUNK = 64
GDN_BASE = 8


def _inverse_masks(L):
    row = lax.broadcasted_iota(jnp.int32, (L, L), 0)
    col = lax.broadcasted_iota(jnp.int32, (L, L), 1)
    base = row // GDN_BASE == col // GDN_BASE
    offs = []
    s = GDN_BASE
    while s < L:
        same_pair = row // (2 * s) == col // (2 * s)
        offs.append(jnp.logical_and(same_pair, (row // s) - (col // s) == 1))
        s *= 2
    return base, row == col, offs


def _unit_lower_inverse(As, masks, mm):
    base, diag, offs = masks
    eye = jnp.where(diag, 1.0, 0.0)
    Ps = [jnp.where(base, -A, 0.0) for A in As]
    Ts = [eye + P for P in Ps]
    k = 1
    while k < GDN_BASE // 2:
        Ps = [mm(P, P) for P in Ps]
        Ts = [T + mm(T, P) for T, P in zip(Ts, Ps)]
        k *= 2
    for off in offs:
        Ms = [mm(T, jnp.where(off, A, 0.0)) for T, A in zip(Ts, As)]
        Ts = [T - mm(M, T) for T, M in zip(Ts, Ms)]
    return Ts


def _gdn_prep_body(L, nch, qkv_ref, prev_ref, small_ref, buf_ref, wc_ref, par_ref,
                   wk_ref, uv_ref, qt_ref, kt_ref, at_ref, egl_ref, xbuf):
    c = pl.program_id(1)
    LA = L * nch

    @pl.when(c == 0)
    def _():
        xbuf[0:8, :] = jnp.zeros((8, GDN_QKV), F32)
        xbuf[5:8, :] = buf_ref[...]

    @pl.when(c > 0)
    def _():
        xbuf[0:8, :] = prev_ref[...]

    xbuf[8:8 + LA, :] = qkv_ref[...]
    y = xbuf[5:5 + LA, :] * wc_ref[0:1, :]
    for j in range(1, CONV_W):
        y = y + xbuf[5 + j:5 + j + LA, :] * wc_ref[j:j + 1, :]
    y = y * _sigmoid(y)

    small = small_ref[...]
    beta_all = _sigmoid(small)
    g_all = -jnp.exp(par_ref[0:1, :]) * jax.nn.softplus(small + par_ref[1:2, :])
    incl, strict, _ = _tri_masks(L)
    tril_b = jnp.where(incl, 1.0, 0.0).astype(BF16)
    masks = _inverse_masks(L)
    H = GDN_HEADS

    units = []
    for ci in range(nch):
        rows = slice(ci * L, (ci + 1) * L)
        G_all = _cumsum_rows(tril_b, g_all[rows])
        GT = G_all.T
        egl_ref[ci] = jnp.broadcast_to(jnp.exp(G_all[L - 1:L, :]), (8, LANES))
        for h in range(H):
            qh = y[rows, h * GDN_DK:(h + 1) * GDN_DK]
            kh = y[rows, (H + h) * GDN_DK:(H + h + 1) * GDN_DK]
            vh = y[rows, 2 * H * GDN_DK + h * GDN_DV:2 * H * GDN_DK + (h + 1) * GDN_DV]
            qh = qh * lax.rsqrt(jnp.sum(qh * qh, axis=-1, keepdims=True) + EPS) * (GDN_DK ** -0.5)
            kh = kh * lax.rsqrt(jnp.sum(kh * kh, axis=-1, keepdims=True) + EPS)
            bc = beta_all[rows, h:h + 1]
            Gc = G_all[:, H + h:H + h + 1]
            Gr = GT[H + h:H + h + 1, :]
            eG = jnp.exp(Gc)
            dec = jnp.exp(jnp.where(incl, Gc - Gr, NEG_BIG))
            qkk = _dot_nt(jnp.concatenate([qh, kh], axis=0), kh)
            A = jnp.where(strict, bc * qkk[L:] * dec, 0.0)
            rhs = jnp.concatenate([bc * vh, (bc * eG) * kh], axis=-1)
            cols = slice(h * GDN_DV, (h + 1) * GDN_DV)
            qt_ref[rows, cols] = (eG * qh).astype(BF16)
            kt_ref[rows, cols] = (kh * jnp.exp(Gc[L - 1:L, :] - Gc)).astype(BF16)
            at_ref[rows, cols] = jnp.concatenate(
                [qkk[:L] * dec, jnp.zeros((L, GDN_DV - L), F32)], axis=-1).astype(BF16)
            units.append((rows, cols, A, rhs))

    Ts = _unit_lower_inverse([u[2] for u in units], masks, _dot_x3)
    for (rows, cols, _, rhs), T in zip(units, Ts):
        sol = _dot_x3(T, rhs)
        uv_ref[rows, cols] = sol[:, :GDN_DV]
        wk_ref[rows, cols] = sol[:, GDN_DV:].astype(BF16)


def _gdn_scan_body(L, cb, NS, wk_ref, uv_ref, qt_ref, kt_ref, at_ref, egl_ref, gate_ref, S0_ref, on_ref,
                   o_ref, Sout_ref, S_scr):
    c = pl.program_id(1)

    @pl.when(c == 0)
    def _():
        S_scr[...] = S0_ref[...]

    onorm = on_ref[...]
    H = GDN_HEADS
    for ci in range(cb):
        rows = slice(ci * L, (ci + 1) * L)
        egl = egl_ref[ci]
        hs = [slice(h * GDN_DV, (h + 1) * GDN_DV) for h in range(H)]
        Ss = [S_scr[h] for h in range(H)]
        Xs = [jnp.dot(jnp.concatenate([wk_ref[rows, hs[h]], qt_ref[rows, hs[h]]], axis=0),
                      Ss[h].astype(BF16), preferred_element_type=F32) for h in range(H)]
        Us = [(uv_ref[rows, hs[h]] - Xs[h][:L]).astype(BF16) for h in range(H)]
        os = [Xs[h][L:] + jnp.dot(at_ref[rows, h * GDN_DV:h * GDN_DV + L], Us[h],
                                  preferred_element_type=F32) for h in range(H)]
        for h in range(H):
            S_scr[h] = egl[0:1, H + h:H + h + 1] * Ss[h] + lax.dot_general(
                kt_ref[rows, hs[h]], Us[h], (((0,), (0,)), ((), ())), preferred_element_type=F32)
        for h in range(H):
            gh = gate_ref[rows, hs[h]]
            o_ref[rows, hs[h]] = (_rms(os[h], onorm) * (gh * _sigmoid(gh))).astype(BF16)

    @pl.when(c == NS - 1)
    def _():
        Sout_ref[...] = S_scr[...]


def _gdn_mixer(main, small, conv_buf, S0, wc_t, par, onorm, nch=4, cb=8):
    B, T, _ = main.shape
    L = min(GDN_CHUNK, T)
    assert T % L == 0
    NC = T // L
    nch = min(nch, NC)
    cb = min(cb, NC)
    assert NC % nch == 0 and NC % cb == 0
    LA = L * nch
    tok = lambda n, j: pl.BlockSpec((None, n, 1024), lambda b, c: (b, c, j))
    sds = lambda dt: jax.ShapeDtypeStruct((B, T, 1024), dt)
    wk, uv, qt, kt, at, egl = pl.pallas_call(
        functools.partial(_gdn_prep_body, L, nch),
        grid=(B, NC // nch),
        in_specs=[pl.BlockSpec((None, LA, GDN_QKV), lambda b, c: (b, c, 0)),
                  pl.BlockSpec((None, 8, GDN_QKV), lambda b, c: (b, jnp.maximum(c * (LA // 8) - 1, 0), 0)),
                  pl.BlockSpec((None, LA, LANES), lambda b, c: (b, c, 0)),
                  pl.BlockSpec((None, CONV_W - 1, GDN_QKV), lambda b, c: (b, 0, 0)),
                  pl.BlockSpec((CONV_W, GDN_QKV), lambda b, c: (0, 0)),
                  pl.BlockSpec((2, LANES), lambda b, c: (0, 0))],
        out_specs=[tok(LA, 0), tok(LA, 0), tok(LA, 0), tok(LA, 0), tok(LA, 0),
                   pl.BlockSpec((None, nch, 8, LANES), lambda b, c: (b, c, 0, 0))],
        out_shape=[sds(BF16), sds(F32), sds(BF16), sds(BF16), sds(BF16),
                   jax.ShapeDtypeStruct((B, NC, 8, LANES), F32)],
        scratch_shapes=[pltpu.VMEM((LA + 8, GDN_QKV), F32)],
        compiler_params=_cparams(("parallel", "parallel")),
        name="gdn_prep",
    )(main, main, small, conv_buf, wc_t, par)
    NS = NC // cb
    state = pl.BlockSpec((None, GDN_HEADS, GDN_DK, GDN_DV), lambda b, c: (b, 0, 0, 0))
    o, S = pl.pallas_call(
        functools.partial(_gdn_scan_body, L, cb, NS),
        grid=(B, NS),
        in_specs=[tok(cb * L, 0), tok(cb * L, 0), tok(cb * L, 0), tok(cb * L, 0), tok(cb * L, 0),
                  pl.BlockSpec((None, cb, 8, LANES), lambda b, c: (b, c, 0, 0)),
                  tok(cb * L, 3), state,
                  pl.BlockSpec((1, GDN_DV), lambda b, c: (0, 0))],
        out_specs=[tok(cb * L, 0), state],
        out_shape=[sds(BF16), jax.ShapeDtypeStruct((B, GDN_HEADS, GDN_DK, GDN_DV), F32)],
        scratch_shapes=[pltpu.VMEM((GDN_HEADS, GDN_DK, GDN_DV), F32)],
        compiler_params=_cparams(("parallel", "arbitrary")),
        name="gdn_scan",
    )(wk, uv, qt, kt, at, egl, main, S0, onorm.reshape(1, GDN_DV))
    new_buf = main[:, T - (CONV_W - 1):, :GDN_QKV]
    return o, new_buf, S


def _mlstm_body(L, NC, qk_ref, v_ref, op_ref, small_ref, C0_ref, n0_ref, m0_ref, bias_ref, on_ref,
                o_ref, Cout_ref, nout_ref, mout_ref, C_scr, n_scr, m_scr):
    c = pl.program_id(1)

    @pl.when(c == 0)
    def _():
        C_scr[...] = C0_ref[...]
        n_scr[...] = n0_ref[...]
        m_scr[...] = m0_ref[...]

    z = small_ref[...] + bias_ref[...]
    sc = GATE_CAP * jnp.tanh(z / GATE_CAP)
    ig_all = sc
    lf_all = jax.nn.log_sigmoid(sc)
    incl, _, _ = _tri_masks(L)
    tril_b = jnp.where(incl, 1.0, 0.0).astype(BF16)
    b_all = _cumsum_rows(tril_b, lf_all)
    igT = ig_all.T
    bT = b_all.T
    qkx = qk_ref[...]
    vx = v_ref[...]
    opre = op_ref[...]
    onorm = on_ref[...]

    for h in range(ML_HEADS):
        qh = qkx[:, h * ML_DQK:(h + 1) * ML_DQK]
        kh = qkx[:, ML_HEADS * ML_DQK + h * ML_DQK:ML_HEADS * ML_DQK + (h + 1) * ML_DQK] * (ML_DQK ** -0.5)
        vh = vx[:, h * ML_DV:(h + 1) * ML_DV]
        ig_c = ig_all[:, h:h + 1]
        b_c = b_all[:, ML_HEADS + h:ML_HEADS + h + 1]
        r_r = igT[h:h + 1, :] - bT[ML_HEADS + h:ML_HEADS + h + 1, :]
        m_prev = m_scr[h:h + 1, 0:1]
        n_prev = n_scr[h:h + 1, :]
        C = C_scr[h]
        D = jnp.where(incl, b_c + r_r, NEG_BIG)
        inter = b_c + m_prev
        mt = jnp.maximum(inter, jnp.max(D, axis=-1, keepdims=True))
        W = jnp.exp(D - mt) * _dot_nt(qh, kh)
        s_inter = jnp.exp(inter - mt)
        num = s_inter * _dot(qh, C) + _dot(W, vh)
        den = s_inter * jnp.sum(qh * n_prev, axis=-1, keepdims=True) + jnp.sum(W, axis=-1, keepdims=True)
        hh = num / jnp.maximum(jnp.abs(den), jnp.exp(-mt))
        mL = mt[L - 1:L, :]
        bL = b_c[L - 1:L, :]
        wk = jnp.exp(bL - b_c + ig_c - mL) * kh
        fL = jnp.exp(bL + m_prev - mL)
        C_scr[h] = fL * C + _dot_tn(wk, vh)
        n_scr[h:h + 1, :] = fL * n_prev + jnp.sum(wk, axis=0, keepdims=True)
        m_scr[h:h + 1, :] = jnp.broadcast_to(mL, (1, LANES))
        hn = _rms(hh, onorm)
        og = opre[:, h * ML_DV:(h + 1) * ML_DV]
        o_ref[:, h * ML_DV:(h + 1) * ML_DV] = (_sigmoid(og) * hn).astype(BF16)

    @pl.when(c == NC - 1)
    def _():
        Cout_ref[...] = C_scr[...]
        nout_ref[...] = n_scr[...]
        mout_ref[...] = m_scr[...]


def _mlstm_mixer(main, small, C0, n0, m0, bias, onorm, L):
    B, T, _ = main.shape
    L = min(L, T)
    assert T % L == 0
    NC = T // L
    m0b = jnp.broadcast_to(m0[:, :, None], (B, ML_HEADS, LANES))
    return pl.pallas_call(
        functools.partial(_mlstm_body, L, NC),
        grid=(B, NC),
        in_specs=[pl.BlockSpec((None, L, 1024), lambda b, c: (b, c, 0)),
                  pl.BlockSpec((None, L, 1024), lambda b, c: (b, c, 1)),
                  pl.BlockSpec((None, L, 1024), lambda b, c: (b, c, 2)),
                  pl.BlockSpec((None, L, LANES), lambda b, c: (b, c, 0)),
                  pl.BlockSpec((None, ML_HEADS, ML_DQK, ML_DV), lambda b, c: (b, 0, 0, 0)),
                  pl.BlockSpec((None, ML_HEADS, ML_DQK), lambda b, c: (b, 0, 0)),
                  pl.BlockSpec((None, ML_HEADS, LANES), lambda b, c: (b, 0, 0)),
                  pl.BlockSpec((1, LANES), lambda b, c: (0, 0)),
                  pl.BlockSpec((1, ML_DV), lambda b, c: (0, 0))],
        out_specs=[pl.BlockSpec((None, L, 1024), lambda b, c: (b, c, 0)),
                   pl.BlockSpec((None, ML_HEADS, ML_DQK, ML_DV), lambda b, c: (b, 0, 0, 0)),
                   pl.BlockSpec((None, ML_HEADS, ML_DQK), lambda b, c: (b, 0, 0)),
                   pl.BlockSpec((None, ML_HEADS, LANES), lambda b, c: (b, 0, 0))],
        out_shape=[jax.ShapeDtypeStruct((B, T, 1024), BF16),
                   jax.ShapeDtypeStruct((B, ML_HEADS, ML_DQK, ML_DV), F32),
                   jax.ShapeDtypeStruct((B, ML_HEADS, ML_DQK), F32),
                   jax.ShapeDtypeStruct((B, ML_HEADS, LANES), F32)],
        scratch_shapes=[pltpu.VMEM((ML_HEADS, ML_DQK, ML_DV), F32),
                        pltpu.VMEM((ML_HEADS, ML_DQK), F32),
                        pltpu.VMEM((ML_HEADS, LANES), F32)],
        compiler_params=_cparams(("parallel", "arbitrary")),
        name="mlstm_mixer",
    )(main, main, main, small, C0, n0, m0b, bias, onorm.reshape(1, ML_DV))


FOX_BIAS_LANE = FOX_DH


def _fox_prep_body(q_ref, k_ref, v_ref, small_ref, bf_ref, qn_ref, kn_ref,
                   qh_ref, k32_ref, v32_ref, lf_ref):
    r = lax.broadcasted_iota(jnp.int32, (LANES, LANES), 0) // FOX_DH
    cc = lax.broadcasted_iota(jnp.int32, (LANES, LANES), 1) // FOX_DH
    avg = jnp.where(r == cc, 1.0 / FOX_DH, 0.0).astype(BF16)

    def head_rms(x, gain):
        parts = []
        for t in range(x.shape[1] // LANES):
            xt = x[:, t * LANES:(t + 1) * LANES]
            hi, lo = _split2(xt * xt)
            ms = (jnp.dot(hi, avg, preferred_element_type=F32)
                  + jnp.dot(lo, avg, preferred_element_type=F32))
            parts.append(xt * lax.rsqrt(ms + EPS))
        return jnp.concatenate(parts, axis=-1) * gain

    qn = head_rms(q_ref[...], qn_ref[...]) * (FOX_DH ** -0.5)
    lane = lax.broadcasted_iota(jnp.int32, (1, LANES), 1)
    ones3 = jnp.where(lane < FOX_BIAS_LANE + 3, 1.0, 0.0)
    for h in range(FOX_HEADS):
        xt = qn[:, (h // 2) * LANES:(h // 2 + 1) * LANES]
        if h % 2:
            xt = pltpu.roll(xt, FOX_DH, axis=1)
        qh_ref[h] = jnp.where(lane < FOX_DH, xt, ones3).astype(BF16)
    k32_ref[...] = head_rms(k_ref[...], kn_ref[...])
    v32_ref[...] = v_ref[...]
    lf_ref[...] = jax.nn.log_sigmoid(small_ref[...] + bf_ref[...])


def _fox_prep(main, small, bf_row, qn_row, kn_row, tm=512):
    B, T, _ = main.shape
    tm = min(tm, T)
    assert T % tm == 0
    blk = lambda j: pl.BlockSpec((None, tm, 1024), lambda b, i: (b, i, j))
    row = lambda n: pl.BlockSpec((1, n), lambda b, i: (0, 0))
    o1024 = pl.BlockSpec((None, tm, 1024), lambda b, i: (b, i, 0))
    return pl.pallas_call(
        _fox_prep_body,
        grid=(B, T // tm),
        in_specs=[blk(0), blk(1), blk(2),
                  pl.BlockSpec((None, tm, LANES), lambda b, i: (b, i, 0)),
                  row(LANES), row(1024), row(1024)],
        out_specs=[pl.BlockSpec((None, FOX_HEADS, tm, LANES), lambda b, i: (b, 0, i, 0)),
                   o1024, o1024,
                   pl.BlockSpec((None, tm, LANES), lambda b, i: (b, i, 0))],
        out_shape=[jax.ShapeDtypeStruct((B, FOX_HEADS, T, LANES), BF16),
                   jax.ShapeDtypeStruct((B, T, 1024), F32),
                   jax.ShapeDtypeStruct((B, T, 1024), F32),
                   jax.ShapeDtypeStruct((B, T, LANES), F32)],
        compiler_params=_cparams(("parallel", "parallel")),
        name="fox_prep",
    )(main, main, main, small, bf_row, qn_row, kn_row)


def _fox_kv_body(tm, k_ref, v_ref, lf_ref, kh_ref, vh_ref, carry):
    @pl.when(pl.program_id(1) == 0)
    def _():
        carry[...] = jnp.zeros_like(carry)

    incl, _, _ = _tri_masks(tm)
    tril_b = jnp.where(incl, 1.0, 0.0).astype(BF16)
    F = _cumsum_rows(tril_b, lf_ref[...]) + carry[0:1, :]
    carry[0:1, :] = F[tm - 1:tm, :]
    nb = -F
    b1 = nb.astype(BF16).astype(F32)
    r1 = nb - b1
    b2 = r1.astype(BF16).astype(F32)
    b3 = r1 - b2
    lane = lax.broadcasted_iota(jnp.int32, (1, LANES), 1)
    ones1 = jnp.where(lane == FOX_BIAS_LANE, 1.0, 0.0)
    k = k_ref[...]
    v = v_ref[...]
    for h in range(FOX_HEADS):
        kt = k[:, (h // 2) * LANES:(h // 2 + 1) * LANES]
        vt = v[:, (h // 2) * LANES:(h // 2 + 1) * LANES]
        if h % 2:
            kt = pltpu.roll(kt, FOX_DH, axis=1)
            vt = pltpu.roll(vt, FOX_DH, axis=1)
        bias = jnp.where(lane == FOX_BIAS_LANE, b1[:, h:h + 1],
                         jnp.where(lane == FOX_BIAS_LANE + 1, b2[:, h:h + 1],
                                   jnp.where(lane == FOX_BIAS_LANE + 2, b3[:, h:h + 1], 0.0)))
        kh_ref[h] = jnp.where(lane < FOX_DH, kt, bias).astype(BF16)
        vh_ref[h] = jnp.where(lane < FOX_DH, vt, ones1).astype(BF16)


def _fox_kv(k_all, v_all, lf_all):
    B, N, _ = k_all.shape
    tm = next(t for t in (512, 384, 256, 128) if N % t == 0)
    hm = pl.BlockSpec((None, FOX_HEADS, tm, LANES), lambda b, i: (b, 0, i, 0))
    return pl.pallas_call(
        functools.partial(_fox_kv_body, tm),
        grid=(B, N // tm),
        in_specs=[pl.BlockSpec((None, tm, 1024), lambda b, i: (b, i, 0)),
                  pl.BlockSpec((None, tm, 1024), lambda b, i: (b, i, 0)),
                  pl.BlockSpec((None, tm, LANES), lambda b, i: (b, i, 0))],
        out_specs=[hm, hm],
        out_shape=[jax.ShapeDtypeStruct((B, FOX_HEADS, N, LANES), BF16),
                   jax.ShapeDtypeStruct((B, FOX_HEADS, N, LANES), BF16)],
        scratch_shapes=[pltpu.VMEM((8, LANES), F32)],
        compiler_params=_cparams(("parallel", "arbitrary")),
        name="fox_kv",
    )(k_all, v_all, lf_all)


def _flash_body(P, tq, tk, rq, qi_tab, kj_tab, q_ref, k_ref, v_ref, gate_ref, o_ref, m_scr, acc_scr):
    s_idx = pl.program_id(2)
    qi = qi_tab[s_idx]
    kj = kj_tab[s_idx]
    last_kj = (P + (qi + 1) * tq - 1) // tk
    aligned = tq == tk and P % tk == 0

    @pl.when(kj == 0)
    def _():
        m_scr[...] = jnp.full(m_scr.shape, NEG_BIG, F32)
        acc_scr[...] = jnp.zeros_like(acc_scr)

    def step(masked):
        for hh in range(2):
            for r in range(tq // rq):
                rows = slice(r * rq, (r + 1) * rq)
                nk = min(tk, (r + 1) * rq) if (masked and aligned) else tk
                s = lax.dot_general(q_ref[hh, rows, :], k_ref[hh, 0:nk, :], (((1,), (1,)), ((), ())),
                                    preferred_element_type=F32)
                if masked:
                    qpos = P + qi * tq + r * rq + lax.broadcasted_iota(jnp.int32, (rq, nk), 0)
                    kpos = kj * tk + lax.broadcasted_iota(jnp.int32, (rq, nk), 1)
                    s = jnp.where(kpos <= qpos, s, NEG_BIG)
                m_prev = m_scr[hh, rows, :][:, 0:1]
                m_new = jnp.maximum(m_prev, jnp.max(s, axis=-1, keepdims=True))
                alpha = jnp.exp(m_prev - m_new)
                p = jnp.exp(s - m_new).astype(BF16)
                acc_scr[hh, rows, :] = alpha * acc_scr[hh, rows, :] + jnp.dot(
                    p, v_ref[hh, 0:nk, :], preferred_element_type=F32)
                m_scr[hh, rows, :] = jnp.broadcast_to(m_new, (rq, LANES))

    needs_mask = (kj + 1) * tk - 1 > P + qi * tq

    @pl.when(needs_mask)
    def _():
        step(True)

    @pl.when(jnp.logical_not(needs_mask))
    def _():
        step(False)

    @pl.when(kj == last_kj)
    def _():
        lane = lax.broadcasted_iota(jnp.int32, (1, LANES), 1)
        a0 = acc_scr[0]
        a1 = acc_scr[1]
        o0 = a0 * (1.0 / a0[:, FOX_BIAS_LANE:FOX_BIAS_LANE + 1])
        o1 = a1 * (1.0 / a1[:, FOX_BIAS_LANE:FOX_BIAS_LANE + 1])
        o = jnp.where(lane < FOX_DH, o0, pltpu.roll(o1, FOX_DH, axis=1))
        g = gate_ref[...]
        o_ref[...] = (o * _sigmoid(g)).astype(BF16)


def _fox_flash(qh, kh, vh, gate_src, P, tq, tk, rq):
    B, _, T, _ = qh.shape
    N = kh.shape[2]
    assert T % tq == 0 and N % tk == 0 and tq % rq == 0
    nq = T // tq
    pairs = [(qi, kj) for qi in range(nq) for kj in range((P + (qi + 1) * tq - 1) // tk + 1)]
    qi_tab = jnp.asarray(np.array([p[0] for p in pairs], np.int32))
    kj_tab = jnp.asarray(np.array([p[1] for p in pairs], np.int32))
    HP = FOX_HEADS // 2
    gate_blk0 = 3 * (1024 // LANES)
    grid_spec = pltpu.PrefetchScalarGridSpec(
        num_scalar_prefetch=2,
        grid=(B, HP, len(pairs)),
        in_specs=[pl.BlockSpec((None, 2, tq, LANES), lambda b, p, s, qt, kt: (b, p, qt[s], 0)),
                  pl.BlockSpec((None, 2, tk, LANES), lambda b, p, s, qt, kt: (b, p, kt[s], 0)),
                  pl.BlockSpec((None, 2, tk, LANES), lambda b, p, s, qt, kt: (b, p, kt[s], 0)),
                  pl.BlockSpec((None, tq, LANES), lambda b, p, s, qt, kt: (b, qt[s], gate_blk0 + p))],
        out_specs=pl.BlockSpec((None, tq, LANES), lambda b, p, s, qt, kt: (b, qt[s], p)),
        scratch_shapes=[pltpu.VMEM((2, tq, LANES), F32),
                        pltpu.VMEM((2, tq, LANES), F32)],
    )
    return pl.pallas_call(
        functools.partial(_flash_body, P, tq, tk, rq),
        grid_spec=grid_spec,
        out_shape=jax.ShapeDtypeStruct((B, T, 1024), BF16),
        compiler_params=_cparams(("parallel", "parallel", "arbitrary")),
        name="fox_flash",
    )(qi_tab, kj_tab, qh, kh, vh, gate_src)


def _pad_lanes(a, offset=0):
    n = a.shape[-1]
    pads = [(0, 0)] * (a.ndim - 1) + [(offset, LANES - n - offset)]
    return jnp.pad(a, pads)


def _trunk(x, gdn_conv, gdn_S, ml_C, ml_n, ml_m, fox_k, fox_v, fox_lf, mem_k, mem_v, W, chunk):
    B, T, D = x.shape
    M = B * T
    x2 = x.reshape(M, D)
    conv_o, S_o, C_o, n_o, m_o, k_o, v_o, lf_o = [], [], [], [], [], [], [], []
    for layer in range(DEPTH):
        kind, j = layer % N_MIXERS, layer // N_MIXERS
        gains = W["norm_gains"][layer]
        if kind == 0:
            main, small = _prenorm_mm(x2, gains[0], W["gdn_w_main"][j], W["gdn_w_small"][j])
            o, buf, S = _gdn_mixer(main.reshape(B, T, GDN_MAIN), small.reshape(B, T, LANES),
                                   gdn_conv[j], gdn_S[j], W["gdn_wc_t"][j], W["gdn_par"][j],
                                   W["gdn_onorm"][j])
            conv_o.append(buf)
            S_o.append(S)
            w_out = W["gdn_w_out"][j]
        elif kind == 1:
            main, small = _prenorm_mm(x2, gains[0], W["ml_w_main"][j], W["ml_w_small"][j])
            o, C, n, m = _mlstm_mixer(main.reshape(B, T, ML_MAIN), small.reshape(B, T, LANES),
                                      ml_C[j], ml_n[j], ml_m[j], W["ml_bias"][j], W["ml_onorm"][j], chunk)
            C_o.append(C)
            n_o.append(n)
            m_o.append(m[:, :, 0])
            w_out = W["ml_w_out"][j]
        else:
            main, small = _prenorm_mm(x2, gains[0], W["fox_w_main"][j], W["fox_w_small"][j])
            main3 = main.reshape(B, T, FOX_MAIN)
            qh, k32, v32, lfp = _fox_prep(main3, small.reshape(B, T, LANES), W["fox_bf"][j],
                                          W["fox_qn"][j], W["fox_kn"][j])
            P = fox_k[j].shape[1]
            if P == 0:
                k_all, v_all, lf_all, tq, tk, rq = k32, v32, lfp, 1024, 1024, 512
            else:
                npad = -(P + T) % LANES
                zkv = jnp.zeros((B, npad, 1024), F32)
                k_all = jnp.concatenate([fox_k[j].reshape(B, P, 1024), k32, zkv], axis=1)
                v_all = jnp.concatenate([fox_v[j].reshape(B, P, 1024), v32, zkv], axis=1)
                lf_all = jnp.concatenate([_pad_lanes(fox_lf[j]), lfp, jnp.zeros((B, npad, LANES), F32)], axis=1)
                tq, tk, rq = T, P + T + npad, T
            kh, vh = _fox_kv(k_all, v_all, lf_all)
            o = _fox_flash(qh, kh, vh, main3, P, tq, tk, rq)
            k_o.append(k32.reshape(B, T, FOX_HEADS, FOX_DH))
            v_o.append(v32.reshape(B, T, FOX_HEADS, FOX_DH))
            lf_o.append(lfp[:, :, :FOX_HEADS])
            w_out = W["fox_w_out"][j]
        x2 = _out_resid(o.reshape(M, 1024), w_out, gains[1], x2)
        (mk_arr, kcol), (mv_arr, vcol) = mem_k[layer], mem_v[layer]
        x2 = _memattn(x2.reshape(B, T, D), gains[2], W["x_w_q"][layer], mk_arr, kcol, mv_arr, vcol,
                      W["x_w_o"][layer], gains[3]).reshape(M, D)
        x2 = _ffn(x2, gains[4], W["ffn_w_gu"][layer], W["ffn_w_down"][layer], gains[5])
    return (x2.reshape(B, T, D), jnp.stack(conv_o), jnp.stack(S_o), jnp.stack(C_o), jnp.stack(n_o),
            jnp.stack(m_o), jnp.stack(k_o), jnp.stack(v_o), jnp.stack(lf_o))


def kernel(x_prompt, x_sample, state_gdn_conv, state_gdn_S, state_mlstm_C, state_mlstm_n, state_mlstm_m, cache_fox_k, cache_fox_v, cache_fox_logf, cache_mem_k, cache_mem_v, mem_prompt, norm_gains, gdn_w_in, gdn_w_conv, gdn_a_log, gdn_dt_bias, gdn_onorm, gdn_w_out, ml_w_in, ml_b_i, ml_b_f, ml_onorm, ml_w_out, fox_w_in, fox_b_f, fox_qk_norm, fox_w_out, x_w_q, x_w_kv, x_w_o, ffn_w_gu, ffn_w_down):
    B = x_prompt.shape[0]
    DB = x_sample.shape[0]
    n_a, n_b, n_c = gdn_w_in.shape[0], ml_w_in.shape[0], fox_w_in.shape[0]
    W = {
        "norm_gains": norm_gains,
        "gdn_w_main": gdn_w_in[:, :, :GDN_MAIN].astype(BF16),
        "gdn_w_small": _pad_lanes(gdn_w_in[:, :, GDN_MAIN:]).astype(BF16),
        "gdn_wc_t": jnp.transpose(gdn_w_conv, (0, 2, 1)),
        "gdn_par": jnp.stack([_pad_lanes(gdn_a_log, GDN_HEADS), _pad_lanes(gdn_dt_bias, GDN_HEADS)], axis=1),
        "gdn_onorm": gdn_onorm,
        "gdn_w_out": gdn_w_out.astype(BF16),
        "ml_w_main": ml_w_in[:, :, :ML_MAIN].astype(BF16),
        "ml_w_small": _pad_lanes(ml_w_in[:, :, ML_MAIN:]).astype(BF16),
        "ml_bias": _pad_lanes(jnp.concatenate([ml_b_i, ml_b_f], axis=-1))[:, None, :],
        "ml_onorm": ml_onorm,
        "ml_w_out": ml_w_out.astype(BF16),
        "fox_w_main": fox_w_in[:, :, :FOX_MAIN].astype(BF16),
        "fox_w_small": _pad_lanes(fox_w_in[:, :, FOX_MAIN:]).astype(BF16),
        "fox_bf": _pad_lanes(fox_b_f)[:, None, :],
        "fox_qn": jnp.tile(fox_qk_norm[:, 0, :], (1, FOX_HEADS))[:, None, :],
        "fox_kn": jnp.tile(fox_qk_norm[:, 1, :], (1, FOX_HEADS))[:, None, :],
        "fox_w_out": fox_w_out.astype(BF16),
        "x_w_q": x_w_q.astype(BF16),
        "x_w_o": x_w_o.astype(BF16),
        "ffn_w_gu": ffn_w_gu.astype(BF16),
        "ffn_w_down": ffn_w_down.astype(BF16),
    }

    mem2 = mem_prompt.reshape(B * N_MEM, D_MODEL)
    x_w_kv_b = x_w_kv.astype(BF16)
    kv = [_prenorm_mm(mem2, norm_gains[l, 6], x_w_kv_b[l]).reshape(B, N_MEM, 2 * D_MODEL)
          for l in range(DEPTH)]
    p_mem_k = jnp.stack([a[:, :, :D_MODEL].reshape(B, N_MEM, X_HEADS, X_DH) for a in kv])
    p_mem_v = jnp.stack([a[:, :, D_MODEL:].reshape(B, N_MEM, X_HEADS, X_DH) for a in kv])

    zeros = lambda *s: jnp.zeros(s, F32)
    prompt = _trunk(
        x_prompt,
        zeros(n_a, B, CONV_W - 1, GDN_QKV), zeros(n_a, B, GDN_HEADS, GDN_DK, GDN_DV),
        zeros(n_b, B, ML_HEADS, ML_DQK, ML_DV), zeros(n_b, B, ML_HEADS, ML_DQK), zeros(n_b, B, ML_HEADS),
        zeros(n_c, B, 0, FOX_HEADS, FOX_DH), zeros(n_c, B, 0, FOX_HEADS, FOX_DH), zeros(n_c, B, 0, FOX_HEADS),
        [(a, 0) for a in kv], [(a, 1) for a in kv], W, 64)
    sample = _trunk(
        x_sample, state_gdn_conv, state_gdn_S, state_mlstm_C, state_mlstm_n, state_mlstm_m,
        cache_fox_k, cache_fox_v, cache_fox_logf,
        [(cache_mem_k[l].reshape(DB, N_MEM, D_MODEL), 0) for l in range(DEPTH)],
        [(cache_mem_v[l].reshape(DB, N_MEM, D_MODEL), 0) for l in range(DEPTH)], W, 64)
    (y_p, p_conv, p_S, p_C, p_n, p_m, p_k, p_v, p_lf) = prompt
    (y_s, s_conv, s_S, s_C, s_n, s_m, s_k, s_v, s_lf) = sample
    return (y_p, y_s, p_conv, p_S, p_C, p_n, p_m, p_k, p_v, p_lf, p_mem_k, p_mem_v,
            s_conv, s_S, s_C, s_n, s_m, s_k, s_v, s_lf)
```

```python
import functools

import numpy as np
import jax
import jax.numpy as jnp
from jax import lax
from jax.experimental import pallas as pl
from jax.experimental.pallas import tpu as pltpu

F32 = jnp.float32
BF16 = jnp.bfloat16

D_MODEL = 1024
DEPTH = 4
N_MIXERS = 3
EPS = 1e-6
CONV_W = 4

GDN_HEADS = 8
GDN_DK = 128
GDN_DV = 128
GDN_QKV = 3072
GDN_MAIN = 4096

ML_HEADS = 4
ML_DQK = 128
ML_DV = 256
ML_MAIN = 3072
GATE_CAP = 15.0

FOX_HEADS = 16
FOX_DH = 64
FOX_MAIN = 4096

N_MEM = 256
X_HEADS = 4
X_DH = 256
D_FF = 2816

LANES = 128
NEG_BIG = -1e30
LOG2E = 1.4426950408889634

VMEM_LIMIT = 48 * 1024 * 1024


def _wspec(li, blk, imap):
    return pl.BlockSpec((None,) + blk, lambda *g: (li,) + imap(*g))


def _cparams(sem):
    return pltpu.CompilerParams(dimension_semantics=sem, vmem_limit_bytes=VMEM_LIMIT)


def _rms(x, g):
    return x * lax.rsqrt(jnp.mean(x * x, axis=-1, keepdims=True) + EPS) * g


def _sigmoid(x):
    return 1.0 / (1.0 + jnp.exp(-x))


def _dot(a, b):
    return jnp.dot(a.astype(BF16), b.astype(BF16), preferred_element_type=F32)


def _dot_nt(a, b):
    return lax.dot_general(a.astype(BF16), b.astype(BF16), (((1,), (1,)), ((), ())),
                           preferred_element_type=F32)


def _dot_tn(a, b):
    return lax.dot_general(a.astype(BF16), b.astype(BF16), (((0,), (0,)), ((), ())),
                           preferred_element_type=F32)


def _split2(x):
    hi = x.astype(BF16)
    lo = (x - hi.astype(F32)).astype(BF16)
    return hi, lo


def _dot_x3(a, b):
    ah, al = _split2(a)
    bh, bl = _split2(b)
    d = lambda u, v: jnp.dot(u, v, preferred_element_type=F32)
    return d(ah, bh) + (d(ah, bl) + d(al, bh))


def _cumsum_rows(tril_b, x):
    h = x.astype(BF16)
    r = x - h.astype(F32)
    m = r.astype(BF16)
    l = (r - m.astype(F32)).astype(BF16)
    d = lambda v: jnp.dot(tril_b, v, preferred_element_type=F32)
    return d(h) + (d(m) + d(l))


def _tri_masks(L):
    row = lax.broadcasted_iota(jnp.int32, (L, L), 0)
    col = lax.broadcasted_iota(jnp.int32, (L, L), 1)
    return row >= col, row > col, row == col


def _prenorm_mm_body(has_small, *refs):
    if has_small:
        x_ref, g_ref, w_ref, ws_ref, o_ref, os_ref, xn_ref = refs
    else:
        x_ref, g_ref, w_ref, o_ref, xn_ref = refs

    @pl.when(pl.program_id(1) == 0)
    def _():
        xb = _rms(x_ref[...], g_ref[...]).astype(BF16)
        xn_ref[...] = xb
        if has_small:
            os_ref[...] = jnp.dot(xb, ws_ref[...], preferred_element_type=F32)

    o_ref[...] = jnp.dot(xn_ref[...], w_ref[...], preferred_element_type=F32)


def _prenorm_mm(x, gain, w_main, li, w_small=None, tm=1024, tn=1024):
    M, D = x.shape
    N = w_main.shape[2]
    tm = min(tm, M)
    tn = min(tn, N)
    assert M % tm == 0 and N % tn == 0
    has_small = w_small is not None
    in_specs = [pl.BlockSpec((tm, D), lambda i, j: (i, 0)),
                pl.BlockSpec((1, D), lambda i, j: (0, 0)),
                _wspec(li, (D, tn), lambda i, j: (0, j))]
    out_specs = [pl.BlockSpec((tm, tn), lambda i, j: (i, j))]
    out_shape = [jax.ShapeDtypeStruct((M, N), F32)]
    args = [x, gain.reshape(1, D), w_main]
    if has_small:
        in_specs.append(_wspec(li, (D, LANES), lambda i, j: (0, 0)))
        out_specs.append(pl.BlockSpec((tm, LANES), lambda i, j: (i, 0)))
        out_shape.append(jax.ShapeDtypeStruct((M, LANES), F32))
        args.append(w_small)
    res = pl.pallas_call(
        functools.partial(_prenorm_mm_body, has_small),
        grid=(M // tm, N // tn),
        in_specs=in_specs, out_specs=out_specs, out_shape=out_shape,
        scratch_shapes=[pltpu.VMEM((tm, D), BF16)],
        compiler_params=_cparams(("parallel", "arbitrary")),
        name="prenorm_mm",
    )(*args)
    return res if has_small else res[0]


def _memattn_body(o_ref, wout_ref, g1_ref, x_ref, g2_ref, wq_ref, mk_ref, mv_ref, wo_ref, g3_ref, y_ref):
    x = x_ref[...] + _rms(jnp.dot(o_ref[...], wout_ref[...], preferred_element_type=F32), g1_ref[...])
    h = _rms(x, g2_ref[...]).astype(BF16)
    q = jnp.dot(h, wq_ref[...], preferred_element_type=F32).astype(BF16)
    mk = mk_ref[...].astype(BF16)
    mv = mv_ref[...].astype(BF16)
    outs = []
    for hd in range(X_HEADS):
        sl = slice(hd * X_DH, (hd + 1) * X_DH)
        s = _dot_nt(q[:, sl], mk[:, sl]) * (X_DH ** -0.5)
        p = jnp.exp(s - jnp.max(s, axis=-1, keepdims=True))
        p = p * (1.0 / jnp.sum(p, axis=-1, keepdims=True))
        outs.append(jnp.dot(p.astype(BF16), mv[:, sl], preferred_element_type=F32))
    o = jnp.concatenate(outs, axis=-1).astype(BF16)
    y = jnp.dot(o, wo_ref[...], preferred_element_type=F32)
    y_ref[...] = x + _rms(y, g3_ref[...])


def _memattn(o, w_out, lo, g1, x, g2, wq, mk_arr, kcol, mv_arr, vcol, wo, lx, g3, tq=512):
    B, T, D = x.shape
    tq = min(tq, T)
    assert T % tq == 0
    HD = X_HEADS * X_DH
    K = o.shape[-1]
    return pl.pallas_call(
        _memattn_body,
        grid=(B, T // tq),
        in_specs=[pl.BlockSpec((None, tq, K), lambda b, i: (b, i, 0)),
                  _wspec(lo, (K, D), lambda b, i: (0, 0)),
                  pl.BlockSpec((1, D), lambda b, i: (0, 0)),
                  pl.BlockSpec((None, tq, D), lambda b, i: (b, i, 0)),
                  pl.BlockSpec((1, D), lambda b, i: (0, 0)),
                  _wspec(lx, (D, HD), lambda b, i: (0, 0)),
                  pl.BlockSpec((None, N_MEM, HD), lambda b, i: (b, 0, kcol)),
                  pl.BlockSpec((None, N_MEM, HD), lambda b, i: (b, 0, vcol)),
                  _wspec(lx, (HD, D), lambda b, i: (0, 0)),
                  pl.BlockSpec((1, D), lambda b, i: (0, 0))],
        out_specs=pl.BlockSpec((None, tq, D), lambda b, i: (b, i, 0)),
        out_shape=jax.ShapeDtypeStruct((B, T, D), F32),
        compiler_params=_cparams(("parallel", "parallel")),
        name="memattn",
    )(o, w_out, g1.reshape(1, D), x, g2.reshape(1, D), wq, mk_arr, mv_arr, wo, g3.reshape(1, D))


def _ffn_body(nf, x_ref, g4_ref, wg_ref, wu_ref, wd_ref, g5_ref, y_ref, xn_ref, acc_ref):
    f = pl.program_id(1)

    @pl.when(f == 0)
    def _():
        xn_ref[...] = _rms(x_ref[...], g4_ref[...]).astype(BF16)
        acc_ref[...] = jnp.zeros_like(acc_ref)

    xn = xn_ref[...]
    g = jnp.dot(xn, wg_ref[...], preferred_element_type=F32)
    u = jnp.dot(xn, wu_ref[...], preferred_element_type=F32)
    a = (g * _sigmoid(g)) * u
    acc_ref[...] += jnp.dot(a.astype(BF16), wd_ref[...], preferred_element_type=F32)

    @pl.when(f == nf - 1)
    def _():
        y_ref[...] = x_ref[...] + _rms(acc_ref[...], g5_ref[...])


def _ffn(x, g4, w_gu, w_down, li, g5, tm=1024, tf=256):
    M, D = x.shape
    tm = min(tm, M)
    assert M % tm == 0 and D_FF % tf == 0
    nf = D_FF // tf
    return pl.pallas_call(
        functools.partial(_ffn_body, nf),
        grid=(M // tm, nf),
        in_specs=[pl.BlockSpec((tm, D), lambda i, f: (i, 0)),
                  pl.BlockSpec((1, D), lambda i, f: (0, 0)),
                  _wspec(li, (D, tf), lambda i, f: (0, f)),
                  _wspec(li, (D, tf), lambda i, f: (0, nf + f)),
                  _wspec(li, (tf, D), lambda i, f: (f, 0)),
                  pl.BlockSpec((1, D), lambda i, f: (0, 0))],
        out_specs=pl.BlockSpec((tm, D), lambda i, f: (i, 0)),
        out_shape=jax.ShapeDtypeStruct((M, D), F32),
        scratch_shapes=[pltpu.VMEM((tm, D), BF16), pltpu.VMEM((tm, D), F32)],
        compiler_params=_cparams(("parallel", "arbitrary")),
        name="ffn",
    )(x, g4.reshape(1, D), w_gu, w_gu, w_down, g5.reshape(1, D))


GDN_CHUNK = 64
GDN_BASE = 8


def _inverse_masks(L):
    row = lax.broadcasted_iota(jnp.int32, (L, L), 0)
    col = lax.broadcasted_iota(jnp.int32, (L, L), 1)
    base = row // GDN_BASE == col // GDN_BASE
    offs = []
    s = GDN_BASE
    while s < L:
        same_pair = row // (2 * s) == col // (2 * s)
        offs.append(jnp.logical_and(same_pair, (row // s) - (col // s) == 1))
        s *= 2
    return base, row == col, offs


def _unit_lower_inverse(As, masks, mm):
    base, diag, offs = masks
    eye = jnp.where(diag, 1.0, 0.0)
    Ps = [jnp.where(base, -A, 0.0) for A in As]
    Ts = [eye + P for P in Ps]
    k = 1
    while k < GDN_BASE // 2:
        Ps = [mm(P, P) for P in Ps]
        Ts = [T + mm(T, P) for T, P in zip(Ts, Ps)]
        k *= 2
    for off in offs:
        Ms = [mm(T, jnp.where(off, A, 0.0)) for T, A in zip(Ts, As)]
        Ts = [T - mm(M, T) for T, M in zip(Ts, Ms)]
    return Ts


def _gdn_prep_body(L, nch, qkv_ref, prev_ref, small_ref, buf_ref, wc_ref, par_ref,
                   wk_ref, uv_ref, qt_ref, kt_ref, at_ref, egl_ref, xbuf):
    c = pl.program_id(1)
    LA = L * nch

    @pl.when(c == 0)
    def _():
        xbuf[0:8, :] = jnp.zeros((8, GDN_QKV), F32)
        xbuf[5:8, :] = buf_ref[...]

    @pl.when(c > 0)
    def _():
        xbuf[0:8, :] = prev_ref[...]

    xbuf[8:8 + LA, :] = qkv_ref[...]
    y = xbuf[5:5 + LA, :] * wc_ref[0:1, :]
    for j in range(1, CONV_W):
        y = y + xbuf[5 + j:5 + j + LA, :] * wc_ref[j:j + 1, :]
    y = y * _sigmoid(y)

    small = small_ref[...]
    beta_all = _sigmoid(small)
    g_all = -jnp.exp(par_ref[0:1, :]) * jax.nn.softplus(small + par_ref[1:2, :])
    incl, strict, _ = _tri_masks(L)
    tril_b = jnp.where(incl, 1.0, 0.0).astype(BF16)
    masks = _inverse_masks(L)
    H = GDN_HEADS

    units = []
    for ci in range(nch):
        rows = slice(ci * L, (ci + 1) * L)
        G_all = _cumsum_rows(tril_b, g_all[rows])
        GT = G_all.T
        egl_ref[ci] = jnp.broadcast_to(jnp.exp(G_all[L - 1:L, :]), (8, LANES))
        for h in range(H):
            qh = y[rows, h * GDN_DK:(h + 1) * GDN_DK]
            kh = y[rows, (H + h) * GDN_DK:(H + h + 1) * GDN_DK]
            vh = y[rows, 2 * H * GDN_DK + h * GDN_DV:2 * H * GDN_DK + (h + 1) * GDN_DV]
            qh = qh * lax.rsqrt(jnp.sum(qh * qh, axis=-1, keepdims=True) + EPS) * (GDN_DK ** -0.5)
            kh = kh * lax.rsqrt(jnp.sum(kh * kh, axis=-1, keepdims=True) + EPS)
            bc = beta_all[rows, h:h + 1]
            Gc = G_all[:, H + h:H + h + 1]
            Gr = GT[H + h:H + h + 1, :]
            eG = jnp.exp(Gc)
            dec = jnp.exp(jnp.where(incl, Gc - Gr, NEG_BIG))
            qkk = _dot_nt(jnp.concatenate([qh, kh], axis=0), kh)
            A = jnp.where(strict, bc * qkk[L:] * dec, 0.0)
            rhs = jnp.concatenate([bc * vh, (bc * eG) * kh], axis=-1)
            cols = slice(h * GDN_DV, (h + 1) * GDN_DV)
            qt_ref[rows, cols] = (eG * qh).astype(BF16)
            kt_ref[rows, cols] = (kh * jnp.exp(Gc[L - 1:L, :] - Gc)).astype(BF16)
            at_ref[rows, cols] = jnp.concatenate(
                [qkk[:L] * dec, jnp.zeros((L, GDN_DV - L), F32)], axis=-1).astype(BF16)
            units.append((rows, cols, A, rhs))

    Ts = _unit_lower_inverse([u[2] for u in units], masks, _dot_x3)
    for (rows, cols, _, rhs), T in zip(units, Ts):
        sol = _dot_x3(T, rhs)
        uv_ref[rows, cols] = sol[:, :GDN_DV]
        wk_ref[rows, cols] = sol[:, GDN_DV:].astype(BF16)


def _gdn_scan_body(L, cb, NS, wk_ref, uv_ref, qt_ref, kt_ref, at_ref, egl_ref, gate_ref, S0_ref, on_ref,
                   o_ref, Sout_ref, S_scr):
    c = pl.program_id(1)

    @pl.when(c == 0)
    def _():
        S_scr[...] = S0_ref[...]

    onorm = on_ref[...]
    H = GDN_HEADS
    for ci in range(cb):
        rows = slice(ci * L, (ci + 1) * L)
        egl = egl_ref[ci]
        hs = [slice(h * GDN_DV, (h + 1) * GDN_DV) for h in range(H)]
        Ss = [S_scr[h] for h in range(H)]
        Xs = [jnp.dot(jnp.concatenate([wk_ref[rows, hs[h]], qt_ref[rows, hs[h]]], axis=0),
                      Ss[h].astype(BF16), preferred_element_type=F32) for h in range(H)]
        Us = [(uv_ref[rows, hs[h]] - Xs[h][:L]).astype(BF16) for h in range(H)]
        os = [Xs[h][L:] + jnp.dot(at_ref[rows, h * GDN_DV:h * GDN_DV + L], Us[h],
                                  preferred_element_type=F32) for h in range(H)]
        for h in range(H):
            S_scr[h] = egl[0:1, H + h:H + h + 1] * Ss[h] + lax.dot_general(
                kt_ref[rows, hs[h]], Us[h], (((0,), (0,)), ((), ())), preferred_element_type=F32)
        for h in range(H):
            gh = gate_ref[rows, hs[h]]
            o_ref[rows, hs[h]] = (_rms(os[h], onorm) * (gh * _sigmoid(gh))).astype(BF16)

    @pl.when(c == NS - 1)
    def _():
        Sout_ref[...] = S_scr[...]


def _gdn_mixer(main, small, conv_buf, S0, wc_t, par, onorm, nch=4, cb=8):
    B, T, _ = main.shape
    L = min(GDN_CHUNK, T)
    assert T % L == 0
    NC = T // L
    nch = min(nch, NC)
    cb = min(cb, NC)
    assert NC % nch == 0 and NC % cb == 0
    LA = L * nch
    tok = lambda n, j: pl.BlockSpec((None, n, 1024), lambda b, c: (b, c, j))
    sds = lambda dt: jax.ShapeDtypeStruct((B, T, 1024), dt)
    wk, uv, qt, kt, at, egl = pl.pallas_call(
        functools.partial(_gdn_prep_body, L, nch),
        grid=(B, NC // nch),
        in_specs=[pl.BlockSpec((None, LA, GDN_QKV), lambda b, c: (b, c, 0)),
                  pl.BlockSpec((None, 8, GDN_QKV), lambda b, c: (b, jnp.maximum(c * (LA // 8) - 1, 0), 0)),
                  pl.BlockSpec((None, LA, LANES), lambda b, c: (b, c, 0)),
                  pl.BlockSpec((None, CONV_W - 1, GDN_QKV), lambda b, c: (b, 0, 0)),
                  pl.BlockSpec((CONV_W, GDN_QKV), lambda b, c: (0, 0)),
                  pl.BlockSpec((2, LANES), lambda b, c: (0, 0))],
        out_specs=[tok(LA, 0), tok(LA, 0), tok(LA, 0), tok(LA, 0), tok(LA, 0),
                   pl.BlockSpec((None, nch, 8, LANES), lambda b, c: (b, c, 0, 0))],
        out_shape=[sds(BF16), sds(F32), sds(BF16), sds(BF16), sds(BF16),
                   jax.ShapeDtypeStruct((B, NC, 8, LANES), F32)],
        scratch_shapes=[pltpu.VMEM((LA + 8, GDN_QKV), F32)],
        compiler_params=_cparams(("parallel", "parallel")),
        name="gdn_prep",
    )(main, main, small, conv_buf, wc_t, par)
    NS = NC // cb
    state = pl.BlockSpec((None, GDN_HEADS, GDN_DK, GDN_DV), lambda b, c: (b, 0, 0, 0))
    o, S = pl.pallas_call(
        functools.partial(_gdn_scan_body, L, cb, NS),
        grid=(B, NS),
        in_specs=[tok(cb * L, 0), tok(cb * L, 0), tok(cb * L, 0), tok(cb * L, 0), tok(cb * L, 0),
                  pl.BlockSpec((None, cb, 8, LANES), lambda b, c: (b, c, 0, 0)),
                  tok(cb * L, 3), state,
                  pl.BlockSpec((1, GDN_DV), lambda b, c: (0, 0))],
        out_specs=[tok(cb * L, 0), state],
        out_shape=[sds(BF16), jax.ShapeDtypeStruct((B, GDN_HEADS, GDN_DK, GDN_DV), F32)],
        scratch_shapes=[pltpu.VMEM((GDN_HEADS, GDN_DK, GDN_DV), F32)],
        compiler_params=_cparams(("parallel", "arbitrary")),
        name="gdn_scan",
    )(wk, uv, qt, kt, at, egl, main, S0, onorm.reshape(1, GDN_DV))
    new_buf = main[:, T - (CONV_W - 1):, :GDN_QKV]
    return o, new_buf, S


def _mlstm_body(L, NC, qk_ref, v_ref, op_ref, small_ref, C0_ref, n0_ref, m0_ref, bias_ref, on_ref,
                o_ref, Cout_ref, nout_ref, mout_ref, C_scr, n_scr, m_scr):
    c = pl.program_id(1)

    @pl.when(c == 0)
    def _():
        C_scr[...] = C0_ref[...]
        n_scr[...] = n0_ref[...]
        m_scr[...] = m0_ref[...]

    z = small_ref[...] + bias_ref[...]
    sc = GATE_CAP * jnp.tanh(z / GATE_CAP)
    ig_all = sc
    lf_all = jax.nn.log_sigmoid(sc)
    incl, _, _ = _tri_masks(L)
    tril_b = jnp.where(incl, 1.0, 0.0).astype(BF16)
    b_all = _cumsum_rows(tril_b, lf_all)
    igT = ig_all.T
    bT = b_all.T
    qkx = qk_ref[...]
    vx = v_ref[...]
    opre = op_ref[...]
    onorm = on_ref[...]

    for h in range(ML_HEADS):
        qh = qkx[:, h * ML_DQK:(h + 1) * ML_DQK]
        kh = qkx[:, ML_HEADS * ML_DQK + h * ML_DQK:ML_HEADS * ML_DQK + (h + 1) * ML_DQK] * (ML_DQK ** -0.5)
        vh = vx[:, h * ML_DV:(h + 1) * ML_DV]
        ig_c = ig_all[:, h:h + 1]
        b_c = b_all[:, ML_HEADS + h:ML_HEADS + h + 1]
        r_r = igT[h:h + 1, :] - bT[ML_HEADS + h:ML_HEADS + h + 1, :]
        m_prev = m_scr[h:h + 1, 0:1]
        n_prev = n_scr[h:h + 1, :]
        C = C_scr[h]
        D = jnp.where(incl, b_c + r_r, NEG_BIG)
        inter = b_c + m_prev
        mt = jnp.maximum(inter, jnp.max(D, axis=-1, keepdims=True))
        W = jnp.exp(D - mt) * _dot_nt(qh, kh)
        s_inter = jnp.exp(inter - mt)
        num = s_inter * _dot(qh, C) + _dot(W, vh)
        den = s_inter * jnp.sum(qh * n_prev, axis=-1, keepdims=True) + jnp.sum(W, axis=-1, keepdims=True)
        hh = num / jnp.maximum(jnp.abs(den), jnp.exp(-mt))
        mL = mt[L - 1:L, :]
        bL = b_c[L - 1:L, :]
        wk = jnp.exp(bL - b_c + ig_c - mL) * kh
        fL = jnp.exp(bL + m_prev - mL)
        C_scr[h] = fL * C + _dot_tn(wk, vh)
        n_scr[h:h + 1, :] = fL * n_prev + jnp.sum(wk, axis=0, keepdims=True)
        m_scr[h:h + 1, :] = jnp.broadcast_to(mL, (1, LANES))
        hn = _rms(hh, onorm)
        og = opre[:, h * ML_DV:(h + 1) * ML_DV]
        o_ref[:, h * ML_DV:(h + 1) * ML_DV] = (_sigmoid(og) * hn).astype(BF16)

    @pl.when(c == NC - 1)
    def _():
        Cout_ref[...] = C_scr[...]
        nout_ref[...] = n_scr[...]
        mout_ref[...] = m_scr[...]


def _mlstm_mixer(main, small, C0, n0, m0, bias, onorm, L):
    B, T, _ = main.shape
    L = min(L, T)
    assert T % L == 0
    NC = T // L
    m0b = jnp.broadcast_to(m0[:, :, None], (B, ML_HEADS, LANES))
    return pl.pallas_call(
        functools.partial(_mlstm_body, L, NC),
        grid=(B, NC),
        in_specs=[pl.BlockSpec((None, L, 1024), lambda b, c: (b, c, 0)),
                  pl.BlockSpec((None, L, 1024), lambda b, c: (b, c, 1)),
                  pl.BlockSpec((None, L, 1024), lambda b, c: (b, c, 2)),
                  pl.BlockSpec((None, L, LANES), lambda b, c: (b, c, 0)),
                  pl.BlockSpec((None, ML_HEADS, ML_DQK, ML_DV), lambda b, c: (b, 0, 0, 0)),
                  pl.BlockSpec((None, ML_HEADS, ML_DQK), lambda b, c: (b, 0, 0)),
                  pl.BlockSpec((None, ML_HEADS, LANES), lambda b, c: (b, 0, 0)),
                  pl.BlockSpec((1, LANES), lambda b, c: (0, 0)),
                  pl.BlockSpec((1, ML_DV), lambda b, c: (0, 0))],
        out_specs=[pl.BlockSpec((None, L, 1024), lambda b, c: (b, c, 0)),
                   pl.BlockSpec((None, ML_HEADS, ML_DQK, ML_DV), lambda b, c: (b, 0, 0, 0)),
                   pl.BlockSpec((None, ML_HEADS, ML_DQK), lambda b, c: (b, 0, 0)),
                   pl.BlockSpec((None, ML_HEADS, LANES), lambda b, c: (b, 0, 0))],
        out_shape=[jax.ShapeDtypeStruct((B, T, 1024), BF16),
                   jax.ShapeDtypeStruct((B, ML_HEADS, ML_DQK, ML_DV), F32),
                   jax.ShapeDtypeStruct((B, ML_HEADS, ML_DQK), F32),
                   jax.ShapeDtypeStruct((B, ML_HEADS, LANES), F32)],
        scratch_shapes=[pltpu.VMEM((ML_HEADS, ML_DQK, ML_DV), F32),
                        pltpu.VMEM((ML_HEADS, ML_DQK), F32),
                        pltpu.VMEM((ML_HEADS, LANES), F32)],
        compiler_params=_cparams(("parallel", "arbitrary")),
        name="mlstm_mixer",
    )(main, main, main, small, C0, n0, m0b, bias, onorm.reshape(1, ML_DV))


FOX_BIAS_LANE = FOX_DH


def _fox_prep_body(q_ref, k_ref, v_ref, small_ref, bf_ref, qn_ref, kn_ref,
                   qh_ref, k32_ref, v32_ref, lf_ref):
    r = lax.broadcasted_iota(jnp.int32, (LANES, LANES), 0) // FOX_DH
    cc = lax.broadcasted_iota(jnp.int32, (LANES, LANES), 1) // FOX_DH
    avg = jnp.where(r == cc, 1.0 / FOX_DH, 0.0).astype(BF16)

    def head_rms(x, gain):
        parts = []
        for t in range(x.shape[1] // LANES):
            xt = x[:, t * LANES:(t + 1) * LANES]
            hi, lo = _split2(xt * xt)
            ms = (jnp.dot(hi, avg, preferred_element_type=F32)
                  + jnp.dot(lo, avg, preferred_element_type=F32))
            parts.append(xt * lax.rsqrt(ms + EPS))
        return jnp.concatenate(parts, axis=-1) * gain

    qn = head_rms(q_ref[...], qn_ref[...]) * (LOG2E * FOX_DH ** -0.5)
    lane = lax.broadcasted_iota(jnp.int32, (1, LANES), 1)
    ones3 = jnp.where(lane < FOX_BIAS_LANE + 3, 1.0, 0.0)
    for h in range(FOX_HEADS):
        xt = qn[:, (h // 2) * LANES:(h // 2 + 1) * LANES]
        if h % 2:
            xt = pltpu.roll(xt, FOX_DH, axis=1)
        qh_ref[h] = jnp.where(lane < FOX_DH, xt, ones3).astype(BF16)
    k32_ref[...] = head_rms(k_ref[...], kn_ref[...])
    v32_ref[...] = v_ref[...]
    lf_ref[...] = jax.nn.log_sigmoid(small_ref[...] + bf_ref[...])


def _fox_prep(main, small, bf_row, qn_row, kn_row, tm=512):
    B, T, _ = main.shape
    tm = min(tm, T)
    assert T % tm == 0
    blk = lambda j: pl.BlockSpec((None, tm, 1024), lambda b, i: (b, i, j))
    row = lambda n: pl.BlockSpec((1, n), lambda b, i: (0, 0))
    o1024 = pl.BlockSpec((None, tm, 1024), lambda b, i: (b, i, 0))
    return pl.pallas_call(
        _fox_prep_body,
        grid=(B, T // tm),
        in_specs=[blk(0), blk(1), blk(2),
                  pl.BlockSpec((None, tm, LANES), lambda b, i: (b, i, 0)),
                  row(LANES), row(1024), row(1024)],
        out_specs=[pl.BlockSpec((None, FOX_HEADS, tm, LANES), lambda b, i: (b, 0, i, 0)),
                   o1024, o1024,
                   pl.BlockSpec((None, tm, LANES), lambda b, i: (b, i, 0))],
        out_shape=[jax.ShapeDtypeStruct((B, FOX_HEADS, T, LANES), BF16),
                   jax.ShapeDtypeStruct((B, T, 1024), F32),
                   jax.ShapeDtypeStruct((B, T, 1024), F32),
                   jax.ShapeDtypeStruct((B, T, LANES), F32)],
        compiler_params=_cparams(("parallel", "parallel")),
        name="fox_prep",
    )(main, main, main, small, bf_row, qn_row, kn_row)


def _fox_kv_body(tm, k_ref, v_ref, lf_ref, kh_ref, vh_ref, carry):
    @pl.when(pl.program_id(1) == 0)
    def _():
        carry[...] = jnp.zeros_like(carry)

    incl, _, _ = _tri_masks(tm)
    tril_b = jnp.where(incl, 1.0, 0.0).astype(BF16)
    F = _cumsum_rows(tril_b, lf_ref[...]) + carry[0:1, :]
    carry[0:1, :] = F[tm - 1:tm, :]
    nb = -LOG2E * F
    b1 = nb.astype(BF16).astype(F32)
    r1 = nb - b1
    b2 = r1.astype(BF16).astype(F32)
    b3 = r1 - b2
    lane = lax.broadcasted_iota(jnp.int32, (1, LANES), 1)
    ones1 = jnp.where(lane == FOX_BIAS_LANE, 1.0, 0.0)
    k = k_ref[...]
    v = v_ref[...]
    for h in range(FOX_HEADS):
        kt = k[:, (h // 2) * LANES:(h // 2 + 1) * LANES]
        vt = v[:, (h // 2) * LANES:(h // 2 + 1) * LANES]
        if h % 2:
            kt = pltpu.roll(kt, FOX_DH, axis=1)
            vt = pltpu.roll(vt, FOX_DH, axis=1)
        bias = jnp.where(lane == FOX_BIAS_LANE, b1[:, h:h + 1],
                         jnp.where(lane == FOX_BIAS_LANE + 1, b2[:, h:h + 1],
                                   jnp.where(lane == FOX_BIAS_LANE + 2, b3[:, h:h + 1], 0.0)))
        kh_ref[h] = jnp.where(lane < FOX_DH, kt, bias).astype(BF16)
        vh_ref[h] = jnp.where(lane < FOX_DH, vt, ones1).astype(BF16)


def _fox_kv(k_all, v_all, lf_all):
    B, N, _ = k_all.shape
    tm = next(t for t in (512, 384, 256, 128) if N % t == 0)
    hm = pl.BlockSpec((None, FOX_HEADS, tm, LANES), lambda b, i: (b, 0, i, 0))
    return pl.pallas_call(
        functools.partial(_fox_kv_body, tm),
        grid=(B, N // tm),
        in_specs=[pl.BlockSpec((None, tm, 1024), lambda b, i: (b, i, 0)),
                  pl.BlockSpec((None, tm, 1024), lambda b, i: (b, i, 0)),
                  pl.BlockSpec((None, tm, LANES), lambda b, i: (b, i, 0))],
        out_specs=[hm, hm],
        out_shape=[jax.ShapeDtypeStruct((B, FOX_HEADS, N, LANES), BF16),
                   jax.ShapeDtypeStruct((B, FOX_HEADS, N, LANES), BF16)],
        scratch_shapes=[pltpu.VMEM((8, LANES), F32)],
        compiler_params=_cparams(("parallel", "arbitrary")),
        name="fox_kv",
    )(k_all, v_all, lf_all)


def _flash_body(P, tq, tk, rq, qi_tab, kj_tab, q_ref, k_ref, v_ref, gate_ref, o_ref, m_scr, acc_scr):
    s_idx = pl.program_id(2)
    qi = qi_tab[s_idx]
    kj = kj_tab[s_idx]
    last_kj = (P + (qi + 1) * tq - 1) // tk
    aligned = tq == tk and P % tk == 0

    @pl.when(kj == 0)
    def _():
        m_scr[...] = jnp.full(m_scr.shape, NEG_BIG, F32)
        acc_scr[...] = jnp.zeros_like(acc_scr)

    def step(masked):
        units = [(hh, r) for r in range(tq // rq) for hh in range(2)]

        def logits(u):
            hh, r = u
            rows = slice(r * rq, (r + 1) * rq)
            nk = min(tk, (r + 1) * rq) if (masked and aligned) else tk
            s = lax.dot_general(q_ref[hh, rows, :], k_ref[hh, 0:nk, :], (((1,), (1,)), ((), ())),
                                preferred_element_type=F32)
            if masked:
                qpos = P + qi * tq + r * rq + lax.broadcasted_iota(jnp.int32, (rq, nk), 0)
                kpos = kj * tk + lax.broadcasted_iota(jnp.int32, (rq, nk), 1)
                s = jnp.where(kpos <= qpos, s, NEG_BIG)
            return s, nk

        def probs(u, s):
            hh, r = u
            rows = slice(r * rq, (r + 1) * rq)
            m_prev = m_scr[hh, rows, :][:, 0:1]
            m_new = jnp.maximum(m_prev, jnp.max(s, axis=-1, keepdims=True))
            m_scr[hh, rows, :] = jnp.broadcast_to(m_new, (rq, LANES))
            return jnp.exp2(m_prev - m_new), jnp.exp2(s - m_new).astype(BF16)

        def accumulate(u, alpha, p, nk):
            hh, r = u
            rows = slice(r * rq, (r + 1) * rq)
            acc_scr[hh, rows, :] = alpha * acc_scr[hh, rows, :] + jnp.dot(
                p, v_ref[hh, 0:nk, :], preferred_element_type=F32)

        n = len(units)
        pend_s = {0: logits(units[0])}
        pend_p = {}
        for i in range(n):
            if i + 1 < n:
                pend_s[i + 1] = logits(units[i + 1])
            s, nk = pend_s.pop(i)
            pend_p[i] = probs(units[i], s) + (nk,)
            if i >= 1:
                accumulate(units[i - 1], *pend_p.pop(i - 1))
        accumulate(units[n - 1], *pend_p.pop(n - 1))

    needs_mask = (kj + 1) * tk - 1 > P + qi * tq

    @pl.when(needs_mask)
    def _():
        step(True)

    @pl.when(jnp.logical_not(needs_mask))
    def _():
        step(False)

    @pl.when(kj == last_kj)
    def _():
        lane = lax.broadcasted_iota(jnp.int32, (1, LANES), 1)
        a0 = acc_scr[0]
        a1 = acc_scr[1]
        o0 = a0 * (1.0 / a0[:, FOX_BIAS_LANE:FOX_BIAS_LANE + 1])
        o1 = a1 * (1.0 / a1[:, FOX_BIAS_LANE:FOX_BIAS_LANE + 1])
        o = jnp.where(lane < FOX_DH, o0, pltpu.roll(o1, FOX_DH, axis=1))
        g = gate_ref[...]
        o_ref[...] = (o * _sigmoid(g)).astype(BF16)


def _fox_flash(qh, kh, vh, gate_src, P, tq, tk, rq):
    B, _, T, _ = qh.shape
    N = kh.shape[2]
    assert T % tq == 0 and N % tk == 0 and tq % rq == 0
    nq = T // tq
    pairs = [(qi, kj) for qi in range(nq) for kj in range((P + (qi + 1) * tq - 1) // tk + 1)]
    qi_tab = jnp.asarray(np.array([p[0] for p in pairs], np.int32))
    kj_tab = jnp.asarray(np.array([p[1] for p in pairs], np.int32))
    HP = FOX_HEADS // 2
    gate_blk0 = 3 * (1024 // LANES)
    grid_spec = pltpu.PrefetchScalarGridSpec(
        num_scalar_prefetch=2,
        grid=(B, HP, len(pairs)),
        in_specs=[pl.BlockSpec((None, 2, tq, LANES), lambda b, p, s, qt, kt: (b, p, qt[s], 0)),
                  pl.BlockSpec((None, 2, tk, LANES), lambda b, p, s, qt, kt: (b, p, kt[s], 0)),
                  pl.BlockSpec((None, 2, tk, LANES), lambda b, p, s, qt, kt: (b, p, kt[s], 0)),
                  pl.BlockSpec((None, tq, LANES), lambda b, p, s, qt, kt: (b, qt[s], gate_blk0 + p))],
        out_specs=pl.BlockSpec((None, tq, LANES), lambda b, p, s, qt, kt: (b, qt[s], p)),
        scratch_shapes=[pltpu.VMEM((2, tq, LANES), F32),
                        pltpu.VMEM((2, tq, LANES), F32)],
    )
    return pl.pallas_call(
        functools.partial(_flash_body, P, tq, tk, rq),
        grid_spec=grid_spec,
        out_shape=jax.ShapeDtypeStruct((B, T, 1024), BF16),
        compiler_params=_cparams(("parallel", "parallel", "arbitrary")),
        name="fox_flash",
    )(qi_tab, kj_tab, qh, kh, vh, gate_src)


def _pad_lanes(a, offset=0):
    n = a.shape[-1]
    pads = [(0, 0)] * (a.ndim - 1) + [(offset, LANES - n - offset)]
    return jnp.pad(a, pads)


def _trunk(x, gdn_conv, gdn_S, ml_C, ml_n, ml_m, fox_k, fox_v, fox_lf, mem_k, mem_v, W, chunk):
    B, T, D = x.shape
    M = B * T
    x2 = x.reshape(M, D)
    conv_o, S_o, C_o, n_o, m_o, k_o, v_o, lf_o = [], [], [], [], [], [], [], []
    for layer in range(DEPTH):
        kind, j = layer % N_MIXERS, layer // N_MIXERS
        gains = W["norm_gains"][layer]
        if kind == 0:
            main, small = _prenorm_mm(x2, gains[0], W["gdn_w_main"], j, W["gdn_w_small"])
            o, buf, S = _gdn_mixer(main.reshape(B, T, GDN_MAIN), small.reshape(B, T, LANES),
                                   gdn_conv[j], gdn_S[j], W["gdn_wc_t"][j], W["gdn_par"][j],
                                   W["gdn_onorm"][j])
            conv_o.append(buf)
            S_o.append(S)
            w_out = W["gdn_w_out"]
        elif kind == 1:
            main, small = _prenorm_mm(x2, gains[0], W["ml_w_main"], j, W["ml_w_small"])
            o, C, n, m = _mlstm_mixer(main.reshape(B, T, ML_MAIN), small.reshape(B, T, LANES),
                                      ml_C[j], ml_n[j], ml_m[j], W["ml_bias"][j], W["ml_onorm"][j], chunk)
            C_o.append(C)
            n_o.append(n)
            m_o.append(m[:, :, 0])
            w_out = W["ml_w_out"]
        else:
            main, small = _prenorm_mm(x2, gains[0], W["fox_w_main"], j, W["fox_w_small"])
            main3 = main.reshape(B, T, FOX_MAIN)
            qh, k32, v32, lfp = _fox_prep(main3, small.reshape(B, T, LANES), W["fox_bf"][j],
                                          W["fox_qn"][j], W["fox_kn"][j])
            P = fox_k[j].shape[1]
            if P == 0:
                k_all, v_all, lf_all, tq, tk, rq = k32, v32, lfp, 1024, 1024, 256
            else:
                npad = -(P + T) % LANES
                zkv = jnp.zeros((B, npad, 1024), F32)
                k_all = jnp.concatenate([fox_k[j].reshape(B, P, 1024), k32, zkv], axis=1)
                v_all = jnp.concatenate([fox_v[j].reshape(B, P, 1024), v32, zkv], axis=1)
                lf_all = jnp.concatenate([_pad_lanes(fox_lf[j]), lfp, jnp.zeros((B, npad, LANES), F32)], axis=1)
                tq, tk, rq = T, P + T + npad, T
            kh, vh = _fox_kv(k_all, v_all, lf_all)
            o = _fox_flash(qh, kh, vh, main3, P, tq, tk, rq)
            k_o.append(k32.reshape(B, T, FOX_HEADS, FOX_DH))
            v_o.append(v32.reshape(B, T, FOX_HEADS, FOX_DH))
            lf_o.append(lfp[:, :, :FOX_HEADS])
            w_out = W["fox_w_out"]
        (mk_arr, kcol), (mv_arr, vcol) = mem_k[layer], mem_v[layer]
        x2 = _memattn(o, w_out, j, gains[1], x2.reshape(B, T, D), gains[2], W["x_w_q"],
                      mk_arr, kcol, mv_arr, vcol, W["x_w_o"], layer, gains[3]).reshape(M, D)
        x2 = _ffn(x2, gains[4], W["ffn_w_gu"], W["ffn_w_down"], layer, gains[5])
    return (x2.reshape(B, T, D), jnp.stack(conv_o), jnp.stack(S_o), jnp.stack(C_o), jnp.stack(n_o),
            jnp.stack(m_o), jnp.stack(k_o), jnp.stack(v_o), jnp.stack(lf_o))


def kernel(x_prompt, x_sample, state_gdn_conv, state_gdn_S, state_mlstm_C, state_mlstm_n, state_mlstm_m, cache_fox_k, cache_fox_v, cache_fox_logf, cache_mem_k, cache_mem_v, mem_prompt, norm_gains, gdn_w_in, gdn_w_conv, gdn_a_log, gdn_dt_bias, gdn_onorm, gdn_w_out, ml_w_in, ml_b_i, ml_b_f, ml_onorm, ml_w_out, fox_w_in, fox_b_f, fox_qk_norm, fox_w_out, x_w_q, x_w_kv, x_w_o, ffn_w_gu, ffn_w_down):
    B = x_prompt.shape[0]
    DB = x_sample.shape[0]
    n_a, n_b, n_c = gdn_w_in.shape[0], ml_w_in.shape[0], fox_w_in.shape[0]
    W = {
        "norm_gains": norm_gains,
        "gdn_w_main": gdn_w_in[:, :, :GDN_MAIN].astype(BF16),
        "gdn_w_small": _pad_lanes(gdn_w_in[:, :, GDN_MAIN:]).astype(BF16),
        "gdn_wc_t": jnp.transpose(gdn_w_conv, (0, 2, 1)),
        "gdn_par": jnp.stack([_pad_lanes(gdn_a_log, GDN_HEADS), _pad_lanes(gdn_dt_bias, GDN_HEADS)], axis=1),
        "gdn_onorm": gdn_onorm,
        "gdn_w_out": gdn_w_out.astype(BF16),
        "ml_w_main": ml_w_in[:, :, :ML_MAIN].astype(BF16),
        "ml_w_small": _pad_lanes(ml_w_in[:, :, ML_MAIN:]).astype(BF16),
        "ml_bias": _pad_lanes(jnp.concatenate([ml_b_i, ml_b_f], axis=-1))[:, None, :],
        "ml_onorm": ml_onorm,
        "ml_w_out": ml_w_out.astype(BF16),
        "fox_w_main": fox_w_in[:, :, :FOX_MAIN].astype(BF16),
        "fox_w_small": _pad_lanes(fox_w_in[:, :, FOX_MAIN:]).astype(BF16),
        "fox_bf": _pad_lanes(fox_b_f)[:, None, :],
        "fox_qn": jnp.tile(fox_qk_norm[:, 0, :], (1, FOX_HEADS))[:, None, :],
        "fox_kn": jnp.tile(fox_qk_norm[:, 1, :], (1, FOX_HEADS))[:, None, :],
        "fox_w_out": fox_w_out.astype(BF16),
        "x_w_q": x_w_q.astype(BF16),
        "x_w_o": x_w_o.astype(BF16),
        "ffn_w_gu": ffn_w_gu.astype(BF16),
        "ffn_w_down": ffn_w_down.astype(BF16),
    }

    mem2 = mem_prompt.reshape(B * N_MEM, D_MODEL)
    x_w_kv_b = x_w_kv.astype(BF16)
    kv = [_prenorm_mm(mem2, norm_gains[l, 6], x_w_kv_b, l).reshape(B, N_MEM, 2 * D_MODEL)
          for l in range(DEPTH)]
    p_mem_k = jnp.stack([a[:, :, :D_MODEL].reshape(B, N_MEM, X_HEADS, X_DH) for a in kv])
    p_mem_v = jnp.stack([a[:, :, D_MODEL:].reshape(B, N_MEM, X_HEADS, X_DH) for a in kv])

    zeros = lambda *s: jnp.zeros(s, F32)
    prompt = _trunk(
        x_prompt,
        zeros(n_a, B, CONV_W - 1, GDN_QKV), zeros(n_a, B, GDN_HEADS, GDN_DK, GDN_DV),
        zeros(n_b, B, ML_HEADS, ML_DQK, ML_DV), zeros(n_b, B, ML_HEADS, ML_DQK), zeros(n_b, B, ML_HEADS),
        zeros(n_c, B, 0, FOX_HEADS, FOX_DH), zeros(n_c, B, 0, FOX_HEADS, FOX_DH), zeros(n_c, B, 0, FOX_HEADS),
        [(a, 0) for a in kv], [(a, 1) for a in kv], W, 64)
    sample = _trunk(
        x_sample, state_gdn_conv, state_gdn_S, state_mlstm_C, state_mlstm_n, state_mlstm_m,
        cache_fox_k, cache_fox_v, cache_fox_logf,
        [(cache_mem_k[l].reshape(DB, N_MEM, D_MODEL), 0) for l in range(DEPTH)],
        [(cache_mem_v[l].reshape(DB, N_MEM, D_MODEL), 0) for l in range(DEPTH)], W, 64)
    (y_p, p_conv, p_S, p_C, p_n, p_m, p_k, p_v, p_lf) = prompt
    (y_s, s_conv, s_S, s_C, s_n, s_m, s_k, s_v, s_lf) = sample
    return (y_p, y_s, p_conv, p_S, p_C, p_n, p_m, p_k, p_v, p_lf, p_mem_k, p_mem_v,
            s_conv, s_S, s_C, s_n, s_m, s_k, s_v, s_lf)
```

```python
import functools

import numpy as np
import jax
import jax.numpy as jnp
from jax import lax
from jax.experimental import pallas as pl
from jax.experimental.pallas import tpu as pltpu

F32 = jnp.float32
BF16 = jnp.bfloat16

D_MODEL = 1024
DEPTH = 4
N_MIXERS = 3
EPS = 1e-6
CONV_W = 4

GDN_HEADS = 8
GDN_DK = 128
GDN_DV = 128
GDN_QKV = 3072
GDN_MAIN = 4096

ML_HEADS = 4
ML_DQK = 128
ML_DV = 256
ML_MAIN = 3072
GATE_CAP = 15.0

FOX_HEADS = 16
FOX_DH = 64
FOX_MAIN = 4096

N_MEM = 256
X_HEADS = 4
X_DH = 256
D_FF = 2816

LANES = 128
NEG_BIG = -1e30
LOG2E = 1.4426950408889634

VMEM_LIMIT = 48 * 1024 * 1024


def _wspec(li, blk, imap):
    return pl.BlockSpec((None,) + blk, lambda *g: (li,) + imap(*g))


def _cparams(sem):
    return pltpu.CompilerParams(dimension_semantics=sem, vmem_limit_bytes=VMEM_LIMIT)


def _rms(x, g):
    return x * lax.rsqrt(jnp.mean(x * x, axis=-1, keepdims=True) + EPS) * g


def _sigmoid(x):
    return 1.0 / (1.0 + jnp.exp(-x))


def _dot(a, b):
    return jnp.dot(a.astype(BF16), b.astype(BF16), preferred_element_type=F32)


def _dot_nt(a, b):
    return lax.dot_general(a.astype(BF16), b.astype(BF16), (((1,), (1,)), ((), ())),
                           preferred_element_type=F32)


def _dot_tn(a, b):
    return lax.dot_general(a.astype(BF16), b.astype(BF16), (((0,), (0,)), ((), ())),
                           preferred_element_type=F32)


def _split2(x):
    hi = x.astype(BF16)
    lo = (x - hi.astype(F32)).astype(BF16)
    return hi, lo


def _dot_x3(a, b):
    ah, al = _split2(a)
    bh, bl = _split2(b)
    d = lambda u, v: jnp.dot(u, v, preferred_element_type=F32)
    return d(ah, bh) + (d(ah, bl) + d(al, bh))


def _cumsum_rows(tril_b, x):
    h = x.astype(BF16)
    r = x - h.astype(F32)
    m = r.astype(BF16)
    l = (r - m.astype(F32)).astype(BF16)
    d = lambda v: jnp.dot(tril_b, v, preferred_element_type=F32)
    return d(h) + (d(m) + d(l))


def _tri_masks(L):
    row = lax.broadcasted_iota(jnp.int32, (L, L), 0)
    col = lax.broadcasted_iota(jnp.int32, (L, L), 1)
    return row >= col, row > col, row == col


def _prenorm_mm_body(has_small, *refs):
    if has_small:
        x_ref, g_ref, w_ref, ws_ref, o_ref, os_ref, xn_ref = refs
    else:
        x_ref, g_ref, w_ref, o_ref, xn_ref = refs

    @pl.when(pl.program_id(1) == 0)
    def _():
        xb = _rms(x_ref[...], g_ref[...]).astype(BF16)
        xn_ref[...] = xb
        if has_small:
            os_ref[...] = jnp.dot(xb, ws_ref[...], preferred_element_type=F32)

    o_ref[...] = jnp.dot(xn_ref[...], w_ref[...], preferred_element_type=F32).astype(o_ref.dtype)


def _prenorm_mm(x, gain, w_main, li, n_out, w_small=None, out_dtype=BF16, tm=1024, tn=1024):
    M, D = x.shape
    N = n_out
    tm = min(tm, M)
    tn = min(tn, N)
    assert M % tm == 0 and N % tn == 0
    has_small = w_small is not None
    in_specs = [pl.BlockSpec((tm, D), lambda i, j: (i, 0)),
                pl.BlockSpec((1, D), lambda i, j: (0, 0)),
                _wspec(li, (D, tn), lambda i, j: (0, j))]
    out_specs = [pl.BlockSpec((tm, tn), lambda i, j: (i, j))]
    out_shape = [jax.ShapeDtypeStruct((M, N), out_dtype)]
    args = [x, gain.reshape(1, D), w_main]
    if has_small:
        in_specs.append(_wspec(li, (D, LANES), lambda i, j: (0, 0)))
        out_specs.append(pl.BlockSpec((tm, LANES), lambda i, j: (i, 0)))
        out_shape.append(jax.ShapeDtypeStruct((M, LANES), F32))
        args.append(w_small)
    res = pl.pallas_call(
        functools.partial(_prenorm_mm_body, has_small),
        grid=(M // tm, N // tn),
        in_specs=in_specs, out_specs=out_specs, out_shape=out_shape,
        scratch_shapes=[pltpu.VMEM((tm, D), BF16)],
        compiler_params=_cparams(("parallel", "arbitrary")),
        name="prenorm_mm",
    )(*args)
    return res if has_small else res[0]


def _memattn_body(nb, o_ref, wout_ref, g1_ref, x_ref, g2_ref, wq_ref, mk_ref, mv_ref, wo_ref, g3_ref, y_ref):
    tq, D = x_ref.shape[1], x_ref.shape[2]
    R = nb * tq
    o_in = o_ref[...].reshape(R, o_ref.shape[2])
    x = x_ref[...].reshape(R, D)
    x = x + _rms(jnp.dot(o_in, wout_ref[...], preferred_element_type=F32), g1_ref[...])
    h = _rms(x, g2_ref[...]).astype(BF16)
    q = jnp.dot(h, wq_ref[...], preferred_element_type=F32).astype(BF16)
    rows = []
    for bi in range(nb):
        mk = mk_ref[bi].astype(BF16)
        mv = mv_ref[bi].astype(BF16)
        qb = q[bi * tq:(bi + 1) * tq]
        outs = []
        for hd in range(X_HEADS):
            sl = slice(hd * X_DH, (hd + 1) * X_DH)
            s = _dot_nt(qb[:, sl], mk[:, sl]) * (X_DH ** -0.5)
            p = jnp.exp(s - jnp.max(s, axis=-1, keepdims=True))
            p = p * (1.0 / jnp.sum(p, axis=-1, keepdims=True))
            outs.append(jnp.dot(p.astype(BF16), mv[:, sl], preferred_element_type=F32))
        rows.append(jnp.concatenate(outs, axis=-1).astype(BF16))
    o = rows[0] if nb == 1 else jnp.concatenate(rows, axis=0)
    y = jnp.dot(o, wo_ref[...], preferred_element_type=F32)
    y_ref[...] = (x + _rms(y, g3_ref[...])).reshape(nb, tq, D)


def _memattn(o, w_out, lo, g1, x, g2, wq, mem_k, mem_v, wo, lx, g3, tq=512, rows_per_step=256):
    B, T, D = x.shape
    tq = min(tq, T)
    assert T % tq == 0
    nb = max(1, min(B, rows_per_step // tq)) if tq == T else 1
    assert B % nb == 0
    HD = X_HEADS * X_DH
    K = o.shape[-1]
    (mk_arr, koff, kcol), (mv_arr, voff, vcol) = mem_k, mem_v
    assert koff % nb == 0 and voff % nb == 0
    return pl.pallas_call(
        functools.partial(_memattn_body, nb),
        grid=(B // nb, T // tq),
        in_specs=[pl.BlockSpec((nb, tq, K), lambda b, i: (b, i, 0)),
                  _wspec(lo, (K, D), lambda b, i: (0, 0)),
                  pl.BlockSpec((1, D), lambda b, i: (0, 0)),
                  pl.BlockSpec((nb, tq, D), lambda b, i: (b, i, 0)),
                  pl.BlockSpec((1, D), lambda b, i: (0, 0)),
                  _wspec(lx, (D, HD), lambda b, i: (0, 0)),
                  pl.BlockSpec((nb, N_MEM, HD), lambda b, i: (koff // nb + b, 0, kcol)),
                  pl.BlockSpec((nb, N_MEM, HD), lambda b, i: (voff // nb + b, 0, vcol)),
                  _wspec(lx, (HD, D), lambda b, i: (0, 0)),
                  pl.BlockSpec((1, D), lambda b, i: (0, 0))],
        out_specs=pl.BlockSpec((nb, tq, D), lambda b, i: (b, i, 0)),
        out_shape=jax.ShapeDtypeStruct((B, T, D), F32),
        compiler_params=_cparams(("parallel", "parallel")),
        name="memattn",
    )(o, w_out, g1.reshape(1, D), x, g2.reshape(1, D), wq, mk_arr, mv_arr, wo, g3.reshape(1, D))


def _ffn_body(nf, x_ref, g4_ref, wg_ref, wu_ref, wd_ref, g5_ref, y_ref, xn_ref, acc_ref):
    f = pl.program_id(1)

    @pl.when(f == 0)
    def _():
        xn_ref[...] = _rms(x_ref[...], g4_ref[...]).astype(BF16)
        acc_ref[...] = jnp.zeros_like(acc_ref)

    xn = xn_ref[...]
    g = jnp.dot(xn, wg_ref[...], preferred_element_type=F32)
    u = jnp.dot(xn, wu_ref[...], preferred_element_type=F32)
    a = (g * _sigmoid(g)) * u
    acc_ref[...] += jnp.dot(a.astype(BF16), wd_ref[...], preferred_element_type=F32)

    @pl.when(f == nf - 1)
    def _():
        y_ref[...] = x_ref[...] + _rms(acc_ref[...], g5_ref[...])


def _ffn(x, g4, w_gu, w_down, li, g5, tm=1024, tf=256):
    M, D = x.shape
    tm = min(tm, M)
    assert M % tm == 0 and D_FF % tf == 0
    nf = D_FF // tf
    return pl.pallas_call(
        functools.partial(_ffn_body, nf),
        grid=(M // tm, nf),
        in_specs=[pl.BlockSpec((tm, D), lambda i, f: (i, 0)),
                  pl.BlockSpec((1, D), lambda i, f: (0, 0)),
                  _wspec(li, (D, tf), lambda i, f: (0, f)),
                  _wspec(li, (D, tf), lambda i, f: (0, nf + f)),
                  _wspec(li, (tf, D), lambda i, f: (f, 0)),
                  pl.BlockSpec((1, D), lambda i, f: (0, 0))],
        out_specs=pl.BlockSpec((tm, D), lambda i, f: (i, 0)),
        out_shape=jax.ShapeDtypeStruct((M, D), F32),
        scratch_shapes=[pltpu.VMEM((tm, D), BF16), pltpu.VMEM((tm, D), F32)],
        compiler_params=_cparams(("parallel", "arbitrary")),
        name="ffn",
    )(x, g4.reshape(1, D), w_gu, w_gu, w_down, g5.reshape(1, D))


GDN_CHUNK = 64
GDN_BASE = 8


def _inverse_masks(L):
    row = lax.broadcasted_iota(jnp.int32, (L, L), 0)
    col = lax.broadcasted_iota(jnp.int32, (L, L), 1)
    base = row // GDN_BASE == col // GDN_BASE
    offs = []
    s = GDN_BASE
    while s < L:
        same_pair = row // (2 * s) == col // (2 * s)
        offs.append(jnp.logical_and(same_pair, (row // s) - (col // s) == 1))
        s *= 2
    return base, row == col, offs


def _unit_lower_inverse(As, masks, mm):
    base, diag, offs = masks
    eye = jnp.where(diag, 1.0, 0.0)
    Ps = [jnp.where(base, -A, 0.0) for A in As]
    Ts = [eye + P for P in Ps]
    k = 1
    while k < GDN_BASE // 2:
        Ps = [mm(P, P) for P in Ps]
        Ts = [T + mm(T, P) for T, P in zip(Ts, Ps)]
        k *= 2
    for off in offs:
        Ms = [mm(T, jnp.where(off, A, 0.0)) for T, A in zip(Ts, As)]
        Ts = [T - mm(M, T) for T, M in zip(Ts, Ms)]
    return Ts


def _gdn_prep_body(L, nch, qkv_ref, prev_ref, small_ref, buf_ref, wc_ref, par_ref,
                   wk_ref, uv_ref, qt_ref, kt_ref, at_ref, egl_ref, xbuf):
    c = pl.program_id(1)
    LA = L * nch

    @pl.when(c == 0)
    def _():
        xbuf[0:8, :] = jnp.zeros((8, GDN_QKV), F32)
        xbuf[5:8, :] = buf_ref[...]

    @pl.when(c > 0)
    def _():
        xbuf[0:8, :] = prev_ref[8:16, :].astype(F32)

    xbuf[8:8 + LA, :] = qkv_ref[...].astype(F32)
    y = xbuf[5:5 + LA, :] * wc_ref[0:1, :]
    for j in range(1, CONV_W):
        y = y + xbuf[5 + j:5 + j + LA, :] * wc_ref[j:j + 1, :]
    y = y * _sigmoid(y)

    small = small_ref[...]
    beta_all = _sigmoid(small)
    g_all = -jnp.exp(par_ref[0:1, :]) * jax.nn.softplus(small + par_ref[1:2, :])
    incl, strict, _ = _tri_masks(L)
    tril_b = jnp.where(incl, 1.0, 0.0).astype(BF16)
    masks = _inverse_masks(L)
    H = GDN_HEADS

    units = []
    for ci in range(nch):
        rows = slice(ci * L, (ci + 1) * L)
        G_all = _cumsum_rows(tril_b, g_all[rows])
        GT = G_all.T
        egl_ref[ci] = jnp.broadcast_to(jnp.exp(G_all[L - 1:L, :]), (8, LANES))
        for h in range(H):
            qh = y[rows, h * GDN_DK:(h + 1) * GDN_DK]
            kh = y[rows, (H + h) * GDN_DK:(H + h + 1) * GDN_DK]
            vh = y[rows, 2 * H * GDN_DK + h * GDN_DV:2 * H * GDN_DK + (h + 1) * GDN_DV]
            qh = qh * lax.rsqrt(jnp.sum(qh * qh, axis=-1, keepdims=True) + EPS) * (GDN_DK ** -0.5)
            kh = kh * lax.rsqrt(jnp.sum(kh * kh, axis=-1, keepdims=True) + EPS)
            bc = beta_all[rows, h:h + 1]
            Gc = G_all[:, H + h:H + h + 1]
            Gr = GT[H + h:H + h + 1, :]
            eG = jnp.exp(Gc)
            dec = jnp.exp(jnp.where(incl, Gc - Gr, NEG_BIG))
            qkk = _dot_nt(jnp.concatenate([qh, kh], axis=0), kh)
            A = jnp.where(strict, bc * qkk[L:] * dec, 0.0)
            rhs = jnp.concatenate([bc * vh, (bc * eG) * kh], axis=-1)
            cols = slice(h * GDN_DV, (h + 1) * GDN_DV)
            qt_ref[rows, cols] = (eG * qh).astype(BF16)
            kt_ref[rows, cols] = (kh * jnp.exp(Gc[L - 1:L, :] - Gc)).astype(BF16)
            at_ref[rows, cols] = jnp.concatenate(
                [qkk[:L] * dec, jnp.zeros((L, GDN_DV - L), F32)], axis=-1).astype(BF16)
            units.append((rows, cols, A, rhs))

    Ts = _unit_lower_inverse([u[2] for u in units], masks, _dot_x3)
    for (rows, cols, _, rhs), T in zip(units, Ts):
        sol = _dot_x3(T, rhs)
        uv_ref[rows, cols] = sol[:, :GDN_DV]
        wk_ref[rows, cols] = sol[:, GDN_DV:].astype(BF16)


def _gdn_scan_body(L, cb, NS, wk_ref, uv_ref, qt_ref, kt_ref, at_ref, egl_ref, gate_ref, S0_ref, on_ref,
                   o_ref, Sout_ref, S_scr):
    c = pl.program_id(1)

    @pl.when(c == 0)
    def _():
        S_scr[...] = S0_ref[...]

    onorm = on_ref[...]
    H = GDN_HEADS
    for ci in range(cb):
        rows = slice(ci * L, (ci + 1) * L)
        egl = egl_ref[ci]
        hs = [slice(h * GDN_DV, (h + 1) * GDN_DV) for h in range(H)]
        Ss = [S_scr[h] for h in range(H)]
        Xs = [jnp.dot(jnp.concatenate([wk_ref[rows, hs[h]], qt_ref[rows, hs[h]]], axis=0),
                      Ss[h].astype(BF16), preferred_element_type=F32) for h in range(H)]
        Us = [(uv_ref[rows, hs[h]] - Xs[h][:L]).astype(BF16) for h in range(H)]
        os = [Xs[h][L:] + jnp.dot(at_ref[rows, h * GDN_DV:h * GDN_DV + L], Us[h],
                                  preferred_element_type=F32) for h in range(H)]
        for h in range(H):
            S_scr[h] = egl[0:1, H + h:H + h + 1] * Ss[h] + lax.dot_general(
                kt_ref[rows, hs[h]], Us[h], (((0,), (0,)), ((), ())), preferred_element_type=F32)
        for h in range(H):
            gh = gate_ref[rows, hs[h]].astype(F32)
            o_ref[rows, hs[h]] = (_rms(os[h], onorm) * (gh * _sigmoid(gh))).astype(BF16)

    @pl.when(c == NS - 1)
    def _():
        Sout_ref[...] = S_scr[...]


def _gdn_mixer(main, small, conv_buf, S0, wc_t, par, onorm, nch=4, cb=8):
    B, T, _ = main.shape
    L = min(GDN_CHUNK, T)
    assert T % L == 0
    NC = T // L
    nch = min(nch, NC)
    cb = min(cb, NC)
    assert NC % nch == 0 and NC % cb == 0
    LA = L * nch
    tok = lambda n, j: pl.BlockSpec((None, n, 1024), lambda b, c: (b, c, j))
    sds = lambda dt: jax.ShapeDtypeStruct((B, T, 1024), dt)
    wk, uv, qt, kt, at, egl = pl.pallas_call(
        functools.partial(_gdn_prep_body, L, nch),
        grid=(B, NC // nch),
        in_specs=[pl.BlockSpec((None, LA, GDN_QKV), lambda b, c: (b, c, 0)),
                  pl.BlockSpec((None, 16, GDN_QKV), lambda b, c: (b, jnp.maximum(c * (LA // 16) - 1, 0), 0)),
                  pl.BlockSpec((None, LA, LANES), lambda b, c: (b, c, 0)),
                  pl.BlockSpec((None, CONV_W - 1, GDN_QKV), lambda b, c: (b, 0, 0)),
                  pl.BlockSpec((CONV_W, GDN_QKV), lambda b, c: (0, 0)),
                  pl.BlockSpec((2, LANES), lambda b, c: (0, 0))],
        out_specs=[tok(LA, 0), tok(LA, 0), tok(LA, 0), tok(LA, 0), tok(LA, 0),
                   pl.BlockSpec((None, nch, 8, LANES), lambda b, c: (b, c, 0, 0))],
        out_shape=[sds(BF16), sds(F32), sds(BF16), sds(BF16), sds(BF16),
                   jax.ShapeDtypeStruct((B, NC, 8, LANES), F32)],
        scratch_shapes=[pltpu.VMEM((LA + 8, GDN_QKV), F32)],
        compiler_params=_cparams(("parallel", "parallel")),
        name="gdn_prep",
    )(main, main, small, conv_buf, wc_t, par)
    NS = NC // cb
    state = pl.BlockSpec((None, GDN_HEADS, GDN_DK, GDN_DV), lambda b, c: (b, 0, 0, 0))
    o, S = pl.pallas_call(
        functools.partial(_gdn_scan_body, L, cb, NS),
        grid=(B, NS),
        in_specs=[tok(cb * L, 0), tok(cb * L, 0), tok(cb * L, 0), tok(cb * L, 0), tok(cb * L, 0),
                  pl.BlockSpec((None, cb, 8, LANES), lambda b, c: (b, c, 0, 0)),
                  tok(cb * L, 3), state,
                  pl.BlockSpec((1, GDN_DV), lambda b, c: (0, 0))],
        out_specs=[tok(cb * L, 0), state],
        out_shape=[sds(BF16), jax.ShapeDtypeStruct((B, GDN_HEADS, GDN_DK, GDN_DV), F32)],
        scratch_shapes=[pltpu.VMEM((GDN_HEADS, GDN_DK, GDN_DV), F32)],
        compiler_params=_cparams(("parallel", "arbitrary")),
        name="gdn_scan",
    )(wk, uv, qt, kt, at, egl, main, S0, onorm.reshape(1, GDN_DV))
    new_buf = main[:, T - (CONV_W - 1):, :GDN_QKV].astype(F32)
    return o, new_buf, S


ML_CHUNK = 64


def _mlstm_gates(small_ref, bias_ref):
    sc = GATE_CAP * jnp.tanh((small_ref[...] + bias_ref[...]) / GATE_CAP)
    return sc, jax.nn.log_sigmoid(sc)


def _mlstm_prep_body(L, nch, qk_ref, v_ref, small_ref, bias_ref, hl_ref, st_ref, kv_ref, ks_ref):
    ig_all, lf_all = _mlstm_gates(small_ref, bias_ref)
    incl, _, _ = _tri_masks(L)
    tril_b = jnp.where(incl, 1.0, 0.0).astype(BF16)
    lane = lax.broadcasted_iota(jnp.int32, (1, LANES), 1)
    H = ML_HEADS
    units = [(ci, h) for ci in range(nch) for h in range(H)]
    rows_of = lambda ci: slice(ci * L, (ci + 1) * L)
    q_of = lambda ci, h: qk_ref[rows_of(ci), h * ML_DQK:(h + 1) * ML_DQK]
    k_of = lambda ci, h: qk_ref[rows_of(ci), (H + h) * ML_DQK:(H + h + 1) * ML_DQK].astype(F32) * (ML_DQK ** -0.5)
    v_of = lambda ci, h: v_ref[rows_of(ci), h * ML_DV:(h + 1) * ML_DV]

    gates = []
    for ci in range(nch):
        b_all = _cumsum_rows(tril_b, lf_all[rows_of(ci)])
        igc = ig_all[rows_of(ci)]
        gates.append((b_all, igc, b_all.T, igc.T))
    qk = [_dot_nt(q_of(ci, h), k_of(ci, h)) for ci, h in units]
    Wl, dmaxs = [], []
    for (ci, h), s in zip(units, qk):
        b_all, igc, bT, igT = gates[ci]
        D = jnp.where(incl, b_all[:, H + h:H + h + 1] + (igT[h:h + 1, :] - bT[H + h:H + h + 1, :]), NEG_BIG)
        dmax = jnp.max(D, axis=-1, keepdims=True)
        dmaxs.append(dmax)
        Wl.append(jnp.exp(D - dmax) * s)
    for (ci, h), w in zip(units, Wl):
        hl_ref[rows_of(ci), h * ML_DV:(h + 1) * ML_DV] = _dot(w, v_of(ci, h))
    wks = []
    for (ci, h), dmax in zip(units, dmaxs):
        b_all, igc, _, _ = gates[ci]
        b_c = b_all[:, H + h:H + h + 1]
        wks.append(jnp.exp(b_c[L - 1:L, :] - b_c + igc[:, h:h + 1] - dmax[L - 1:L, :]) * k_of(ci, h))
    for (ci, h), wk in zip(units, wks):
        kv_ref[ci, h] = _dot_tn(wk, v_of(ci, h))
        ks_ref[ci, h:h + 1, :] = jnp.sum(wk, axis=0, keepdims=True)
    for ci in range(nch):
        stats = jnp.zeros((L, LANES), F32)
        for h in range(H):
            u = ci * H + h
            stats = jnp.where(lane == h, gates[ci][0][:, H + h:H + h + 1],
                              jnp.where(lane == H + h, dmaxs[u],
                                        jnp.where(lane == 2 * H + h, jnp.sum(Wl[u], axis=-1, keepdims=True), stats)))
        st_ref[rows_of(ci), :] = stats


def _mlstm_scan_body(L, cb, NS, q_ref, hl_ref, st_ref, kv_ref, ks_ref, op_ref, C0_ref, n0_ref, m0_ref, on_ref,
                     o_ref, Cout_ref, nout_ref, mout_ref, C_scr, n_scr, m_scr):
    c = pl.program_id(1)

    @pl.when(c == 0)
    def _():
        C_scr[...] = C0_ref[...]
        n_scr[...] = n0_ref[...]
        m_scr[...] = m0_ref[...]

    onorm = on_ref[...]
    H = ML_HEADS
    units = [(ci, h) for ci in range(cb) for h in range(H)]
    rows_of = lambda ci: slice(ci * L, (ci + 1) * L)
    sts = [st_ref[rows_of(ci), :] for ci in range(cb)]
    pre = {}
    for h in range(H):
        m_prev = m_scr[h:h + 1, 0:1]
        n_prev = n_scr[h:h + 1, :]
        C = C_scr[h]
        for ci in range(cb):
            b_c = sts[ci][:, h:h + 1]
            dmax = sts[ci][:, H + h:H + h + 1]
            mt = jnp.maximum(b_c + m_prev, dmax)
            pre[(ci, h)] = (C, n_prev, m_prev, mt)
            mL = mt[L - 1:L, :]
            fL = jnp.exp(b_c[L - 1:L, :] + m_prev - mL)
            gL = jnp.exp(dmax[L - 1:L, :] - mL)
            C = fL * C + gL * kv_ref[ci, h]
            n_prev = fL * n_prev + gL * ks_ref[ci, h:h + 1, :]
            m_prev = mL
        C_scr[h] = C
        n_scr[h:h + 1, :] = n_prev
        m_scr[h:h + 1, :] = jnp.broadcast_to(m_prev, (1, LANES))

    qs = {u: q_ref[rows_of(u[0]), u[1] * ML_DQK:(u[1] + 1) * ML_DQK].astype(F32) for u in units}
    qC = {u: _dot(qs[u], pre[u][0]) for u in units}
    hs = {}
    for u in units:
        ci, h = u
        _, n_prev, m_prev, mt = pre[u]
        b_c = sts[ci][:, h:h + 1]
        dmax = sts[ci][:, H + h:H + h + 1]
        sw = sts[ci][:, 2 * H + h:2 * H + h + 1]
        e_i = jnp.exp(b_c + m_prev - mt)
        e_l = jnp.exp(dmax - mt)
        num = e_i * qC[u] + e_l * hl_ref[rows_of(ci), h * ML_DV:(h + 1) * ML_DV]
        den = e_i * jnp.sum(qs[u] * n_prev, axis=-1, keepdims=True) + e_l * sw
        hs[u] = num / jnp.maximum(jnp.abs(den), jnp.exp(-mt))
    for u in units:
        ci, h = u
        og = op_ref[rows_of(ci), h * ML_DV:(h + 1) * ML_DV].astype(F32)
        o_ref[rows_of(ci), h * ML_DV:(h + 1) * ML_DV] = (_sigmoid(og) * _rms(hs[u], onorm)).astype(BF16)

    @pl.when(c == NS - 1)
    def _():
        Cout_ref[...] = C_scr[...]
        nout_ref[...] = n_scr[...]
        mout_ref[...] = m_scr[...]


def _mlstm_mixer(main, small, C0, n0, m0, bias, onorm, nch=4, cb=4):
    B, T, _ = main.shape
    L = min(ML_CHUNK, T)
    assert T % L == 0
    NC = T // L
    nch = min(nch, NC)
    cb = min(cb, NC)
    assert NC % nch == 0 and NC % cb == 0
    LA = L * nch
    H = ML_HEADS
    hl, st, kv, ks = pl.pallas_call(
        functools.partial(_mlstm_prep_body, L, nch),
        grid=(B, NC // nch),
        in_specs=[pl.BlockSpec((None, LA, 1024), lambda b, c: (b, c, 0)),
                  pl.BlockSpec((None, LA, 1024), lambda b, c: (b, c, 1)),
                  pl.BlockSpec((None, LA, LANES), lambda b, c: (b, c, 0)),
                  pl.BlockSpec((1, LANES), lambda b, c: (0, 0))],
        out_specs=[pl.BlockSpec((None, LA, 1024), lambda b, c: (b, c, 0)),
                   pl.BlockSpec((None, LA, LANES), lambda b, c: (b, c, 0)),
                   pl.BlockSpec((None, nch, H, ML_DQK, ML_DV), lambda b, c: (b, c, 0, 0, 0)),
                   pl.BlockSpec((None, nch, H, ML_DQK), lambda b, c: (b, c, 0, 0))],
        out_shape=[jax.ShapeDtypeStruct((B, T, 1024), F32),
                   jax.ShapeDtypeStruct((B, T, LANES), F32),
                   jax.ShapeDtypeStruct((B, NC, H, ML_DQK, ML_DV), F32),
                   jax.ShapeDtypeStruct((B, NC, H, ML_DQK), F32)],
        compiler_params=_cparams(("parallel", "parallel")),
        name="mlstm_prep",
    )(main, main, small, bias)
    NS = NC // cb
    m0b = jnp.broadcast_to(m0[:, :, None], (B, H, LANES))
    st_C = pl.BlockSpec((None, H, ML_DQK, ML_DV), lambda b, c: (b, 0, 0, 0))
    st_n = pl.BlockSpec((None, H, ML_DQK), lambda b, c: (b, 0, 0))
    st_m = pl.BlockSpec((None, H, LANES), lambda b, c: (b, 0, 0))
    return pl.pallas_call(
        functools.partial(_mlstm_scan_body, L, cb, NS),
        grid=(B, NS),
        in_specs=[pl.BlockSpec((None, cb * L, H * ML_DQK), lambda b, c: (b, c, 0)),
                  pl.BlockSpec((None, cb * L, 1024), lambda b, c: (b, c, 0)),
                  pl.BlockSpec((None, cb * L, LANES), lambda b, c: (b, c, 0)),
                  pl.BlockSpec((None, cb, H, ML_DQK, ML_DV), lambda b, c: (b, c, 0, 0, 0)),
                  pl.BlockSpec((None, cb, H, ML_DQK), lambda b, c: (b, c, 0, 0)),
                  pl.BlockSpec((None, cb * L, 1024), lambda b, c: (b, c, 2)),
                  st_C, st_n, st_m,
                  pl.BlockSpec((1, ML_DV), lambda b, c: (0, 0))],
        out_specs=[pl.BlockSpec((None, cb * L, 1024), lambda b, c: (b, c, 0)), st_C, st_n, st_m],
        out_shape=[jax.ShapeDtypeStruct((B, T, 1024), BF16),
                   jax.ShapeDtypeStruct((B, H, ML_DQK, ML_DV), F32),
                   jax.ShapeDtypeStruct((B, H, ML_DQK), F32),
                   jax.ShapeDtypeStruct((B, H, LANES), F32)],
        scratch_shapes=[pltpu.VMEM((H, ML_DQK, ML_DV), F32),
                        pltpu.VMEM((H, ML_DQK), F32),
                        pltpu.VMEM((H, LANES), F32)],
        compiler_params=_cparams(("parallel", "arbitrary")),
        name="mlstm_scan",
    )(main, hl, st, kv, ks, main, C0, n0, m0b, onorm.reshape(1, ML_DV))


FOX_BIAS_LANE = FOX_DH


def _fox_prep_body(q_ref, k_ref, v_ref, small_ref, bf_ref, qn_ref, kn_ref,
                   qh_ref, k32_ref, v32_ref, lf_ref):
    r = lax.broadcasted_iota(jnp.int32, (LANES, LANES), 0) // FOX_DH
    cc = lax.broadcasted_iota(jnp.int32, (LANES, LANES), 1) // FOX_DH
    avg = jnp.where(r == cc, 1.0 / FOX_DH, 0.0).astype(BF16)

    def head_rms(x, gain):
        parts = []
        for t in range(x.shape[1] // LANES):
            xt = x[:, t * LANES:(t + 1) * LANES]
            hi, lo = _split2(xt * xt)
            ms = (jnp.dot(hi, avg, preferred_element_type=F32)
                  + jnp.dot(lo, avg, preferred_element_type=F32))
            parts.append(xt * lax.rsqrt(ms + EPS))
        return jnp.concatenate(parts, axis=-1) * gain

    qn = head_rms(q_ref[...].astype(F32), qn_ref[...]) * (LOG2E * FOX_DH ** -0.5)
    lane = lax.broadcasted_iota(jnp.int32, (1, LANES), 1)
    ones3 = jnp.where(lane < FOX_BIAS_LANE + 3, 1.0, 0.0)
    for h in range(FOX_HEADS):
        xt = qn[:, (h // 2) * LANES:(h // 2 + 1) * LANES]
        if h % 2:
            xt = pltpu.roll(xt, FOX_DH, axis=1)
        qh_ref[h] = jnp.where(lane < FOX_DH, xt, ones3).astype(BF16)
    k32_ref[...] = head_rms(k_ref[...].astype(F32), kn_ref[...])
    v32_ref[...] = v_ref[...].astype(F32)
    lf_ref[...] = jax.nn.log_sigmoid(small_ref[...] + bf_ref[...])


def _fox_prep(main, small, bf_row, qn_row, kn_row, tm=512):
    B, T, _ = main.shape
    tm = min(tm, T)
    assert T % tm == 0
    blk = lambda j: pl.BlockSpec((None, tm, 1024), lambda b, i: (b, i, j))
    row = lambda n: pl.BlockSpec((1, n), lambda b, i: (0, 0))
    o1024 = pl.BlockSpec((None, tm, 1024), lambda b, i: (b, i, 0))
    return pl.pallas_call(
        _fox_prep_body,
        grid=(B, T // tm),
        in_specs=[blk(0), blk(1), blk(2),
                  pl.BlockSpec((None, tm, LANES), lambda b, i: (b, i, 0)),
                  row(LANES), row(1024), row(1024)],
        out_specs=[pl.BlockSpec((None, FOX_HEADS, tm, LANES), lambda b, i: (b, 0, i, 0)),
                   o1024, o1024,
                   pl.BlockSpec((None, tm, LANES), lambda b, i: (b, i, 0))],
        out_shape=[jax.ShapeDtypeStruct((B, FOX_HEADS, T, LANES), BF16),
                   jax.ShapeDtypeStruct((B, T, 1024), F32),
                   jax.ShapeDtypeStruct((B, T, 1024), F32),
                   jax.ShapeDtypeStruct((B, T, LANES), F32)],
        compiler_params=_cparams(("parallel", "parallel")),
        name="fox_prep",
    )(main, main, main, small, bf_row, qn_row, kn_row)


def _fox_kv_body(tm, k_ref, v_ref, lf_ref, kh_ref, vh_ref, carry):
    @pl.when(pl.program_id(1) == 0)
    def _():
        carry[...] = jnp.zeros_like(carry)

    incl, _, _ = _tri_masks(tm)
    tril_b = jnp.where(incl, 1.0, 0.0).astype(BF16)
    F = _cumsum_rows(tril_b, lf_ref[...]) + carry[0:1, :]
    carry[0:1, :] = F[tm - 1:tm, :]
    nb = -LOG2E * F
    b1 = nb.astype(BF16).astype(F32)
    r1 = nb - b1
    b2 = r1.astype(BF16).astype(F32)
    b3 = r1 - b2
    lane = lax.broadcasted_iota(jnp.int32, (1, LANES), 1)
    ones1 = jnp.where(lane == FOX_BIAS_LANE, 1.0, 0.0)
    k = k_ref[...]
    v = v_ref[...]
    for h in range(FOX_HEADS):
        kt = k[:, (h // 2) * LANES:(h // 2 + 1) * LANES]
        vt = v[:, (h // 2) * LANES:(h // 2 + 1) * LANES]
        if h % 2:
            kt = pltpu.roll(kt, FOX_DH, axis=1)
            vt = pltpu.roll(vt, FOX_DH, axis=1)
        bias = jnp.where(lane == FOX_BIAS_LANE, b1[:, h:h + 1],
                         jnp.where(lane == FOX_BIAS_LANE + 1, b2[:, h:h + 1],
                                   jnp.where(lane == FOX_BIAS_LANE + 2, b3[:, h:h + 1], 0.0)))
        kh_ref[h] = jnp.where(lane < FOX_DH, kt, bias).astype(BF16)
        vh_ref[h] = jnp.where(lane < FOX_DH, vt, ones1).astype(BF16)


def _fox_kv(k_all, v_all, lf_all):
    B, N, _ = k_all.shape
    tm = next(t for t in (512, 384, 256, 128) if N % t == 0)
    hm = pl.BlockSpec((None, FOX_HEADS, tm, LANES), lambda b, i: (b, 0, i, 0))
    return pl.pallas_call(
        functools.partial(_fox_kv_body, tm),
        grid=(B, N // tm),
        in_specs=[pl.BlockSpec((None, tm, 1024), lambda b, i: (b, i, 0)),
                  pl.BlockSpec((None, tm, 1024), lambda b, i: (b, i, 0)),
                  pl.BlockSpec((None, tm, LANES), lambda b, i: (b, i, 0))],
        out_specs=[hm, hm],
        out_shape=[jax.ShapeDtypeStruct((B, FOX_HEADS, N, LANES), BF16),
                   jax.ShapeDtypeStruct((B, FOX_HEADS, N, LANES), BF16)],
        scratch_shapes=[pltpu.VMEM((8, LANES), F32)],
        compiler_params=_cparams(("parallel", "arbitrary")),
        name="fox_kv",
    )(k_all, v_all, lf_all)


def _flash_body(P, tq, tk, rq, qi_tab, kj_tab, q_ref, k_ref, v_ref, gate_ref, o_ref, m_scr, acc_scr):
    s_idx = pl.program_id(2)
    qi = qi_tab[s_idx]
    kj = kj_tab[s_idx]
    last_kj = (P + (qi + 1) * tq - 1) // tk
    aligned = tq == tk and P % tk == 0

    @pl.when(kj == 0)
    def _():
        m_scr[...] = jnp.full(m_scr.shape, NEG_BIG, F32)
        acc_scr[...] = jnp.zeros_like(acc_scr)

    def step(masked):
        units = [(hh, r) for r in range(tq // rq) for hh in range(2)]

        def logits(u):
            hh, r = u
            rows = slice(r * rq, (r + 1) * rq)
            nk = min(tk, (r + 1) * rq) if (masked and aligned) else tk
            s = lax.dot_general(q_ref[hh, rows, :], k_ref[hh, 0:nk, :], (((1,), (1,)), ((), ())),
                                preferred_element_type=F32)
            if masked:
                qpos = P + qi * tq + r * rq + lax.broadcasted_iota(jnp.int32, (rq, nk), 0)
                kpos = kj * tk + lax.broadcasted_iota(jnp.int32, (rq, nk), 1)
                s = jnp.where(kpos <= qpos, s, NEG_BIG)
            return s, nk

        def probs(u, s):
            hh, r = u
            rows = slice(r * rq, (r + 1) * rq)
            m_prev = m_scr[hh, rows, :][:, 0:1]
            m_new = jnp.maximum(m_prev, jnp.max(s, axis=-1, keepdims=True))
            m_scr[hh, rows, :] = jnp.broadcast_to(m_new, (rq, LANES))
            return jnp.exp2(m_prev - m_new), jnp.exp2(s - m_new).astype(BF16)

        def accumulate(u, alpha, p, nk):
            hh, r = u
            rows = slice(r * rq, (r + 1) * rq)
            acc_scr[hh, rows, :] = alpha * acc_scr[hh, rows, :] + jnp.dot(
                p, v_ref[hh, 0:nk, :], preferred_element_type=F32)

        n = len(units)
        pend_s = {0: logits(units[0])}
        pend_p = {}
        for i in range(n):
            if i + 1 < n:
                pend_s[i + 1] = logits(units[i + 1])
            s, nk = pend_s.pop(i)
            pend_p[i] = probs(units[i], s) + (nk,)
            if i >= 1:
                accumulate(units[i - 1], *pend_p.pop(i - 1))
        accumulate(units[n - 1], *pend_p.pop(n - 1))

    needs_mask = (kj + 1) * tk - 1 > P + qi * tq

    @pl.when(needs_mask)
    def _():
        step(True)

    @pl.when(jnp.logical_not(needs_mask))
    def _():
        step(False)

    @pl.when(kj == last_kj)
    def _():
        lane = lax.broadcasted_iota(jnp.int32, (1, LANES), 1)
        a0 = acc_scr[0]
        a1 = acc_scr[1]
        o0 = a0 * (1.0 / a0[:, FOX_BIAS_LANE:FOX_BIAS_LANE + 1])
        o1 = a1 * (1.0 / a1[:, FOX_BIAS_LANE:FOX_BIAS_LANE + 1])
        o = jnp.where(lane < FOX_DH, o0, pltpu.roll(o1, FOX_DH, axis=1))
        g = gate_ref[...].astype(F32)
        o_ref[...] = (o * _sigmoid(g)).astype(BF16)


def _fox_flash(qh, kh, vh, gate_src, P, tq, tk, rq):
    B, _, T, _ = qh.shape
    N = kh.shape[2]
    assert T % tq == 0 and N % tk == 0 and tq % rq == 0
    nq = T // tq
    pairs = [(qi, kj) for qi in range(nq) for kj in range((P + (qi + 1) * tq - 1) // tk + 1)]
    qi_tab = jnp.asarray(np.array([p[0] for p in pairs], np.int32))
    kj_tab = jnp.asarray(np.array([p[1] for p in pairs], np.int32))
    HP = FOX_HEADS // 2
    gate_blk0 = 3 * (1024 // LANES)
    grid_spec = pltpu.PrefetchScalarGridSpec(
        num_scalar_prefetch=2,
        grid=(B, HP, len(pairs)),
        in_specs=[pl.BlockSpec((None, 2, tq, LANES), lambda b, p, s, qt, kt: (b, p, qt[s], 0)),
                  pl.BlockSpec((None, 2, tk, LANES), lambda b, p, s, qt, kt: (b, p, kt[s], 0)),
                  pl.BlockSpec((None, 2, tk, LANES), lambda b, p, s, qt, kt: (b, p, kt[s], 0)),
                  pl.BlockSpec((None, tq, LANES), lambda b, p, s, qt, kt: (b, qt[s], gate_blk0 + p))],
        out_specs=pl.BlockSpec((None, tq, LANES), lambda b, p, s, qt, kt: (b, qt[s], p)),
        scratch_shapes=[pltpu.VMEM((2, tq, LANES), F32),
                        pltpu.VMEM((2, tq, LANES), F32)],
    )
    return pl.pallas_call(
        functools.partial(_flash_body, P, tq, tk, rq),
        grid_spec=grid_spec,
        out_shape=jax.ShapeDtypeStruct((B, T, 1024), BF16),
        compiler_params=_cparams(("parallel", "parallel", "arbitrary")),
        name="fox_flash",
    )(qi_tab, kj_tab, qh, kh, vh, gate_src)


def _pad_lanes(a, offset=0):
    n = a.shape[-1]
    pads = [(0, 0)] * (a.ndim - 1) + [(offset, LANES - n - offset)]
    return jnp.pad(a, pads)


def _trunk(x, gdn_conv, gdn_S, ml_C, ml_n, ml_m, fox_k, fox_v, fox_lf, mem_k, mem_v, W):
    B, T, D = x.shape
    M = B * T
    x2 = x.reshape(M, D)
    conv_o, S_o, C_o, n_o, m_o, k_o, v_o, lf_o = [], [], [], [], [], [], [], []
    for layer in range(DEPTH):
        kind, j = layer % N_MIXERS, layer // N_MIXERS
        gains = W["norm_gains"][layer]
        if kind == 0:
            main, small = _prenorm_mm(x2, gains[0], W["gdn_w_in"], j, GDN_MAIN, W["gdn_w_small"], out_dtype=F32)
            o, buf, S = _gdn_mixer(main.reshape(B, T, GDN_MAIN), small.reshape(B, T, LANES),
                                   gdn_conv[j], gdn_S[j], W["gdn_wc_t"][j], W["gdn_par"][j],
                                   W["gdn_onorm"][j])
            conv_o.append(buf)
            S_o.append(S)
            w_out = W["gdn_w_out"]
        elif kind == 1:
            main, small = _prenorm_mm(x2, gains[0], W["ml_w_in"], j, ML_MAIN, W["ml_w_small"])
            o, C, n, m = _mlstm_mixer(main.reshape(B, T, ML_MAIN), small.reshape(B, T, LANES),
                                      ml_C[j], ml_n[j], ml_m[j], W["ml_bias"][j], W["ml_onorm"][j])
            C_o.append(C)
            n_o.append(n)
            m_o.append(m[:, :, 0])
            w_out = W["ml_w_out"]
        else:
            main, small = _prenorm_mm(x2, gains[0], W["fox_w_in"], j, FOX_MAIN, W["fox_w_small"])
            main3 = main.reshape(B, T, FOX_MAIN)
            qh, k32, v32, lfp = _fox_prep(main3, small.reshape(B, T, LANES), W["fox_bf"][j],
                                          W["fox_qn"][j], W["fox_kn"][j])
            P = fox_k[j].shape[1]
            if P == 0:
                k_all, v_all, lf_all, tq, tk, rq = k32, v32, lfp, min(1024, T), min(1024, T), 256
            else:
                npad = -(P + T) % LANES
                zkv = jnp.zeros((B, npad, 1024), F32)
                k_all = jnp.concatenate([fox_k[j].reshape(B, P, 1024), k32, zkv], axis=1)
                v_all = jnp.concatenate([fox_v[j].reshape(B, P, 1024), v32, zkv], axis=1)
                lf_all = jnp.concatenate([_pad_lanes(fox_lf[j]), lfp, jnp.zeros((B, npad, LANES), F32)], axis=1)
                tq, tk, rq = T, P + T + npad, T
            kh, vh = _fox_kv(k_all, v_all, lf_all)
            o = _fox_flash(qh, kh, vh, main3, P, tq, tk, rq)
            k_o.append(k32.reshape(B, T, FOX_HEADS, FOX_DH))
            v_o.append(v32.reshape(B, T, FOX_HEADS, FOX_DH))
            lf_o.append(lfp[:, :, :FOX_HEADS])
            w_out = W["fox_w_out"]
        x2 = _memattn(o, w_out, j, gains[1], x2.reshape(B, T, D), gains[2], W["x_w_q"],
                      mem_k[layer], mem_v[layer], W["x_w_o"], layer, gains[3]).reshape(M, D)
        x2 = _ffn(x2, gains[4], W["ffn_w_gu"], W["ffn_w_down"], layer, gains[5])
    stack = lambda xs: xs[0][None] if len(xs) == 1 else jnp.stack(xs)
    return (x2.reshape(B, T, D), stack(conv_o), stack(S_o), stack(C_o), stack(n_o),
            stack(m_o), stack(k_o), stack(v_o), stack(lf_o))


def kernel(x_prompt, x_sample, state_gdn_conv, state_gdn_S, state_mlstm_C, state_mlstm_n, state_mlstm_m, cache_fox_k, cache_fox_v, cache_fox_logf, cache_mem_k, cache_mem_v, mem_prompt, norm_gains, gdn_w_in, gdn_w_conv, gdn_a_log, gdn_dt_bias, gdn_onorm, gdn_w_out, ml_w_in, ml_b_i, ml_b_f, ml_onorm, ml_w_out, fox_w_in, fox_b_f, fox_qk_norm, fox_w_out, x_w_q, x_w_kv, x_w_o, ffn_w_gu, ffn_w_down):
    B = x_prompt.shape[0]
    DB = x_sample.shape[0]
    n_a, n_b, n_c = gdn_w_in.shape[0], ml_w_in.shape[0], fox_w_in.shape[0]
    W = {
        "norm_gains": norm_gains,
        "gdn_w_in": gdn_w_in.astype(BF16),
        "gdn_w_small": _pad_lanes(gdn_w_in[:, :, GDN_MAIN:]).astype(BF16),
        "gdn_wc_t": jnp.transpose(gdn_w_conv, (0, 2, 1)),
        "gdn_par": jnp.stack([_pad_lanes(gdn_a_log, GDN_HEADS), _pad_lanes(gdn_dt_bias, GDN_HEADS)], axis=1),
        "gdn_onorm": gdn_onorm,
        "gdn_w_out": gdn_w_out.astype(BF16),
        "ml_w_in": ml_w_in.astype(BF16),
        "ml_w_small": _pad_lanes(ml_w_in[:, :, ML_MAIN:]).astype(BF16),
        "ml_bias": _pad_lanes(jnp.concatenate([ml_b_i, ml_b_f], axis=-1))[:, None, :],
        "ml_onorm": ml_onorm,
        "ml_w_out": ml_w_out.astype(BF16),
        "fox_w_in": fox_w_in.astype(BF16),
        "fox_w_small": _pad_lanes(fox_w_in[:, :, FOX_MAIN:]).astype(BF16),
        "fox_bf": _pad_lanes(fox_b_f)[:, None, :],
        "fox_qn": jnp.tile(fox_qk_norm[:, 0, :], (1, FOX_HEADS))[:, None, :],
        "fox_kn": jnp.tile(fox_qk_norm[:, 1, :], (1, FOX_HEADS))[:, None, :],
        "fox_w_out": fox_w_out.astype(BF16),
        "x_w_q": x_w_q.astype(BF16),
        "x_w_o": x_w_o.astype(BF16),
        "ffn_w_gu": ffn_w_gu.astype(BF16),
        "ffn_w_down": ffn_w_down.astype(BF16),
    }

    mem2 = mem_prompt.reshape(B * N_MEM, D_MODEL)
    x_w_kv_b = x_w_kv.astype(BF16)
    kv = [_prenorm_mm(mem2, norm_gains[l, 6], x_w_kv_b, l, 2 * D_MODEL, out_dtype=F32).reshape(B, N_MEM, 2 * D_MODEL)
          for l in range(DEPTH)]
    p_mem_k = jnp.stack([a[:, :, :D_MODEL].reshape(B, N_MEM, X_HEADS, X_DH) for a in kv])
    p_mem_v = jnp.stack([a[:, :, D_MODEL:].reshape(B, N_MEM, X_HEADS, X_DH) for a in kv])

    cmk = cache_mem_k.reshape(DEPTH * DB, N_MEM, D_MODEL)
    cmv = cache_mem_v.reshape(DEPTH * DB, N_MEM, D_MODEL)
    zeros = lambda *s: jnp.zeros(s, F32)
    prompt = _trunk(
        x_prompt,
        zeros(n_a, B, CONV_W - 1, GDN_QKV), zeros(n_a, B, GDN_HEADS, GDN_DK, GDN_DV),
        zeros(n_b, B, ML_HEADS, ML_DQK, ML_DV), zeros(n_b, B, ML_HEADS, ML_DQK), zeros(n_b, B, ML_HEADS),
        zeros(n_c, B, 0, FOX_HEADS, FOX_DH), zeros(n_c, B, 0, FOX_HEADS, FOX_DH), zeros(n_c, B, 0, FOX_HEADS),
        [(a, 0, 0) for a in kv], [(a, 0, 1) for a in kv], W)
    sample = _trunk(
        x_sample, state_gdn_conv, state_gdn_S, state_mlstm_C, state_mlstm_n, state_mlstm_m,
        cache_fox_k, cache_fox_v, cache_fox_logf,
        [(cmk, l * DB, 0) for l in range(DEPTH)], [(cmv, l * DB, 0) for l in range(DEPTH)], W)
    (y_p, p_conv, p_S, p_C, p_n, p_m, p_k, p_v, p_lf) = prompt
    (y_s, s_conv, s_S, s_C, s_n, s_m, s_k, s_v, s_lf) = sample
    return (y_p, y_s, p_conv, p_S, p_C, p_n, p_m, p_k, p_v, p_lf, p_mem_k, p_mem_v,
            s_conv, s_S, s_C, s_n, s_m, s_k, s_v, s_lf)
```

```python
import functools

import numpy as np
import jax
import jax.numpy as jnp
from jax import lax
from jax.experimental import pallas as pl
from jax.experimental.pallas import tpu as pltpu

F32 = jnp.float32
BF16 = jnp.bfloat16

D_MODEL = 1024
DEPTH = 4
N_MIXERS = 3
EPS = 1e-6
CONV_W = 4

GDN_HEADS = 8
GDN_DK = 128
GDN_DV = 128
GDN_QKV = 3072
GDN_MAIN = 4096

ML_HEADS = 4
ML_DQK = 128
ML_DV = 256
ML_MAIN = 3072
GATE_CAP = 15.0

FOX_HEADS = 16
FOX_DH = 64
FOX_MAIN = 4096

N_MEM = 256
X_HEADS = 4
X_DH = 256
D_FF = 2816

LANES = 128
NEG_BIG = -1e30
LOG2E = 1.4426950408889634

VMEM_LIMIT = 48 * 1024 * 1024


def _wspec(li, blk, imap):
    return pl.BlockSpec((None,) + blk, lambda *g: (li,) + imap(*g))


def _cparams(sem):
    return pltpu.CompilerParams(dimension_semantics=sem, vmem_limit_bytes=VMEM_LIMIT)


def _rms(x, g):
    return x * lax.rsqrt(jnp.mean(x * x, axis=-1, keepdims=True) + EPS) * g


def _sigmoid(x):
    return 1.0 / (1.0 + jnp.exp(-x))


def _dot(a, b):
    return jnp.dot(a.astype(BF16), b.astype(BF16), preferred_element_type=F32)


def _dot_nt(a, b):
    return lax.dot_general(a.astype(BF16), b.astype(BF16), (((1,), (1,)), ((), ())),
                           preferred_element_type=F32)


def _dot_tn(a, b):
    return lax.dot_general(a.astype(BF16), b.astype(BF16), (((0,), (0,)), ((), ())),
                           preferred_element_type=F32)


def _split2(x):
    hi = x.astype(BF16)
    lo = (x - hi.astype(F32)).astype(BF16)
    return hi, lo


def _dot_x3(a, b):
    ah, al = _split2(a)
    bh, bl = _split2(b)
    d = lambda u, v: jnp.dot(u, v, preferred_element_type=F32)
    return d(ah, bh) + (d(ah, bl) + d(al, bh))


def _cumsum_rows(tril_b, x):
    h = x.astype(BF16)
    r = x - h.astype(F32)
    m = r.astype(BF16)
    l = (r - m.astype(F32)).astype(BF16)
    d = lambda v: jnp.dot(tril_b, v, preferred_element_type=F32)
    return d(h) + (d(m) + d(l))


def _tri_masks(L):
    row = lax.broadcasted_iota(jnp.int32, (L, L), 0)
    col = lax.broadcasted_iota(jnp.int32, (L, L), 1)
    return row >= col, row > col, row == col


def _prenorm_mm_body(has_small, *refs):
    if has_small:
        x_ref, g_ref, w_ref, ws_ref, o_ref, os_ref, xn_ref = refs
    else:
        x_ref, g_ref, w_ref, o_ref, xn_ref = refs

    @pl.when(pl.program_id(1) == 0)
    def _():
        xb = _rms(x_ref[...], g_ref[...]).astype(BF16)
        xn_ref[...] = xb
        if has_small:
            os_ref[...] = jnp.dot(xb, ws_ref[...], preferred_element_type=F32)

    o_ref[...] = jnp.dot(xn_ref[...], w_ref[...], preferred_element_type=F32).astype(o_ref.dtype)


def _prenorm_mm(x, gain, w_main, li, n_out, w_small=None, out_dtype=BF16, tm=1024, tn=1024):
    M, D = x.shape
    N = n_out
    tm = min(tm, M)
    tn = min(tn, N)
    assert M % tm == 0 and N % tn == 0
    has_small = w_small is not None
    in_specs = [pl.BlockSpec((tm, D), lambda i, j: (i, 0)),
                pl.BlockSpec((1, D), lambda i, j: (0, 0)),
                _wspec(li, (D, tn), lambda i, j: (0, j))]
    out_specs = [pl.BlockSpec((tm, tn), lambda i, j: (i, j))]
    out_shape = [jax.ShapeDtypeStruct((M, N), out_dtype)]
    args = [x, gain.reshape(1, D), w_main]
    if has_small:
        in_specs.append(_wspec(li, (D, LANES), lambda i, j: (0, 0)))
        out_specs.append(pl.BlockSpec((tm, LANES), lambda i, j: (i, 0)))
        out_shape.append(jax.ShapeDtypeStruct((M, LANES), F32))
        args.append(w_small)
    res = pl.pallas_call(
        functools.partial(_prenorm_mm_body, has_small),
        grid=(M // tm, N // tn),
        in_specs=in_specs, out_specs=out_specs, out_shape=out_shape,
        scratch_shapes=[pltpu.VMEM((tm, D), BF16)],
        compiler_params=_cparams(("parallel", "arbitrary")),
        name="prenorm_mm",
    )(*args)
    return res if has_small else res[0]


def _memattn_body(nb, o_ref, wout_ref, g1_ref, x_ref, g2_ref, wq_ref, mk_ref, mv_ref, wo_ref, g3_ref, y_ref):
    tq, D = x_ref.shape[1], x_ref.shape[2]
    R = nb * tq
    o_in = o_ref[...].reshape(R, o_ref.shape[2])
    x = x_ref[...].reshape(R, D)
    x = x + _rms(jnp.dot(o_in, wout_ref[...], preferred_element_type=F32), g1_ref[...])
    h = _rms(x, g2_ref[...]).astype(BF16)
    q = jnp.dot(h, wq_ref[...], preferred_element_type=F32).astype(BF16)
    rows = []
    for bi in range(nb):
        mk = mk_ref[bi].astype(BF16)
        mv = mv_ref[bi].astype(BF16)
        qb = q[bi * tq:(bi + 1) * tq]
        outs = []
        for hd in range(X_HEADS):
            sl = slice(hd * X_DH, (hd + 1) * X_DH)
            s = _dot_nt(qb[:, sl], mk[:, sl]) * (X_DH ** -0.5)
            p = jnp.exp(s - jnp.max(s, axis=-1, keepdims=True))
            p = p * (1.0 / jnp.sum(p, axis=-1, keepdims=True))
            outs.append(jnp.dot(p.astype(BF16), mv[:, sl], preferred_element_type=F32))
        rows.append(jnp.concatenate(outs, axis=-1).astype(BF16))
    o = rows[0] if nb == 1 else jnp.concatenate(rows, axis=0)
    y = jnp.dot(o, wo_ref[...], preferred_element_type=F32)
    y_ref[...] = (x + _rms(y, g3_ref[...])).reshape(nb, tq, D)


def _memattn(o, w_out, lo, g1, x, g2, wq, mem_k, mem_v, wo, lx, g3, tq=512, rows_per_step=256):
    B, T, D = x.shape
    tq = min(tq, T)
    assert T % tq == 0
    nb = max(1, min(B, rows_per_step // tq)) if tq == T else 1
    assert B % nb == 0
    HD = X_HEADS * X_DH
    K = o.shape[-1]
    (mk_arr, koff, kcol), (mv_arr, voff, vcol) = mem_k, mem_v
    assert koff % nb == 0 and voff % nb == 0
    return pl.pallas_call(
        functools.partial(_memattn_body, nb),
        grid=(B // nb, T // tq),
        in_specs=[pl.BlockSpec((nb, tq, K), lambda b, i: (b, i, 0)),
                  _wspec(lo, (K, D), lambda b, i: (0, 0)),
                  pl.BlockSpec((1, D), lambda b, i: (0, 0)),
                  pl.BlockSpec((nb, tq, D), lambda b, i: (b, i, 0)),
                  pl.BlockSpec((1, D), lambda b, i: (0, 0)),
                  _wspec(lx, (D, HD), lambda b, i: (0, 0)),
                  pl.BlockSpec((nb, N_MEM, HD), lambda b, i: (koff // nb + b, 0, kcol)),
                  pl.BlockSpec((nb, N_MEM, HD), lambda b, i: (voff // nb + b, 0, vcol)),
                  _wspec(lx, (HD, D), lambda b, i: (0, 0)),
                  pl.BlockSpec((1, D), lambda b, i: (0, 0))],
        out_specs=pl.BlockSpec((nb, tq, D), lambda b, i: (b, i, 0)),
        out_shape=jax.ShapeDtypeStruct((B, T, D), F32),
        compiler_params=_cparams(("parallel", "parallel")),
        name="memattn",
    )(o, w_out, g1.reshape(1, D), x, g2.reshape(1, D), wq, mk_arr, mv_arr, wo, g3.reshape(1, D))


def _ffn_body(nf, x_ref, g4_ref, wg_ref, wu_ref, wd_ref, g5_ref, y_ref, xn_ref, acc_ref):
    f = pl.program_id(1)

    @pl.when(f == 0)
    def _():
        xn_ref[...] = _rms(x_ref[...], g4_ref[...]).astype(BF16)
        acc_ref[...] = jnp.zeros_like(acc_ref)

    xn = xn_ref[...]
    g = jnp.dot(xn, wg_ref[...], preferred_element_type=F32)
    u = jnp.dot(xn, wu_ref[...], preferred_element_type=F32)
    a = (g * _sigmoid(g)) * u
    acc_ref[...] += jnp.dot(a.astype(BF16), wd_ref[...], preferred_element_type=F32)

    @pl.when(f == nf - 1)
    def _():
        y_ref[...] = x_ref[...] + _rms(acc_ref[...], g5_ref[...])


def _ffn(x, g4, w_gu, w_down, li, g5, tm=1024, tf=256):
    M, D = x.shape
    tm = min(tm, M)
    assert M % tm == 0 and D_FF % tf == 0
    nf = D_FF // tf
    return pl.pallas_call(
        functools.partial(_ffn_body, nf),
        grid=(M // tm, nf),
        in_specs=[pl.BlockSpec((tm, D), lambda i, f: (i, 0)),
                  pl.BlockSpec((1, D), lambda i, f: (0, 0)),
                  _wspec(li, (D, tf), lambda i, f: (0, f)),
                  _wspec(li, (D, tf), lambda i, f: (0, nf + f)),
                  _wspec(li, (tf, D), lambda i, f: (f, 0)),
                  pl.BlockSpec((1, D), lambda i, f: (0, 0))],
        out_specs=pl.BlockSpec((tm, D), lambda i, f: (i, 0)),
        out_shape=jax.ShapeDtypeStruct((M, D), F32),
        scratch_shapes=[pltpu.VMEM((tm, D), BF16), pltpu.VMEM((tm, D), F32)],
        compiler_params=_cparams(("parallel", "arbitrary")),
        name="ffn",
    )(x, g4.reshape(1, D), w_gu, w_gu, w_down, g5.reshape(1, D))


GDN_CHUNK = 64
GDN_BASE = 8


def _inverse_masks(L):
    row = lax.broadcasted_iota(jnp.int32, (L, L), 0)
    col = lax.broadcasted_iota(jnp.int32, (L, L), 1)
    base = row // GDN_BASE == col // GDN_BASE
    offs = []
    s = GDN_BASE
    while s < L:
        same_pair = row // (2 * s) == col // (2 * s)
        offs.append(jnp.logical_and(same_pair, (row // s) - (col // s) == 1))
        s *= 2
    return base, row == col, offs


def _unit_lower_inverse(As, masks, mm):
    base, diag, offs = masks
    eye = jnp.where(diag, 1.0, 0.0)
    Ps = [jnp.where(base, -A, 0.0) for A in As]
    Ts = [eye + P for P in Ps]
    k = 1
    while k < GDN_BASE // 2:
        Ps = [mm(P, P) for P in Ps]
        Ts = [T + mm(T, P) for T, P in zip(Ts, Ps)]
        k *= 2
    for off in offs:
        Ms = [_dot(T, jnp.where(off, A, 0.0)) for T, A in zip(Ts, As)]
        Ts = [T - _dot(M, T) for T, M in zip(Ts, Ms)]
    return Ts


def _gdn_prep_body(L, nch, qkv_ref, prev_ref, small_ref, buf_ref, wc_ref, par_ref,
                   wk_ref, uv_ref, qt_ref, kt_ref, at_ref, egl_ref, xbuf):
    c = pl.program_id(1)
    LA = L * nch

    @pl.when(c == 0)
    def _():
        xbuf[0:8, :] = jnp.zeros((8, GDN_QKV), F32)
        xbuf[5:8, :] = buf_ref[...]

    @pl.when(c > 0)
    def _():
        xbuf[0:8, :] = prev_ref[8:16, :].astype(F32)

    xbuf[8:8 + LA, :] = qkv_ref[...].astype(F32)
    y = xbuf[5:5 + LA, :] * wc_ref[0:1, :]
    for j in range(1, CONV_W):
        y = y + xbuf[5 + j:5 + j + LA, :] * wc_ref[j:j + 1, :]
    y = y * _sigmoid(y)

    small = small_ref[...]
    beta_all = _sigmoid(small)
    g_all = -jnp.exp(par_ref[0:1, :]) * jax.nn.softplus(small + par_ref[1:2, :])
    incl, strict, _ = _tri_masks(L)
    tril_b = jnp.where(incl, 1.0, 0.0).astype(BF16)
    masks = _inverse_masks(L)
    H = GDN_HEADS

    units = []
    for ci in range(nch):
        rows = slice(ci * L, (ci + 1) * L)
        G_all = _cumsum_rows(tril_b, g_all[rows])
        GT = G_all.T
        egl_ref[ci] = jnp.broadcast_to(jnp.exp(G_all[L - 1:L, :]), (8, LANES))
        for h in range(H):
            qh = y[rows, h * GDN_DK:(h + 1) * GDN_DK]
            kh = y[rows, (H + h) * GDN_DK:(H + h + 1) * GDN_DK]
            vh = y[rows, 2 * H * GDN_DK + h * GDN_DV:2 * H * GDN_DK + (h + 1) * GDN_DV]
            qh = qh * lax.rsqrt(jnp.sum(qh * qh, axis=-1, keepdims=True) + EPS) * (GDN_DK ** -0.5)
            kh = kh * lax.rsqrt(jnp.sum(kh * kh, axis=-1, keepdims=True) + EPS)
            bc = beta_all[rows, h:h + 1]
            Gc = G_all[:, H + h:H + h + 1]
            Gr = GT[H + h:H + h + 1, :]
            eG = jnp.exp(Gc)
            dec = jnp.exp(jnp.where(incl, Gc - Gr, NEG_BIG))
            qkk = _dot_nt(jnp.concatenate([qh, kh], axis=0), kh)
            A = jnp.where(strict, bc * qkk[L:] * dec, 0.0)
            rhs = jnp.concatenate([bc * vh, (bc * eG) * kh], axis=-1)
            cols = slice(h * GDN_DV, (h + 1) * GDN_DV)
            qt_ref[rows, cols] = (eG * qh).astype(BF16)
            kt_ref[rows, cols] = (kh * jnp.exp(Gc[L - 1:L, :] - Gc)).astype(BF16)
            at_ref[rows, cols] = jnp.concatenate(
                [qkk[:L] * dec, jnp.zeros((L, GDN_DV - L), F32)], axis=-1).astype(BF16)
            units.append((rows, cols, A, rhs))

    Ts = _unit_lower_inverse([u[2] for u in units], masks, _dot)
    for (rows, cols, _, rhs), T in zip(units, Ts):
        sol = _dot(T, rhs)
        uv_ref[rows, cols] = sol[:, :GDN_DV]
        wk_ref[rows, cols] = sol[:, GDN_DV:].astype(BF16)


def _gdn_scan_body(L, cb, NS, wk_ref, uv_ref, qt_ref, kt_ref, at_ref, egl_ref, gate_ref, S0_ref, on_ref,
                   o_ref, Sout_ref, S_scr):
    c = pl.program_id(1)

    @pl.when(c == 0)
    def _():
        S_scr[...] = S0_ref[...]

    onorm = on_ref[...]
    H = GDN_HEADS
    for ci in range(cb):
        rows = slice(ci * L, (ci + 1) * L)
        egl = egl_ref[ci]
        hs = [slice(h * GDN_DV, (h + 1) * GDN_DV) for h in range(H)]
        Ss = [S_scr[h] for h in range(H)]
        Xs = [jnp.dot(jnp.concatenate([wk_ref[rows, hs[h]], qt_ref[rows, hs[h]]], axis=0),
                      Ss[h].astype(BF16), preferred_element_type=F32) for h in range(H)]
        Us = [(uv_ref[rows, hs[h]] - Xs[h][:L]).astype(BF16) for h in range(H)]
        os = [Xs[h][L:] + jnp.dot(at_ref[rows, h * GDN_DV:h * GDN_DV + L], Us[h],
                                  preferred_element_type=F32) for h in range(H)]
        for h in range(H):
            S_scr[h] = egl[0:1, H + h:H + h + 1] * Ss[h] + lax.dot_general(
                kt_ref[rows, hs[h]], Us[h], (((0,), (0,)), ((), ())), preferred_element_type=F32)
        for h in range(H):
            gh = gate_ref[rows, hs[h]].astype(F32)
            o_ref[rows, hs[h]] = (_rms(os[h], onorm) * (gh * _sigmoid(gh))).astype(BF16)

    @pl.when(c == NS - 1)
    def _():
        Sout_ref[...] = S_scr[...]


def _gdn_mixer(main, small, conv_buf, S0, wc_t, par, onorm, nch=4, cb=8):
    B, T, _ = main.shape
    L = min(GDN_CHUNK, T)
    assert T % L == 0
    NC = T // L
    nch = min(nch, NC)
    cb = min(cb, NC)
    assert NC % nch == 0 and NC % cb == 0
    LA = L * nch
    tok = lambda n, j: pl.BlockSpec((None, n, 1024), lambda b, c: (b, c, j))
    sds = lambda dt: jax.ShapeDtypeStruct((B, T, 1024), dt)
    wk, uv, qt, kt, at, egl = pl.pallas_call(
        functools.partial(_gdn_prep_body, L, nch),
        grid=(B, NC // nch),
        in_specs=[pl.BlockSpec((None, LA, GDN_QKV), lambda b, c: (b, c, 0)),
                  pl.BlockSpec((None, 16, GDN_QKV), lambda b, c: (b, jnp.maximum(c * (LA // 16) - 1, 0), 0)),
                  pl.BlockSpec((None, LA, LANES), lambda b, c: (b, c, 0)),
                  pl.BlockSpec((None, CONV_W - 1, GDN_QKV), lambda b, c: (b, 0, 0)),
                  pl.BlockSpec((CONV_W, GDN_QKV), lambda b, c: (0, 0)),
                  pl.BlockSpec((2, LANES), lambda b, c: (0, 0))],
        out_specs=[tok(LA, 0), tok(LA, 0), tok(LA, 0), tok(LA, 0), tok(LA, 0),
                   pl.BlockSpec((None, nch, 8, LANES), lambda b, c: (b, c, 0, 0))],
        out_shape=[sds(BF16), sds(F32), sds(BF16), sds(BF16), sds(BF16),
                   jax.ShapeDtypeStruct((B, NC, 8, LANES), F32)],
        scratch_shapes=[pltpu.VMEM((LA + 8, GDN_QKV), F32)],
        compiler_params=_cparams(("parallel", "parallel")),
        name="gdn_prep",
    )(main, main, small, conv_buf, wc_t, par)
    NS = NC // cb
    state = pl.BlockSpec((None, GDN_HEADS, GDN_DK, GDN_DV), lambda b, c: (b, 0, 0, 0))
    o, S = pl.pallas_call(
        functools.partial(_gdn_scan_body, L, cb, NS),
        grid=(B, NS),
        in_specs=[tok(cb * L, 0), tok(cb * L, 0), tok(cb * L, 0), tok(cb * L, 0), tok(cb * L, 0),
                  pl.BlockSpec((None, cb, 8, LANES), lambda b, c: (b, c, 0, 0)),
                  tok(cb * L, 3), state,
                  pl.BlockSpec((1, GDN_DV), lambda b, c: (0, 0))],
        out_specs=[tok(cb * L, 0), state],
        out_shape=[sds(BF16), jax.ShapeDtypeStruct((B, GDN_HEADS, GDN_DK, GDN_DV), F32)],
        scratch_shapes=[pltpu.VMEM((GDN_HEADS, GDN_DK, GDN_DV), F32)],
        compiler_params=_cparams(("parallel", "arbitrary")),
        name="gdn_scan",
    )(wk, uv, qt, kt, at, egl, main, S0, onorm.reshape(1, GDN_DV))
    new_buf = main[:, T - (CONV_W - 1):, :GDN_QKV].astype(F32)
    return o, new_buf, S


ML_CHUNK = 64


def _mlstm_gates(small_ref, bias_ref):
    sc = GATE_CAP * jnp.tanh((small_ref[...] + bias_ref[...]) / GATE_CAP)
    return sc, jax.nn.log_sigmoid(sc)


def _mlstm_prep_body(L, nch, qk_ref, v_ref, small_ref, bias_ref, hl_ref, st_ref, kv_ref, ks_ref):
    ig_all, lf_all = _mlstm_gates(small_ref, bias_ref)
    incl, _, _ = _tri_masks(L)
    tril_b = jnp.where(incl, 1.0, 0.0).astype(BF16)
    lane = lax.broadcasted_iota(jnp.int32, (1, LANES), 1)
    H = ML_HEADS
    units = [(ci, h) for ci in range(nch) for h in range(H)]
    rows_of = lambda ci: slice(ci * L, (ci + 1) * L)
    q_of = lambda ci, h: qk_ref[rows_of(ci), h * ML_DQK:(h + 1) * ML_DQK]
    k_of = lambda ci, h: qk_ref[rows_of(ci), (H + h) * ML_DQK:(H + h + 1) * ML_DQK].astype(F32) * (ML_DQK ** -0.5)
    v_of = lambda ci, h: v_ref[rows_of(ci), h * ML_DV:(h + 1) * ML_DV]

    gates = []
    for ci in range(nch):
        b_all = _cumsum_rows(tril_b, lf_all[rows_of(ci)])
        igc = ig_all[rows_of(ci)]
        gates.append((b_all, igc, b_all.T, igc.T))
    qk = [_dot_nt(q_of(ci, h), k_of(ci, h)) for ci, h in units]
    Wl, dmaxs = [], []
    for (ci, h), s in zip(units, qk):
        b_all, igc, bT, igT = gates[ci]
        D = jnp.where(incl, b_all[:, H + h:H + h + 1] + (igT[h:h + 1, :] - bT[H + h:H + h + 1, :]), NEG_BIG)
        dmax = jnp.max(D, axis=-1, keepdims=True)
        dmaxs.append(dmax)
        Wl.append(jnp.exp(D - dmax) * s)
    for (ci, h), w in zip(units, Wl):
        hl_ref[rows_of(ci), h * ML_DV:(h + 1) * ML_DV] = _dot(w, v_of(ci, h))
    wks = []
    for (ci, h), dmax in zip(units, dmaxs):
        b_all, igc, _, _ = gates[ci]
        b_c = b_all[:, H + h:H + h + 1]
        wks.append(jnp.exp(b_c[L - 1:L, :] - b_c + igc[:, h:h + 1] - dmax[L - 1:L, :]) * k_of(ci, h))
    for (ci, h), wk in zip(units, wks):
        kv_ref[ci, h] = _dot_tn(wk, v_of(ci, h))
        ks_ref[ci, h:h + 1, :] = jnp.sum(wk, axis=0, keepdims=True)
    for ci in range(nch):
        stats = jnp.zeros((L, LANES), F32)
        for h in range(H):
            u = ci * H + h
            stats = jnp.where(lane == h, gates[ci][0][:, H + h:H + h + 1],
                              jnp.where(lane == H + h, dmaxs[u],
                                        jnp.where(lane == 2 * H + h, jnp.sum(Wl[u], axis=-1, keepdims=True), stats)))
        st_ref[rows_of(ci), :] = stats


def _mlstm_scan_body(L, cb, NS, q_ref, hl_ref, st_ref, kv_ref, ks_ref, op_ref, C0_ref, n0_ref, m0_ref, on_ref,
                     o_ref, Cout_ref, nout_ref, mout_ref, C_scr, n_scr, m_scr):
    c = pl.program_id(1)

    @pl.when(c == 0)
    def _():
        C_scr[...] = C0_ref[...]
        n_scr[...] = n0_ref[...]
        m_scr[...] = m0_ref[...]

    onorm = on_ref[...]
    H = ML_HEADS
    units = [(ci, h) for ci in range(cb) for h in range(H)]
    rows_of = lambda ci: slice(ci * L, (ci + 1) * L)
    sts = [st_ref[rows_of(ci), :] for ci in range(cb)]
    pre = {}
    for h in range(H):
        m_prev = m_scr[h:h + 1, 0:1]
        n_prev = n_scr[h:h + 1, :]
        C = C_scr[h]
        for ci in range(cb):
            b_c = sts[ci][:, h:h + 1]
            dmax = sts[ci][:, H + h:H + h + 1]
            mt = jnp.maximum(b_c + m_prev, dmax)
            pre[(ci, h)] = (C, n_prev, m_prev, mt)
            mL = mt[L - 1:L, :]
            fL = jnp.exp(b_c[L - 1:L, :] + m_prev - mL)
            gL = jnp.exp(dmax[L - 1:L, :] - mL)
            C = fL * C + gL * kv_ref[ci, h]
            n_prev = fL * n_prev + gL * ks_ref[ci, h:h + 1, :]
            m_prev = mL
        C_scr[h] = C
        n_scr[h:h + 1, :] = n_prev
        m_scr[h:h + 1, :] = jnp.broadcast_to(m_prev, (1, LANES))

    qs = {u: q_ref[rows_of(u[0]), u[1] * ML_DQK:(u[1] + 1) * ML_DQK].astype(F32) for u in units}
    qC = {u: _dot(qs[u], pre[u][0]) for u in units}
    hs = {}
    for u in units:
        ci, h = u
        _, n_prev, m_prev, mt = pre[u]
        b_c = sts[ci][:, h:h + 1]
        dmax = sts[ci][:, H + h:H + h + 1]
        sw = sts[ci][:, 2 * H + h:2 * H + h + 1]
        e_i = jnp.exp(b_c + m_prev - mt)
        e_l = jnp.exp(dmax - mt)
        num = e_i * qC[u] + e_l * hl_ref[rows_of(ci), h * ML_DV:(h + 1) * ML_DV]
        den = e_i * jnp.sum(qs[u] * n_prev, axis=-1, keepdims=True) + e_l * sw
        hs[u] = num / jnp.maximum(jnp.abs(den), jnp.exp(-mt))
    for u in units:
        ci, h = u
        og = op_ref[rows_of(ci), h * ML_DV:(h + 1) * ML_DV].astype(F32)
        o_ref[rows_of(ci), h * ML_DV:(h + 1) * ML_DV] = (_sigmoid(og) * _rms(hs[u], onorm)).astype(BF16)

    @pl.when(c == NS - 1)
    def _():
        Cout_ref[...] = C_scr[...]
        nout_ref[...] = n_scr[...]
        mout_ref[...] = m_scr[...]


def _mlstm_mixer(main, small, C0, n0, m0, bias, onorm, nch=4, cb=4):
    B, T, _ = main.shape
    L = min(ML_CHUNK, T)
    assert T % L == 0
    NC = T // L
    nch = min(nch, NC)
    cb = min(cb, NC)
    assert NC % nch == 0 and NC % cb == 0
    LA = L * nch
    H = ML_HEADS
    hl, st, kv, ks = pl.pallas_call(
        functools.partial(_mlstm_prep_body, L, nch),
        grid=(B, NC // nch),
        in_specs=[pl.BlockSpec((None, LA, 1024), lambda b, c: (b, c, 0)),
                  pl.BlockSpec((None, LA, 1024), lambda b, c: (b, c, 1)),
                  pl.BlockSpec((None, LA, LANES), lambda b, c: (b, c, 0)),
                  pl.BlockSpec((1, LANES), lambda b, c: (0, 0))],
        out_specs=[pl.BlockSpec((None, LA, 1024), lambda b, c: (b, c, 0)),
                   pl.BlockSpec((None, LA, LANES), lambda b, c: (b, c, 0)),
                   pl.BlockSpec((None, nch, H, ML_DQK, ML_DV), lambda b, c: (b, c, 0, 0, 0)),
                   pl.BlockSpec((None, nch, H, ML_DQK), lambda b, c: (b, c, 0, 0))],
        out_shape=[jax.ShapeDtypeStruct((B, T, 1024), F32),
                   jax.ShapeDtypeStruct((B, T, LANES), F32),
                   jax.ShapeDtypeStruct((B, NC, H, ML_DQK, ML_DV), F32),
                   jax.ShapeDtypeStruct((B, NC, H, ML_DQK), F32)],
        compiler_params=_cparams(("parallel", "parallel")),
        name="mlstm_prep",
    )(main, main, small, bias)
    NS = NC // cb
    m0b = jnp.broadcast_to(m0[:, :, None], (B, H, LANES))
    st_C = pl.BlockSpec((None, H, ML_DQK, ML_DV), lambda b, c: (b, 0, 0, 0))
    st_n = pl.BlockSpec((None, H, ML_DQK), lambda b, c: (b, 0, 0))
    st_m = pl.BlockSpec((None, H, LANES), lambda b, c: (b, 0, 0))
    return pl.pallas_call(
        functools.partial(_mlstm_scan_body, L, cb, NS),
        grid=(B, NS),
        in_specs=[pl.BlockSpec((None, cb * L, H * ML_DQK), lambda b, c: (b, c, 0)),
                  pl.BlockSpec((None, cb * L, 1024), lambda b, c: (b, c, 0)),
                  pl.BlockSpec((None, cb * L, LANES), lambda b, c: (b, c, 0)),
                  pl.BlockSpec((None, cb, H, ML_DQK, ML_DV), lambda b, c: (b, c, 0, 0, 0)),
                  pl.BlockSpec((None, cb, H, ML_DQK), lambda b, c: (b, c, 0, 0)),
                  pl.BlockSpec((None, cb * L, 1024), lambda b, c: (b, c, 2)),
                  st_C, st_n, st_m,
                  pl.BlockSpec((1, ML_DV), lambda b, c: (0, 0))],
        out_specs=[pl.BlockSpec((None, cb * L, 1024), lambda b, c: (b, c, 0)), st_C, st_n, st_m],
        out_shape=[jax.ShapeDtypeStruct((B, T, 1024), BF16),
                   jax.ShapeDtypeStruct((B, H, ML_DQK, ML_DV), F32),
                   jax.ShapeDtypeStruct((B, H, ML_DQK), F32),
                   jax.ShapeDtypeStruct((B, H, LANES), F32)],
        scratch_shapes=[pltpu.VMEM((H, ML_DQK, ML_DV), F32),
                        pltpu.VMEM((H, ML_DQK), F32),
                        pltpu.VMEM((H, LANES), F32)],
        compiler_params=_cparams(("parallel", "arbitrary")),
        name="mlstm_scan",
    )(main, hl, st, kv, ks, main, C0, n0, m0b, onorm.reshape(1, ML_DV))


FOX_BIAS_LANE = FOX_DH


def _fox_prep_body(q_ref, k_ref, v_ref, small_ref, bf_ref, qn_ref, kn_ref,
                   qh_ref, k32_ref, v32_ref, lf_ref):
    r = lax.broadcasted_iota(jnp.int32, (LANES, LANES), 0) // FOX_DH
    cc = lax.broadcasted_iota(jnp.int32, (LANES, LANES), 1) // FOX_DH
    avg = jnp.where(r == cc, 1.0 / FOX_DH, 0.0).astype(BF16)

    def head_rms(x, gain):
        parts = []
        for t in range(x.shape[1] // LANES):
            xt = x[:, t * LANES:(t + 1) * LANES]
            hi, lo = _split2(xt * xt)
            ms = (jnp.dot(hi, avg, preferred_element_type=F32)
                  + jnp.dot(lo, avg, preferred_element_type=F32))
            parts.append(xt * lax.rsqrt(ms + EPS))
        return jnp.concatenate(parts, axis=-1) * gain

    qn = head_rms(q_ref[...].astype(F32), qn_ref[...]) * (LOG2E * FOX_DH ** -0.5)
    lane = lax.broadcasted_iota(jnp.int32, (1, LANES), 1)
    ones3 = jnp.where(lane < FOX_BIAS_LANE + 3, 1.0, 0.0)
    for h in range(FOX_HEADS):
        xt = qn[:, (h // 2) * LANES:(h // 2 + 1) * LANES]
        if h % 2:
            xt = pltpu.roll(xt, FOX_DH, axis=1)
        qh_ref[h] = jnp.where(lane < FOX_DH, xt, ones3).astype(BF16)
    k32_ref[...] = head_rms(k_ref[...].astype(F32), kn_ref[...])
    v32_ref[...] = v_ref[...].astype(F32)
    lf_ref[...] = jax.nn.log_sigmoid(small_ref[...] + bf_ref[...])


def _fox_prep(main, small, bf_row, qn_row, kn_row, tm=512):
    B, T, _ = main.shape
    tm = min(tm, T)
    assert T % tm == 0
    blk = lambda j: pl.BlockSpec((None, tm, 1024), lambda b, i: (b, i, j))
    row = lambda n: pl.BlockSpec((1, n), lambda b, i: (0, 0))
    o1024 = pl.BlockSpec((None, tm, 1024), lambda b, i: (b, i, 0))
    return pl.pallas_call(
        _fox_prep_body,
        grid=(B, T // tm),
        in_specs=[blk(0), blk(1), blk(2),
                  pl.BlockSpec((None, tm, LANES), lambda b, i: (b, i, 0)),
                  row(LANES), row(1024), row(1024)],
        out_specs=[pl.BlockSpec((None, FOX_HEADS, tm, LANES), lambda b, i: (b, 0, i, 0)),
                   o1024, o1024,
                   pl.BlockSpec((None, tm, LANES), lambda b, i: (b, i, 0))],
        out_shape=[jax.ShapeDtypeStruct((B, FOX_HEADS, T, LANES), BF16),
                   jax.ShapeDtypeStruct((B, T, 1024), F32),
                   jax.ShapeDtypeStruct((B, T, 1024), F32),
                   jax.ShapeDtypeStruct((B, T, LANES), F32)],
        compiler_params=_cparams(("parallel", "parallel")),
        name="fox_prep",
    )(main, main, main, small, bf_row, qn_row, kn_row)


def _fox_kv_body(tm, n_past, *refs):
    if n_past:
        kp_ref, vp_ref, lp_ref, k_ref, v_ref, lf_ref, kh_ref, vh_ref, carry = refs
    else:
        k_ref, v_ref, lf_ref, kh_ref, vh_ref, carry = refs
    i = pl.program_id(1)

    @pl.when(i == 0)
    def _():
        carry[...] = jnp.zeros_like(carry)

    k = k_ref[...]
    v = v_ref[...]
    lf = lf_ref[...]
    if n_past:
        past = i < n_past
        k = jnp.where(past, kp_ref[...], k)
        v = jnp.where(past, vp_ref[...], v)
        lf = jnp.where(past, lp_ref[...], lf)
    incl, _, _ = _tri_masks(tm)
    tril_b = jnp.where(incl, 1.0, 0.0).astype(BF16)
    F = _cumsum_rows(tril_b, lf) + carry[0:1, :]
    carry[0:1, :] = F[tm - 1:tm, :]
    nb = -LOG2E * F
    b1 = nb.astype(BF16).astype(F32)
    r1 = nb - b1
    b2 = r1.astype(BF16).astype(F32)
    b3 = r1 - b2
    lane = lax.broadcasted_iota(jnp.int32, (1, LANES), 1)
    ones1 = jnp.where(lane == FOX_BIAS_LANE, 1.0, 0.0)
    for h in range(FOX_HEADS):
        kt = k[:, (h // 2) * LANES:(h // 2 + 1) * LANES]
        vt = v[:, (h // 2) * LANES:(h // 2 + 1) * LANES]
        if h % 2:
            kt = pltpu.roll(kt, FOX_DH, axis=1)
            vt = pltpu.roll(vt, FOX_DH, axis=1)
        bias = jnp.where(lane == FOX_BIAS_LANE, b1[:, h:h + 1],
                         jnp.where(lane == FOX_BIAS_LANE + 1, b2[:, h:h + 1],
                                   jnp.where(lane == FOX_BIAS_LANE + 2, b3[:, h:h + 1], 0.0)))
        kh_ref[h] = jnp.where(lane < FOX_DH, kt, bias).astype(BF16)
        vh_ref[h] = jnp.where(lane < FOX_DH, vt, ones1).astype(BF16)


def _fox_kv(k_new, v_new, lf_new, past=None, tm=512):
    B, T, _ = k_new.shape
    if past is None:
        tm = min(tm, T)
        assert T % tm == 0
        n_past, n_new = 0, T // tm
        args, specs = [], []
    else:
        k_past, v_past, lf_past = past
        P = k_past.shape[1]
        tm = min(tm, P)
        assert P % tm == 0 and T <= tm
        n_past, n_new = P // tm, 1
        pad = lambda a: jnp.pad(a, ((0, 0), (0, tm - T), (0, 0)))
        k_new, v_new, lf_new = pad(k_new), pad(v_new), pad(lf_new)
        clamp = lambda b, i: (b, jnp.minimum(i, n_past - 1), 0)
        args = [k_past, v_past, lf_past]
        specs = [pl.BlockSpec((None, tm, 1024), clamp), pl.BlockSpec((None, tm, 1024), clamp),
                 pl.BlockSpec((None, tm, LANES), clamp)]
    new_idx = lambda b, i: (b, jnp.maximum(i - n_past, 0), 0)
    N = (n_past + n_new) * tm
    hm = pl.BlockSpec((None, FOX_HEADS, tm, LANES), lambda b, i: (b, 0, i, 0))
    return pl.pallas_call(
        functools.partial(_fox_kv_body, tm, n_past),
        grid=(B, n_past + n_new),
        in_specs=specs + [pl.BlockSpec((None, tm, 1024), new_idx),
                          pl.BlockSpec((None, tm, 1024), new_idx),
                          pl.BlockSpec((None, tm, LANES), new_idx)],
        out_specs=[hm, hm],
        out_shape=[jax.ShapeDtypeStruct((B, FOX_HEADS, N, LANES), BF16),
                   jax.ShapeDtypeStruct((B, FOX_HEADS, N, LANES), BF16)],
        scratch_shapes=[pltpu.VMEM((8, LANES), F32)],
        compiler_params=_cparams(("parallel", "arbitrary")),
        name="fox_kv",
    )(*args, k_new, v_new, lf_new)


def _flash_body(P, tq, tk, rq, qi_tab, kj_tab, q_ref, k_ref, v_ref, gate_ref, o_ref, m_scr, acc_scr):
    s_idx = pl.program_id(2)
    qi = qi_tab[s_idx]
    kj = kj_tab[s_idx]
    last_kj = (P + (qi + 1) * tq - 1) // tk
    aligned = tq == tk and P % tk == 0

    @pl.when(kj == 0)
    def _():
        m_scr[...] = jnp.full(m_scr.shape, NEG_BIG, F32)
        acc_scr[...] = jnp.zeros_like(acc_scr)

    def step(masked):
        units = [(hh, r) for r in range(tq // rq) for hh in range(2)]

        def logits(u):
            hh, r = u
            rows = slice(r * rq, (r + 1) * rq)
            nk = min(tk, (r + 1) * rq) if (masked and aligned) else tk
            s = lax.dot_general(q_ref[hh, rows, :], k_ref[hh, 0:nk, :], (((1,), (1,)), ((), ())),
                                preferred_element_type=F32)
            if masked:
                qpos = P + qi * tq + r * rq + lax.broadcasted_iota(jnp.int32, (rq, nk), 0)
                kpos = kj * tk + lax.broadcasted_iota(jnp.int32, (rq, nk), 1)
                s = jnp.where(kpos <= qpos, s, NEG_BIG)
            return s, nk

        def probs(u, s):
            hh, r = u
            rows = slice(r * rq, (r + 1) * rq)
            m_prev = m_scr[hh, rows, :][:, 0:1]
            m_new = jnp.maximum(m_prev, jnp.max(s, axis=-1, keepdims=True))
            m_scr[hh, rows, :] = jnp.broadcast_to(m_new, (rq, LANES))
            return jnp.exp2(m_prev - m_new), jnp.exp2(s - m_new).astype(BF16)

        def accumulate(u, alpha, p, nk):
            hh, r = u
            rows = slice(r * rq, (r + 1) * rq)
            acc_scr[hh, rows, :] = alpha * acc_scr[hh, rows, :] + jnp.dot(
                p, v_ref[hh, 0:nk, :], preferred_element_type=F32)

        n = len(units)
        pend_s = {0: logits(units[0])}
        pend_p = {}
        for i in range(n):
            if i + 1 < n:
                pend_s[i + 1] = logits(units[i + 1])
            s, nk = pend_s.pop(i)
            pend_p[i] = probs(units[i], s) + (nk,)
            if i >= 1:
                accumulate(units[i - 1], *pend_p.pop(i - 1))
        accumulate(units[n - 1], *pend_p.pop(n - 1))

    needs_mask = (kj + 1) * tk - 1 > P + qi * tq

    @pl.when(needs_mask)
    def _():
        step(True)

    @pl.when(jnp.logical_not(needs_mask))
    def _():
        step(False)

    @pl.when(kj == last_kj)
    def _():
        lane = lax.broadcasted_iota(jnp.int32, (1, LANES), 1)
        a0 = acc_scr[0]
        a1 = acc_scr[1]
        o0 = a0 * (1.0 / a0[:, FOX_BIAS_LANE:FOX_BIAS_LANE + 1])
        o1 = a1 * (1.0 / a1[:, FOX_BIAS_LANE:FOX_BIAS_LANE + 1])
        o = jnp.where(lane < FOX_DH, o0, pltpu.roll(o1, FOX_DH, axis=1))
        g = gate_ref[...].astype(F32)
        o_ref[...] = (o * _sigmoid(g)).astype(BF16)


def _fox_flash(qh, kh, vh, gate_src, P, tq, tk, rq):
    B, _, T, _ = qh.shape
    N = kh.shape[2]
    assert T % tq == 0 and N % tk == 0 and tq % rq == 0
    nq = T // tq
    pairs = [(qi, kj) for qi in range(nq) for kj in range((P + (qi + 1) * tq - 1) // tk + 1)]
    qi_tab = jnp.asarray(np.array([p[0] for p in pairs], np.int32))
    kj_tab = jnp.asarray(np.array([p[1] for p in pairs], np.int32))
    HP = FOX_HEADS // 2
    gate_blk0 = 3 * (1024 // LANES)
    grid_spec = pltpu.PrefetchScalarGridSpec(
        num_scalar_prefetch=2,
        grid=(B, HP, len(pairs)),
        in_specs=[pl.BlockSpec((None, 2, tq, LANES), lambda b, p, s, qt, kt: (b, p, qt[s], 0)),
                  pl.BlockSpec((None, 2, tk, LANES), lambda b, p, s, qt, kt: (b, p, kt[s], 0)),
                  pl.BlockSpec((None, 2, tk, LANES), lambda b, p, s, qt, kt: (b, p, kt[s], 0)),
                  pl.BlockSpec((None, tq, LANES), lambda b, p, s, qt, kt: (b, qt[s], gate_blk0 + p))],
        out_specs=pl.BlockSpec((None, tq, LANES), lambda b, p, s, qt, kt: (b, qt[s], p)),
        scratch_shapes=[pltpu.VMEM((2, tq, LANES), F32),
                        pltpu.VMEM((2, tq, LANES), F32)],
    )
    return pl.pallas_call(
        functools.partial(_flash_body, P, tq, tk, rq),
        grid_spec=grid_spec,
        out_shape=jax.ShapeDtypeStruct((B, T, 1024), BF16),
        compiler_params=_cparams(("parallel", "parallel", "arbitrary")),
        name="fox_flash",
    )(qi_tab, kj_tab, qh, kh, vh, gate_src)


def _pad_lanes(a, offset=0):
    n = a.shape[-1]
    pads = [(0, 0)] * (a.ndim - 1) + [(offset, LANES - n - offset)]
    return jnp.pad(a, pads)


def _trunk(x, gdn_conv, gdn_S, ml_C, ml_n, ml_m, fox_k, fox_v, fox_lf, mem_k, mem_v, W):
    B, T, D = x.shape
    M = B * T
    x2 = x.reshape(M, D)
    conv_o, S_o, C_o, n_o, m_o, k_o, v_o, lf_o = [], [], [], [], [], [], [], []
    for layer in range(DEPTH):
        kind, j = layer % N_MIXERS, layer // N_MIXERS
        gains = W["norm_gains"][layer]
        if kind == 0:
            main, small = _prenorm_mm(x2, gains[0], W["gdn_w_in"], j, GDN_MAIN, W["gdn_w_small"], out_dtype=F32)
            o, buf, S = _gdn_mixer(main.reshape(B, T, GDN_MAIN), small.reshape(B, T, LANES),
                                   gdn_conv[j], gdn_S[j], W["gdn_wc_t"][j], W["gdn_par"][j],
                                   W["gdn_onorm"][j])
            conv_o.append(buf)
            S_o.append(S)
            w_out = W["gdn_w_out"]
        elif kind == 1:
            main, small = _prenorm_mm(x2, gains[0], W["ml_w_in"], j, ML_MAIN, W["ml_w_small"])
            o, C, n, m = _mlstm_mixer(main.reshape(B, T, ML_MAIN), small.reshape(B, T, LANES),
                                      ml_C[j], ml_n[j], ml_m[j], W["ml_bias"][j], W["ml_onorm"][j])
            C_o.append(C)
            n_o.append(n)
            m_o.append(m[:, :, 0])
            w_out = W["ml_w_out"]
        else:
            main, small = _prenorm_mm(x2, gains[0], W["fox_w_in"], j, FOX_MAIN, W["fox_w_small"])
            main3 = main.reshape(B, T, FOX_MAIN)
            qh, k32, v32, lfp = _fox_prep(main3, small.reshape(B, T, LANES), W["fox_bf"][j],
                                          W["fox_qn"][j], W["fox_kn"][j])
            P = fox_k[j].shape[1]
            if P == 0:
                kh, vh = _fox_kv(k32, v32, lfp)
                tq, tk, rq = min(1024, T), min(1024, T), 256
            else:
                kh, vh = _fox_kv(k32, v32, lfp, past=(fox_k[j].reshape(B, P, 1024), fox_v[j].reshape(B, P, 1024),
                                                      _pad_lanes(fox_lf[j])), tm=256)
                tq, tk, rq = T, kh.shape[2], T
            o = _fox_flash(qh, kh, vh, main3, P, tq, tk, rq)
            k_o.append(k32.reshape(B, T, FOX_HEADS, FOX_DH))
            v_o.append(v32.reshape(B, T, FOX_HEADS, FOX_DH))
            lf_o.append(lfp[:, :, :FOX_HEADS])
            w_out = W["fox_w_out"]
        x2 = _memattn(o, w_out, j, gains[1], x2.reshape(B, T, D), gains[2], W["x_w_q"],
                      mem_k[layer], mem_v[layer], W["x_w_o"], layer, gains[3]).reshape(M, D)
        x2 = _ffn(x2, gains[4], W["ffn_w_gu"], W["ffn_w_down"], layer, gains[5])
    return (x2.reshape(B, T, D), jnp.stack(conv_o), jnp.stack(S_o), jnp.stack(C_o), jnp.stack(n_o),
            jnp.stack(m_o), jnp.stack(k_o), jnp.stack(v_o), jnp.stack(lf_o))


def kernel(x_prompt, x_sample, state_gdn_conv, state_gdn_S, state_mlstm_C, state_mlstm_n, state_mlstm_m, cache_fox_k, cache_fox_v, cache_fox_logf, cache_mem_k, cache_mem_v, mem_prompt, norm_gains, gdn_w_in, gdn_w_conv, gdn_a_log, gdn_dt_bias, gdn_onorm, gdn_w_out, ml_w_in, ml_b_i, ml_b_f, ml_onorm, ml_w_out, fox_w_in, fox_b_f, fox_qk_norm, fox_w_out, x_w_q, x_w_kv, x_w_o, ffn_w_gu, ffn_w_down):
    B = x_prompt.shape[0]
    DB = x_sample.shape[0]
    n_a, n_b, n_c = gdn_w_in.shape[0], ml_w_in.shape[0], fox_w_in.shape[0]
    W = {
        "norm_gains": norm_gains,
        "gdn_w_in": gdn_w_in.astype(BF16),
        "gdn_w_small": _pad_lanes(gdn_w_in[:, :, GDN_MAIN:]).astype(BF16),
        "gdn_wc_t": jnp.transpose(gdn_w_conv, (0, 2, 1)),
        "gdn_par": jnp.stack([_pad_lanes(gdn_a_log, GDN_HEADS), _pad_lanes(gdn_dt_bias, GDN_HEADS)], axis=1),
        "gdn_onorm": gdn_onorm,
        "gdn_w_out": gdn_w_out.astype(BF16),
        "ml_w_in": ml_w_in.astype(BF16),
        "ml_w_small": _pad_lanes(ml_w_in[:, :, ML_MAIN:]).astype(BF16),
        "ml_bias": _pad_lanes(jnp.concatenate([ml_b_i, ml_b_f], axis=-1))[:, None, :],
        "ml_onorm": ml_onorm,
        "ml_w_out": ml_w_out.astype(BF16),
        "fox_w_in": fox_w_in.astype(BF16),
        "fox_w_small": _pad_lanes(fox_w_in[:, :, FOX_MAIN:]).astype(BF16),
        "fox_bf": _pad_lanes(fox_b_f)[:, None, :],
        "fox_qn": jnp.tile(fox_qk_norm[:, 0, :], (1, FOX_HEADS))[:, None, :],
        "fox_kn": jnp.tile(fox_qk_norm[:, 1, :], (1, FOX_HEADS))[:, None, :],
        "fox_w_out": fox_w_out.astype(BF16),
        "x_w_q": x_w_q.astype(BF16),
        "x_w_o": x_w_o.astype(BF16),
        "ffn_w_gu": ffn_w_gu.astype(BF16),
        "ffn_w_down": ffn_w_down.astype(BF16),
    }

    mem2 = mem_prompt.reshape(B * N_MEM, D_MODEL)
    x_w_kv_b = x_w_kv.astype(BF16)
    kv = [_prenorm_mm(mem2, norm_gains[l, 6], x_w_kv_b, l, 2 * D_MODEL, out_dtype=F32).reshape(B, N_MEM, 2 * D_MODEL)
          for l in range(DEPTH)]
    p_mem_k = jnp.stack([a[:, :, :D_MODEL].reshape(B, N_MEM, X_HEADS, X_DH) for a in kv])
    p_mem_v = jnp.stack([a[:, :, D_MODEL:].reshape(B, N_MEM, X_HEADS, X_DH) for a in kv])

    cmk = cache_mem_k.reshape(DEPTH * DB, N_MEM, D_MODEL)
    cmv = cache_mem_v.reshape(DEPTH * DB, N_MEM, D_MODEL)
    zeros = lambda *s: jnp.zeros(s, F32)
    prompt = _trunk(
        x_prompt,
        zeros(n_a, B, CONV_W - 1, GDN_QKV), zeros(n_a, B, GDN_HEADS, GDN_DK, GDN_DV),
        zeros(n_b, B, ML_HEADS, ML_DQK, ML_DV), zeros(n_b, B, ML_HEADS, ML_DQK), zeros(n_b, B, ML_HEADS),
        zeros(n_c, B, 0, FOX_HEADS, FOX_DH), zeros(n_c, B, 0, FOX_HEADS, FOX_DH), zeros(n_c, B, 0, FOX_HEADS),
        [(a, 0, 0) for a in kv], [(a, 0, 1) for a in kv], W)
    sample = _trunk(
        x_sample, state_gdn_conv, state_gdn_S, state_mlstm_C, state_mlstm_n, state_mlstm_m,
        cache_fox_k, cache_fox_v, cache_fox_logf,
        [(cmk, l * DB, 0) for l in range(DEPTH)], [(cmv, l * DB, 0) for l in range(DEPTH)], W)
    (y_p, p_conv, p_S, p_C, p_n, p_m, p_k, p_v, p_lf) = prompt
    (y_s, s_conv, s_S, s_C, s_n, s_m, s_k, s_v, s_lf) = sample
    return (y_p, y_s, p_conv, p_S, p_C, p_n, p_m, p_k, p_v, p_lf, p_mem_k, p_mem_v,
            s_conv, s_S, s_C, s_n, s_m, s_k, s_v, s_lf)
```

```python
import functools

import numpy as np
import jax
import jax.numpy as jnp
from jax import lax
from jax.experimental import pallas as pl
from jax.experimental.pallas import tpu as pltpu

F32 = jnp.float32
BF16 = jnp.bfloat16

D_MODEL = 1024
DEPTH = 4
N_MIXERS = 3
EPS = 1e-6
CONV_W = 4

GDN_HEADS = 8
GDN_DK = 128
GDN_DV = 128
GDN_QKV = 3072
GDN_MAIN = 4096

ML_HEADS = 4
ML_DQK = 128
ML_DV = 256
ML_MAIN = 3072
GATE_CAP = 15.0

FOX_HEADS = 16
FOX_DH = 64
FOX_MAIN = 4096

N_MEM = 256
X_HEADS = 4
X_DH = 256
D_FF = 2816

LANES = 128
NEG_BIG = -1e30
LOG2E = 1.4426950408889634

VMEM_LIMIT = 48 * 1024 * 1024


def _wspec(li, blk, imap):
    return pl.BlockSpec((None,) + blk, lambda *g: (li,) + imap(*g))


def _cparams(sem):
    return pltpu.CompilerParams(dimension_semantics=sem, vmem_limit_bytes=VMEM_LIMIT)


def _rms(x, g):
    return x * lax.rsqrt(jnp.mean(x * x, axis=-1, keepdims=True) + EPS) * g


def _sigmoid(x):
    return 1.0 / (1.0 + jnp.exp(-x))


def _dot(a, b):
    return jnp.dot(a.astype(BF16), b.astype(BF16), preferred_element_type=F32)


def _dot_nt(a, b):
    return lax.dot_general(a.astype(BF16), b.astype(BF16), (((1,), (1,)), ((), ())),
                           preferred_element_type=F32)


def _dot_tn(a, b):
    return lax.dot_general(a.astype(BF16), b.astype(BF16), (((0,), (0,)), ((), ())),
                           preferred_element_type=F32)


def _split2(x):
    hi = x.astype(BF16)
    lo = (x - hi.astype(F32)).astype(BF16)
    return hi, lo


def _dot_x3(a, b):
    ah, al = _split2(a)
    bh, bl = _split2(b)
    d = lambda u, v: jnp.dot(u, v, preferred_element_type=F32)
    return d(ah, bh) + (d(ah, bl) + d(al, bh))


def _cumsum_rows(tril_b, x):
    h = x.astype(BF16)
    r = x - h.astype(F32)
    m = r.astype(BF16)
    l = (r - m.astype(F32)).astype(BF16)
    d = lambda v: jnp.dot(tril_b, v, preferred_element_type=F32)
    return d(h) + (d(m) + d(l))


def _tri_masks(L):
    row = lax.broadcasted_iota(jnp.int32, (L, L), 0)
    col = lax.broadcasted_iota(jnp.int32, (L, L), 1)
    return row >= col, row > col, row == col


def _prenorm_mm_body(has_small, *refs):
    if has_small:
        x_ref, g_ref, w_ref, ws_ref, o_ref, os_ref, xn_ref = refs
    else:
        x_ref, g_ref, w_ref, o_ref, xn_ref = refs

    @pl.when(pl.program_id(1) == 0)
    def _():
        xb = _rms(x_ref[...], g_ref[...]).astype(BF16)
        xn_ref[...] = xb
        if has_small:
            os_ref[...] = jnp.dot(xb, ws_ref[...], preferred_element_type=F32)

    o_ref[...] = jnp.dot(xn_ref[...], w_ref[...], preferred_element_type=F32).astype(o_ref.dtype)


def _prenorm_mm(x, gain, w_main, li, n_out, w_small=None, out_dtype=BF16, tm=1024, tn=1024):
    M, D = x.shape
    N = n_out
    tm = min(tm, M)
    tn = min(tn, N)
    assert M % tm == 0 and N % tn == 0
    has_small = w_small is not None
    in_specs = [pl.BlockSpec((tm, D), lambda i, j: (i, 0)),
                pl.BlockSpec((1, D), lambda i, j: (0, 0)),
                _wspec(li, (D, tn), lambda i, j: (0, j))]
    out_specs = [pl.BlockSpec((tm, tn), lambda i, j: (i, j))]
    out_shape = [jax.ShapeDtypeStruct((M, N), out_dtype)]
    args = [x, gain.reshape(1, D), w_main]
    if has_small:
        in_specs.append(_wspec(li, (D, LANES), lambda i, j: (0, 0)))
        out_specs.append(pl.BlockSpec((tm, LANES), lambda i, j: (i, 0)))
        out_shape.append(jax.ShapeDtypeStruct((M, LANES), F32))
        args.append(w_small)
    res = pl.pallas_call(
        functools.partial(_prenorm_mm_body, has_small),
        grid=(M // tm, N // tn),
        in_specs=in_specs, out_specs=out_specs, out_shape=out_shape,
        scratch_shapes=[pltpu.VMEM((tm, D), BF16)],
        compiler_params=_cparams(("parallel", "arbitrary")),
        name="prenorm_mm",
    )(*args)
    return res if has_small else res[0]


def _memattn_body(nb, o_ref, wout_ref, g1_ref, x_ref, g2_ref, wq_ref, mk_ref, mv_ref, wo_ref, g3_ref, y_ref):
    tq, D = x_ref.shape[1], x_ref.shape[2]
    R = nb * tq
    o_in = o_ref[...].reshape(R, o_ref.shape[2])
    x = x_ref[...].reshape(R, D)
    x = x + _rms(jnp.dot(o_in, wout_ref[...], preferred_element_type=F32), g1_ref[...])
    h = _rms(x, g2_ref[...]).astype(BF16)
    q = jnp.dot(h, wq_ref[...], preferred_element_type=F32).astype(BF16)
    rows = []
    for bi in range(nb):
        mk = mk_ref[bi].astype(BF16)
        mv = mv_ref[bi].astype(BF16)
        qb = q[bi * tq:(bi + 1) * tq]
        outs = []
        for hd in range(X_HEADS):
            sl = slice(hd * X_DH, (hd + 1) * X_DH)
            s = _dot_nt(qb[:, sl], mk[:, sl]) * (X_DH ** -0.5)
            p = jnp.exp(s - jnp.max(s, axis=-1, keepdims=True))
            p = p * (1.0 / jnp.sum(p, axis=-1, keepdims=True))
            outs.append(jnp.dot(p.astype(BF16), mv[:, sl], preferred_element_type=F32))
        rows.append(jnp.concatenate(outs, axis=-1).astype(BF16))
    o = rows[0] if nb == 1 else jnp.concatenate(rows, axis=0)
    y = jnp.dot(o, wo_ref[...], preferred_element_type=F32)
    y_ref[...] = (x + _rms(y, g3_ref[...])).reshape(nb, tq, D)


def _memattn(o, w_out, lo, g1, x, g2, wq, mem_k, mem_v, wo, lx, g3, tq=512, rows_per_step=256):
    B, T, D = x.shape
    tq = min(tq, T)
    assert T % tq == 0
    nb = max(1, min(B, rows_per_step // tq)) if tq == T else 1
    assert B % nb == 0
    HD = X_HEADS * X_DH
    K = o.shape[-1]
    (mk_arr, koff, kcol), (mv_arr, voff, vcol) = mem_k, mem_v
    assert koff % nb == 0 and voff % nb == 0
    return pl.pallas_call(
        functools.partial(_memattn_body, nb),
        grid=(B // nb, T // tq),
        in_specs=[pl.BlockSpec((nb, tq, K), lambda b, i: (b, i, 0)),
                  _wspec(lo, (K, D), lambda b, i: (0, 0)),
                  pl.BlockSpec((1, D), lambda b, i: (0, 0)),
                  pl.BlockSpec((nb, tq, D), lambda b, i: (b, i, 0)),
                  pl.BlockSpec((1, D), lambda b, i: (0, 0)),
                  _wspec(lx, (D, HD), lambda b, i: (0, 0)),
                  pl.BlockSpec((nb, N_MEM, HD), lambda b, i: (koff // nb + b, 0, kcol)),
                  pl.BlockSpec((nb, N_MEM, HD), lambda b, i: (voff // nb + b, 0, vcol)),
                  _wspec(lx, (HD, D), lambda b, i: (0, 0)),
                  pl.BlockSpec((1, D), lambda b, i: (0, 0))],
        out_specs=pl.BlockSpec((nb, tq, D), lambda b, i: (b, i, 0)),
        out_shape=jax.ShapeDtypeStruct((B, T, D), F32),
        compiler_params=_cparams(("parallel", "parallel")),
        name="memattn",
    )(o, w_out, g1.reshape(1, D), x, g2.reshape(1, D), wq, mk_arr, mv_arr, wo, g3.reshape(1, D))


def _ffn_body(nf, x_ref, g4_ref, wg_ref, wu_ref, wd_ref, g5_ref, y_ref, xn_ref, acc_ref):
    f = pl.program_id(1)

    @pl.when(f == 0)
    def _():
        xn_ref[...] = _rms(x_ref[...], g4_ref[...]).astype(BF16)
        acc_ref[...] = jnp.zeros_like(acc_ref)

    xn = xn_ref[...]
    g = jnp.dot(xn, wg_ref[...], preferred_element_type=F32)
    u = jnp.dot(xn, wu_ref[...], preferred_element_type=F32)
    a = (g * _sigmoid(g)) * u
    acc_ref[...] += jnp.dot(a.astype(BF16), wd_ref[...], preferred_element_type=F32)

    @pl.when(f == nf - 1)
    def _():
        y_ref[...] = x_ref[...] + _rms(acc_ref[...], g5_ref[...])


def _ffn(x, g4, w_gu, w_down, li, g5, tm=1024, tf=1408):
    M, D = x.shape
    tm = min(tm, M)
    assert M % tm == 0 and D_FF % tf == 0
    nf = D_FF // tf
    return pl.pallas_call(
        functools.partial(_ffn_body, nf),
        grid=(M // tm, nf),
        in_specs=[pl.BlockSpec((tm, D), lambda i, f: (i, 0)),
                  pl.BlockSpec((1, D), lambda i, f: (0, 0)),
                  _wspec(li, (D, tf), lambda i, f: (0, f)),
                  _wspec(li, (D, tf), lambda i, f: (0, nf + f)),
                  _wspec(li, (tf, D), lambda i, f: (f, 0)),
                  pl.BlockSpec((1, D), lambda i, f: (0, 0))],
        out_specs=pl.BlockSpec((tm, D), lambda i, f: (i, 0)),
        out_shape=jax.ShapeDtypeStruct((M, D), F32),
        scratch_shapes=[pltpu.VMEM((tm, D), BF16), pltpu.VMEM((tm, D), F32)],
        compiler_params=_cparams(("parallel", "arbitrary")),
        name="ffn",
    )(x, g4.reshape(1, D), w_gu, w_gu, w_down, g5.reshape(1, D))


GDN_CHUNK = 64
GDN_BASE = 8


def _inverse_masks(L):
    row = lax.broadcasted_iota(jnp.int32, (L, L), 0)
    col = lax.broadcasted_iota(jnp.int32, (L, L), 1)
    base = row // GDN_BASE == col // GDN_BASE
    offs = []
    s = GDN_BASE
    while s < L:
        same_pair = row // (2 * s) == col // (2 * s)
        offs.append(jnp.logical_and(same_pair, (row // s) - (col // s) == 1))
        s *= 2
    return base, row == col, offs


def _unit_lower_inverse(As, masks, mm):
    base, diag, offs = masks
    eye = jnp.where(diag, 1.0, 0.0)
    Ps = [jnp.where(base, -A, 0.0) for A in As]
    Ts = [eye + P for P in Ps]
    k = 1
    while k < GDN_BASE // 2:
        Ps = [mm(P, P) for P in Ps]
        Ts = [T + mm(T, P) for T, P in zip(Ts, Ps)]
        k *= 2
    for off in offs:
        Ms = [_dot(T, jnp.where(off, A, 0.0)) for T, A in zip(Ts, As)]
        Ts = [T - _dot(M, T) for T, M in zip(Ts, Ms)]
    return Ts


def _gdn_prep_body(L, nch, qkv_ref, prev_ref, small_ref, buf_ref, wc_ref, par_ref,
                   wk_ref, uv_ref, qt_ref, kt_ref, at_ref, egl_ref, xbuf):
    c = pl.program_id(1)
    LA = L * nch

    @pl.when(c == 0)
    def _():
        xbuf[0:8, :] = jnp.zeros((8, GDN_QKV), F32)
        xbuf[5:8, :] = buf_ref[...]

    @pl.when(c > 0)
    def _():
        xbuf[0:8, :] = prev_ref[8:16, :].astype(F32)

    xbuf[8:8 + LA, :] = qkv_ref[...].astype(F32)
    xb = xbuf[...]
    z = xb * wc_ref[0:1, :]
    for j in range(1, CONV_W):
        z = pltpu.roll(z, 1, axis=0) + xb * wc_ref[j:j + 1, :]
    y = z[8:, :]
    y = y * _sigmoid(y)

    small = small_ref[...]
    beta_all = _sigmoid(small)
    g_all = -jnp.exp(par_ref[0:1, :]) * jax.nn.softplus(small + par_ref[1:2, :])
    incl, strict, _ = _tri_masks(L)
    tril_b = jnp.where(incl, 1.0, 0.0).astype(BF16)
    masks = _inverse_masks(L)
    H = GDN_HEADS

    units = []
    for ci in range(nch):
        rows = slice(ci * L, (ci + 1) * L)
        G_all = _cumsum_rows(tril_b, g_all[rows])
        GT = G_all.T
        egl_ref[ci] = jnp.broadcast_to(jnp.exp(G_all[L - 1:L, :]), (8, LANES))
        for h in range(H):
            qh = y[rows, h * GDN_DK:(h + 1) * GDN_DK]
            kh = y[rows, (H + h) * GDN_DK:(H + h + 1) * GDN_DK]
            vh = y[rows, 2 * H * GDN_DK + h * GDN_DV:2 * H * GDN_DK + (h + 1) * GDN_DV]
            qh = qh * lax.rsqrt(jnp.sum(qh * qh, axis=-1, keepdims=True) + EPS) * (GDN_DK ** -0.5)
            kh = kh * lax.rsqrt(jnp.sum(kh * kh, axis=-1, keepdims=True) + EPS)
            bc = beta_all[rows, h:h + 1]
            Gc = G_all[:, H + h:H + h + 1]
            Gr = GT[H + h:H + h + 1, :]
            eG = jnp.exp(Gc)
            dec = jnp.exp(jnp.where(incl, Gc - Gr, NEG_BIG))
            qkk = _dot_nt(jnp.concatenate([qh, kh], axis=0), kh)
            A = jnp.where(strict, bc * qkk[L:] * dec, 0.0)
            rhs = jnp.concatenate([bc * vh, (bc * eG) * kh], axis=-1)
            cols = slice(h * GDN_DV, (h + 1) * GDN_DV)
            qt_ref[rows, cols] = (eG * qh).astype(BF16)
            kt_ref[rows, cols] = (kh * jnp.exp(Gc[L - 1:L, :] - Gc)).astype(BF16)
            at_ref[rows, cols] = jnp.concatenate(
                [qkk[:L] * dec, jnp.zeros((L, GDN_DV - L), F32)], axis=-1).astype(BF16)
            units.append((rows, cols, A, rhs))

    Ts = _unit_lower_inverse([u[2] for u in units], masks, _dot)
    for (rows, cols, _, rhs), T in zip(units, Ts):
        sol = _dot(T, rhs)
        uv_ref[rows, cols] = sol[:, :GDN_DV]
        wk_ref[rows, cols] = sol[:, GDN_DV:].astype(BF16)


def _gdn_scan_body(L, cb, NS, wk_ref, uv_ref, qt_ref, kt_ref, at_ref, egl_ref, gate_ref, S0_ref, on_ref,
                   o_ref, Sout_ref, S_scr):
    c = pl.program_id(1)

    @pl.when(c == 0)
    def _():
        S_scr[...] = S0_ref[...]

    onorm = on_ref[...]
    H = GDN_HEADS
    for ci in range(cb):
        rows = slice(ci * L, (ci + 1) * L)
        egl = egl_ref[ci]
        hs = [slice(h * GDN_DV, (h + 1) * GDN_DV) for h in range(H)]
        Ss = [S_scr[h] for h in range(H)]
        Xs = [jnp.dot(jnp.concatenate([wk_ref[rows, hs[h]], qt_ref[rows, hs[h]]], axis=0),
                      Ss[h].astype(BF16), preferred_element_type=F32) for h in range(H)]
        Us = [(uv_ref[rows, hs[h]] - Xs[h][:L]).astype(BF16) for h in range(H)]
        os = [Xs[h][L:] + jnp.dot(at_ref[rows, h * GDN_DV:h * GDN_DV + L], Us[h],
                                  preferred_element_type=F32) for h in range(H)]
        for h in range(H):
            S_scr[h] = egl[0:1, H + h:H + h + 1] * Ss[h] + lax.dot_general(
                kt_ref[rows, hs[h]], Us[h], (((0,), (0,)), ((), ())), preferred_element_type=F32)
        for h in range(H):
            gh = gate_ref[rows, hs[h]].astype(F32)
            o_ref[rows, hs[h]] = (_rms(os[h], onorm) * (gh * _sigmoid(gh))).astype(BF16)

    @pl.when(c == NS - 1)
    def _():
        Sout_ref[...] = S_scr[...]


def _gdn_mixer(main, small, conv_buf, S0, wc_t, par, onorm, nch=4, cb=8):
    B, T, _ = main.shape
    L = min(GDN_CHUNK, T)
    assert T % L == 0
    NC = T // L
    nch = min(nch, NC)
    cb = min(cb, NC)
    assert NC % nch == 0 and NC % cb == 0
    LA = L * nch
    tok = lambda n, j: pl.BlockSpec((None, n, 1024), lambda b, c: (b, c, j))
    sds = lambda dt: jax.ShapeDtypeStruct((B, T, 1024), dt)
    wk, uv, qt, kt, at, egl = pl.pallas_call(
        functools.partial(_gdn_prep_body, L, nch),
        grid=(B, NC // nch),
        in_specs=[pl.BlockSpec((None, LA, GDN_QKV), lambda b, c: (b, c, 0)),
                  pl.BlockSpec((None, 16, GDN_QKV), lambda b, c: (b, jnp.maximum(c * (LA // 16) - 1, 0), 0)),
                  pl.BlockSpec((None, LA, LANES), lambda b, c: (b, c, 0)),
                  pl.BlockSpec((None, CONV_W - 1, GDN_QKV), lambda b, c: (b, 0, 0)),
                  pl.BlockSpec((CONV_W, GDN_QKV), lambda b, c: (0, 0)),
                  pl.BlockSpec((2, LANES), lambda b, c: (0, 0))],
        out_specs=[tok(LA, 0), tok(LA, 0), tok(LA, 0), tok(LA, 0), tok(LA, 0),
                   pl.BlockSpec((None, nch, 8, LANES), lambda b, c: (b, c, 0, 0))],
        out_shape=[sds(BF16), sds(F32), sds(BF16), sds(BF16), sds(BF16),
                   jax.ShapeDtypeStruct((B, NC, 8, LANES), F32)],
        scratch_shapes=[pltpu.VMEM((LA + 8, GDN_QKV), F32)],
        compiler_params=_cparams(("parallel", "parallel")),
        name="gdn_prep",
    )(main, main, small, conv_buf, wc_t, par)
    NS = NC // cb
    state = pl.BlockSpec((None, GDN_HEADS, GDN_DK, GDN_DV), lambda b, c: (b, 0, 0, 0))
    o, S = pl.pallas_call(
        functools.partial(_gdn_scan_body, L, cb, NS),
        grid=(B, NS),
        in_specs=[tok(cb * L, 0), tok(cb * L, 0), tok(cb * L, 0), tok(cb * L, 0), tok(cb * L, 0),
                  pl.BlockSpec((None, cb, 8, LANES), lambda b, c: (b, c, 0, 0)),
                  tok(cb * L, 3), state,
                  pl.BlockSpec((1, GDN_DV), lambda b, c: (0, 0))],
        out_specs=[tok(cb * L, 0), state],
        out_shape=[sds(BF16), jax.ShapeDtypeStruct((B, GDN_HEADS, GDN_DK, GDN_DV), F32)],
        scratch_shapes=[pltpu.VMEM((GDN_HEADS, GDN_DK, GDN_DV), F32)],
        compiler_params=_cparams(("parallel", "arbitrary")),
        name="gdn_scan",
    )(wk, uv, qt, kt, at, egl, main, S0, onorm.reshape(1, GDN_DV))
    new_buf = main[:, T - (CONV_W - 1):, :GDN_QKV].astype(F32)
    return o, new_buf, S


ML_CHUNK = 64


def _mlstm_gates(small_ref, bias_ref):
    sc = GATE_CAP * jnp.tanh((small_ref[...] + bias_ref[...]) / GATE_CAP)
    return sc, jax.nn.log_sigmoid(sc)


def _mlstm_prep_body(L, nch, qk_ref, v_ref, small_ref, bias_ref, hl_ref, st_ref, kv_ref, ks_ref):
    ig_all, lf_all = _mlstm_gates(small_ref, bias_ref)
    incl, _, _ = _tri_masks(L)
    tril_b = jnp.where(incl, 1.0, 0.0).astype(BF16)
    lane = lax.broadcasted_iota(jnp.int32, (1, LANES), 1)
    H = ML_HEADS
    units = [(ci, h) for ci in range(nch) for h in range(H)]
    rows_of = lambda ci: slice(ci * L, (ci + 1) * L)
    q_of = lambda ci, h: qk_ref[rows_of(ci), h * ML_DQK:(h + 1) * ML_DQK]
    k_of = lambda ci, h: qk_ref[rows_of(ci), (H + h) * ML_DQK:(H + h + 1) * ML_DQK].astype(F32) * (ML_DQK ** -0.5)
    v_of = lambda ci, h: v_ref[rows_of(ci), h * ML_DV:(h + 1) * ML_DV]

    gates = []
    for ci in range(nch):
        b_all = _cumsum_rows(tril_b, lf_all[rows_of(ci)])
        igc = ig_all[rows_of(ci)]
        gates.append((b_all, igc, b_all.T, igc.T))
    qk = [_dot_nt(q_of(ci, h), k_of(ci, h)) for ci, h in units]
    Wl, dmaxs = [], []
    for (ci, h), s in zip(units, qk):
        b_all, igc, bT, igT = gates[ci]
        D = jnp.where(incl, b_all[:, H + h:H + h + 1] + (igT[h:h + 1, :] - bT[H + h:H + h + 1, :]), NEG_BIG)
        dmax = jnp.max(D, axis=-1, keepdims=True)
        dmaxs.append(dmax)
        Wl.append(jnp.exp(D - dmax) * s)
    for (ci, h), w in zip(units, Wl):
        hl_ref[rows_of(ci), h * ML_DV:(h + 1) * ML_DV] = _dot(w, v_of(ci, h))
    wks = []
    for (ci, h), dmax in zip(units, dmaxs):
        b_all, igc, _, _ = gates[ci]
        b_c = b_all[:, H + h:H + h + 1]
        wks.append(jnp.exp(b_c[L - 1:L, :] - b_c + igc[:, h:h + 1] - dmax[L - 1:L, :]) * k_of(ci, h))
    for (ci, h), wk in zip(units, wks):
        kv_ref[ci, h] = _dot_tn(wk, v_of(ci, h))
        ks_ref[ci, h:h + 1, :] = jnp.sum(wk, axis=0, keepdims=True)
    for ci in range(nch):
        stats = jnp.zeros((L, LANES), F32)
        for h in range(H):
            u = ci * H + h
            stats = jnp.where(lane == h, gates[ci][0][:, H + h:H + h + 1],
                              jnp.where(lane == H + h, dmaxs[u],
                                        jnp.where(lane == 2 * H + h, jnp.sum(Wl[u], axis=-1, keepdims=True), stats)))
        st_ref[rows_of(ci), :] = stats


def _mlstm_scan_body(L, cb, NS, q_ref, hl_ref, st_ref, kv_ref, ks_ref, op_ref, C0_ref, n0_ref, m0_ref, on_ref,
                     o_ref, Cout_ref, nout_ref, mout_ref, C_scr, n_scr, m_scr):
    c = pl.program_id(1)

    @pl.when(c == 0)
    def _():
        C_scr[...] = C0_ref[...]
        n_scr[...] = n0_ref[...]
        m_scr[...] = m0_ref[...]

    onorm = on_ref[...]
    H = ML_HEADS
    units = [(ci, h) for ci in range(cb) for h in range(H)]
    rows_of = lambda ci: slice(ci * L, (ci + 1) * L)
    sts = [st_ref[rows_of(ci), :] for ci in range(cb)]
    pre = {}
    for h in range(H):
        m_prev = m_scr[h:h + 1, 0:1]
        n_prev = n_scr[h:h + 1, :]
        C = C_scr[h]
        for ci in range(cb):
            b_c = sts[ci][:, h:h + 1]
            dmax = sts[ci][:, H + h:H + h + 1]
            mt = jnp.maximum(b_c + m_prev, dmax)
            pre[(ci, h)] = (C, n_prev, m_prev, mt)
            mL = mt[L - 1:L, :]
            fL = jnp.exp(b_c[L - 1:L, :] + m_prev - mL)
            gL = jnp.exp(dmax[L - 1:L, :] - mL)
            C = fL * C + gL * kv_ref[ci, h]
            n_prev = fL * n_prev + gL * ks_ref[ci, h:h + 1, :]
            m_prev = mL
        C_scr[h] = C
        n_scr[h:h + 1, :] = n_prev
        m_scr[h:h + 1, :] = jnp.broadcast_to(m_prev, (1, LANES))

    qs = {u: q_ref[rows_of(u[0]), u[1] * ML_DQK:(u[1] + 1) * ML_DQK].astype(F32) for u in units}
    qC = {u: _dot(qs[u], pre[u][0]) for u in units}
    hs = {}
    for u in units:
        ci, h = u
        _, n_prev, m_prev, mt = pre[u]
        b_c = sts[ci][:, h:h + 1]
        dmax = sts[ci][:, H + h:H + h + 1]
        sw = sts[ci][:, 2 * H + h:2 * H + h + 1]
        e_i = jnp.exp(b_c + m_prev - mt)
        e_l = jnp.exp(dmax - mt)
        num = e_i * qC[u] + e_l * hl_ref[rows_of(ci), h * ML_DV:(h + 1) * ML_DV]
        den = e_i * jnp.sum(qs[u] * n_prev, axis=-1, keepdims=True) + e_l * sw
        hs[u] = num / jnp.maximum(jnp.abs(den), jnp.exp(-mt))
    for u in units:
        ci, h = u
        og = op_ref[rows_of(ci), h * ML_DV:(h + 1) * ML_DV].astype(F32)
        o_ref[rows_of(ci), h * ML_DV:(h + 1) * ML_DV] = (_sigmoid(og) * _rms(hs[u], onorm)).astype(BF16)

    @pl.when(c == NS - 1)
    def _():
        Cout_ref[...] = C_scr[...]
        nout_ref[...] = n_scr[...]
        mout_ref[...] = m_scr[...]


def _mlstm_mixer(main, small, C0, n0, m0, bias, onorm, nch=4, cb=4):
    B, T, _ = main.shape
    L = min(ML_CHUNK, T)
    assert T % L == 0
    NC = T // L
    nch = min(nch, NC)
    cb = min(cb, NC)
    assert NC % nch == 0 and NC % cb == 0
    LA = L * nch
    H = ML_HEADS
    hl, st, kv, ks = pl.pallas_call(
        functools.partial(_mlstm_prep_body, L, nch),
        grid=(B, NC // nch),
        in_specs=[pl.BlockSpec((None, LA, 1024), lambda b, c: (b, c, 0)),
                  pl.BlockSpec((None, LA, 1024), lambda b, c: (b, c, 1)),
                  pl.BlockSpec((None, LA, LANES), lambda b, c: (b, c, 0)),
                  pl.BlockSpec((1, LANES), lambda b, c: (0, 0))],
        out_specs=[pl.BlockSpec((None, LA, 1024), lambda b, c: (b, c, 0)),
                   pl.BlockSpec((None, LA, LANES), lambda b, c: (b, c, 0)),
                   pl.BlockSpec((None, nch, H, ML_DQK, ML_DV), lambda b, c: (b, c, 0, 0, 0)),
                   pl.BlockSpec((None, nch, H, ML_DQK), lambda b, c: (b, c, 0, 0))],
        out_shape=[jax.ShapeDtypeStruct((B, T, 1024), F32),
                   jax.ShapeDtypeStruct((B, T, LANES), F32),
                   jax.ShapeDtypeStruct((B, NC, H, ML_DQK, ML_DV), F32),
                   jax.ShapeDtypeStruct((B, NC, H, ML_DQK), F32)],
        compiler_params=_cparams(("parallel", "parallel")),
        name="mlstm_prep",
    )(main, main, small, bias)
    NS = NC // cb
    m0b = jnp.broadcast_to(m0[:, :, None], (B, H, LANES))
    st_C = pl.BlockSpec((None, H, ML_DQK, ML_DV), lambda b, c: (b, 0, 0, 0))
    st_n = pl.BlockSpec((None, H, ML_DQK), lambda b, c: (b, 0, 0))
    st_m = pl.BlockSpec((None, H, LANES), lambda b, c: (b, 0, 0))
    return pl.pallas_call(
        functools.partial(_mlstm_scan_body, L, cb, NS),
        grid=(B, NS),
        in_specs=[pl.BlockSpec((None, cb * L, H * ML_DQK), lambda b, c: (b, c, 0)),
                  pl.BlockSpec((None, cb * L, 1024), lambda b, c: (b, c, 0)),
                  pl.BlockSpec((None, cb * L, LANES), lambda b, c: (b, c, 0)),
                  pl.BlockSpec((None, cb, H, ML_DQK, ML_DV), lambda b, c: (b, c, 0, 0, 0)),
                  pl.BlockSpec((None, cb, H, ML_DQK), lambda b, c: (b, c, 0, 0)),
                  pl.BlockSpec((None, cb * L, 1024), lambda b, c: (b, c, 2)),
                  st_C, st_n, st_m,
                  pl.BlockSpec((1, ML_DV), lambda b, c: (0, 0))],
        out_specs=[pl.BlockSpec((None, cb * L, 1024), lambda b, c: (b, c, 0)), st_C, st_n, st_m],
        out_shape=[jax.ShapeDtypeStruct((B, T, 1024), BF16),
                   jax.ShapeDtypeStruct((B, H, ML_DQK, ML_DV), F32),
                   jax.ShapeDtypeStruct((B, H, ML_DQK), F32),
                   jax.ShapeDtypeStruct((B, H, LANES), F32)],
        scratch_shapes=[pltpu.VMEM((H, ML_DQK, ML_DV), F32),
                        pltpu.VMEM((H, ML_DQK), F32),
                        pltpu.VMEM((H, LANES), F32)],
        compiler_params=_cparams(("parallel", "arbitrary")),
        name="mlstm_scan",
    )(main, hl, st, kv, ks, main, C0, n0, m0b, onorm.reshape(1, ML_DV))


FOX_BIAS_LANE = FOX_DH


def _fox_prep_body(q_ref, k_ref, v_ref, small_ref, bf_ref, qn_ref, kn_ref,
                   qh_ref, k32_ref, v32_ref, lf_ref):
    r = lax.broadcasted_iota(jnp.int32, (LANES, LANES), 0) // FOX_DH
    cc = lax.broadcasted_iota(jnp.int32, (LANES, LANES), 1) // FOX_DH
    avg = jnp.where(r == cc, 1.0 / FOX_DH, 0.0).astype(BF16)

    def head_rms(x, gain):
        parts = []
        for t in range(x.shape[1] // LANES):
            xt = x[:, t * LANES:(t + 1) * LANES]
            hi, lo = _split2(xt * xt)
            ms = (jnp.dot(hi, avg, preferred_element_type=F32)
                  + jnp.dot(lo, avg, preferred_element_type=F32))
            parts.append(xt * lax.rsqrt(ms + EPS))
        return jnp.concatenate(parts, axis=-1) * gain

    qn = head_rms(q_ref[...].astype(F32), qn_ref[...]) * (LOG2E * FOX_DH ** -0.5)
    lane = lax.broadcasted_iota(jnp.int32, (1, LANES), 1)
    ones3 = jnp.where(lane < FOX_BIAS_LANE + 3, 1.0, 0.0)
    for h in range(FOX_HEADS):
        xt = qn[:, (h // 2) * LANES:(h // 2 + 1) * LANES]
        if h % 2:
            xt = pltpu.roll(xt, FOX_DH, axis=1)
        qh_ref[h] = jnp.where(lane < FOX_DH, xt, ones3).astype(BF16)
    k32_ref[...] = head_rms(k_ref[...].astype(F32), kn_ref[...])
    v32_ref[...] = v_ref[...].astype(F32)
    lf_ref[...] = jax.nn.log_sigmoid(small_ref[...] + bf_ref[...])


def _fox_prep(main, small, bf_row, qn_row, kn_row, tm=512):
    B, T, _ = main.shape
    tm = min(tm, T)
    assert T % tm == 0
    blk = lambda j: pl.BlockSpec((None, tm, 1024), lambda b, i: (b, i, j))
    row = lambda n: pl.BlockSpec((1, n), lambda b, i: (0, 0))
    o1024 = pl.BlockSpec((None, tm, 1024), lambda b, i: (b, i, 0))
    return pl.pallas_call(
        _fox_prep_body,
        grid=(B, T // tm),
        in_specs=[blk(0), blk(1), blk(2),
                  pl.BlockSpec((None, tm, LANES), lambda b, i: (b, i, 0)),
                  row(LANES), row(1024), row(1024)],
        out_specs=[pl.BlockSpec((None, FOX_HEADS, tm, LANES), lambda b, i: (b, 0, i, 0)),
                   o1024, o1024,
                   pl.BlockSpec((None, tm, LANES), lambda b, i: (b, i, 0))],
        out_shape=[jax.ShapeDtypeStruct((B, FOX_HEADS, T, LANES), BF16),
                   jax.ShapeDtypeStruct((B, T, 1024), F32),
                   jax.ShapeDtypeStruct((B, T, 1024), F32),
                   jax.ShapeDtypeStruct((B, T, LANES), F32)],
        compiler_params=_cparams(("parallel", "parallel")),
        name="fox_prep",
    )(main, main, main, small, bf_row, qn_row, kn_row)


def _fox_kv_body(tm, n_past, *refs):
    if n_past:
        kp_ref, vp_ref, lp_ref, k_ref, v_ref, lf_ref, kh_ref, vh_ref, carry = refs
    else:
        k_ref, v_ref, lf_ref, kh_ref, vh_ref, carry = refs
    i = pl.program_id(1)

    @pl.when(i == 0)
    def _():
        carry[...] = jnp.zeros_like(carry)

    k = k_ref[...]
    v = v_ref[...]
    lf = lf_ref[...]
    if n_past:
        past = i < n_past
        k = jnp.where(past, kp_ref[...], k)
        v = jnp.where(past, vp_ref[...], v)
        lf = jnp.where(past, lp_ref[...], lf)
    incl, _, _ = _tri_masks(tm)
    tril_b = jnp.where(incl, 1.0, 0.0).astype(BF16)
    F = _cumsum_rows(tril_b, lf) + carry[0:1, :]
    carry[0:1, :] = F[tm - 1:tm, :]
    nb = -LOG2E * F
    b1 = nb.astype(BF16).astype(F32)
    r1 = nb - b1
    b2 = r1.astype(BF16).astype(F32)
    b3 = r1 - b2
    lane = lax.broadcasted_iota(jnp.int32, (1, LANES), 1)
    ones1 = jnp.where(lane == FOX_BIAS_LANE, 1.0, 0.0)
    for h in range(FOX_HEADS):
        kt = k[:, (h // 2) * LANES:(h // 2 + 1) * LANES]
        vt = v[:, (h // 2) * LANES:(h // 2 + 1) * LANES]
        if h % 2:
            kt = pltpu.roll(kt, FOX_DH, axis=1)
            vt = pltpu.roll(vt, FOX_DH, axis=1)
        bias = jnp.where(lane == FOX_BIAS_LANE, b1[:, h:h + 1],
                         jnp.where(lane == FOX_BIAS_LANE + 1, b2[:, h:h + 1],
                                   jnp.where(lane == FOX_BIAS_LANE + 2, b3[:, h:h + 1], 0.0)))
        kh_ref[h] = jnp.where(lane < FOX_DH, kt, bias).astype(BF16)
        vh_ref[h] = jnp.where(lane < FOX_DH, vt, ones1).astype(BF16)


def _fox_kv(k_new, v_new, lf_new, past=None, tm=512):
    B, T, _ = k_new.shape
    if past is None:
        tm = min(tm, T)
        assert T % tm == 0
        n_past, n_new = 0, T // tm
        args, specs = [], []
    else:
        k_past, v_past, lf_past = past
        P = k_past.shape[1]
        tm = min(tm, P)
        assert P % tm == 0 and T <= tm
        n_past, n_new = P // tm, 1
        pad = lambda a: jnp.pad(a, ((0, 0), (0, tm - T), (0, 0)))
        k_new, v_new, lf_new = pad(k_new), pad(v_new), pad(lf_new)
        clamp = lambda b, i: (b, jnp.minimum(i, n_past - 1), 0)
        args = [k_past, v_past, lf_past]
        specs = [pl.BlockSpec((None, tm, 1024), clamp), pl.BlockSpec((None, tm, 1024), clamp),
                 pl.BlockSpec((None, tm, LANES), clamp)]
    new_idx = lambda b, i: (b, jnp.maximum(i - n_past, 0), 0)
    N = (n_past + n_new) * tm
    hm = pl.BlockSpec((None, FOX_HEADS, tm, LANES), lambda b, i: (b, 0, i, 0))
    return pl.pallas_call(
        functools.partial(_fox_kv_body, tm, n_past),
        grid=(B, n_past + n_new),
        in_specs=specs + [pl.BlockSpec((None, tm, 1024), new_idx),
                          pl.BlockSpec((None, tm, 1024), new_idx),
                          pl.BlockSpec((None, tm, LANES), new_idx)],
        out_specs=[hm, hm],
        out_shape=[jax.ShapeDtypeStruct((B, FOX_HEADS, N, LANES), BF16),
                   jax.ShapeDtypeStruct((B, FOX_HEADS, N, LANES), BF16)],
        scratch_shapes=[pltpu.VMEM((8, LANES), F32)],
        compiler_params=_cparams(("parallel", "arbitrary")),
        name="fox_kv",
    )(*args, k_new, v_new, lf_new)


def _flash_body(P, tq, tk, rq, qi_tab, kj_tab, q_ref, k_ref, v_ref, gate_ref, o_ref, m_scr, acc_scr):
    s_idx = pl.program_id(2)
    qi = qi_tab[s_idx]
    kj = kj_tab[s_idx]
    last_kj = (P + (qi + 1) * tq - 1) // tk
    aligned = tq == tk and P % tk == 0

    @pl.when(kj == 0)
    def _():
        m_scr[...] = jnp.full(m_scr.shape, NEG_BIG, F32)
        acc_scr[...] = jnp.zeros_like(acc_scr)

    def step(masked):
        units = [(hh, r) for r in range(tq // rq) for hh in range(2)]

        def logits(u):
            hh, r = u
            rows = slice(r * rq, (r + 1) * rq)
            nk = min(tk, (r + 1) * rq) if (masked and aligned) else tk
            s = lax.dot_general(q_ref[hh, rows, :], k_ref[hh, 0:nk, :], (((1,), (1,)), ((), ())),
                                preferred_element_type=F32)
            if masked:
                qpos = P + qi * tq + r * rq + lax.broadcasted_iota(jnp.int32, (rq, nk), 0)
                kpos = kj * tk + lax.broadcasted_iota(jnp.int32, (rq, nk), 1)
                s = jnp.where(kpos <= qpos, s, NEG_BIG)
            return s, nk

        def probs(u, s):
            hh, r = u
            rows = slice(r * rq, (r + 1) * rq)
            m_prev = m_scr[hh, rows, :][:, 0:1]
            m_new = jnp.maximum(m_prev, jnp.max(s, axis=-1, keepdims=True))
            m_scr[hh, rows, :] = jnp.broadcast_to(m_new, (rq, LANES))
            return jnp.exp2(m_prev - m_new), jnp.exp2(s - m_new).astype(BF16)

        def accumulate(u, alpha, p, nk):
            hh, r = u
            rows = slice(r * rq, (r + 1) * rq)
            acc_scr[hh, rows, :] = alpha * acc_scr[hh, rows, :] + jnp.dot(
                p, v_ref[hh, 0:nk, :], preferred_element_type=F32)

        n = len(units)
        pend_s = {0: logits(units[0])}
        pend_p = {}
        for i in range(n):
            if i + 1 < n:
                pend_s[i + 1] = logits(units[i + 1])
            s, nk = pend_s.pop(i)
            pend_p[i] = probs(units[i], s) + (nk,)
            if i >= 1:
                accumulate(units[i - 1], *pend_p.pop(i - 1))
        accumulate(units[n - 1], *pend_p.pop(n - 1))

    needs_mask = (kj + 1) * tk - 1 > P + qi * tq

    @pl.when(needs_mask)
    def _():
        step(True)

    @pl.when(jnp.logical_not(needs_mask))
    def _():
        step(False)

    @pl.when(kj == last_kj)
    def _():
        lane = lax.broadcasted_iota(jnp.int32, (1, LANES), 1)
        a0 = acc_scr[0]
        a1 = acc_scr[1]
        o0 = a0 * (1.0 / a0[:, FOX_BIAS_LANE:FOX_BIAS_LANE + 1])
        o1 = a1 * (1.0 / a1[:, FOX_BIAS_LANE:FOX_BIAS_LANE + 1])
        o = jnp.where(lane < FOX_DH, o0, pltpu.roll(o1, FOX_DH, axis=1))
        g = gate_ref[...].astype(F32)
        o_ref[...] = (o * _sigmoid(g)).astype(BF16)


def _fox_flash(qh, kh, vh, gate_src, P, tq, tk, rq):
    B, _, T, _ = qh.shape
    N = kh.shape[2]
    assert T % tq == 0 and N % tk == 0 and tq % rq == 0
    nq = T // tq
    pairs = [(qi, kj) for qi in range(nq) for kj in range((P + (qi + 1) * tq - 1) // tk + 1)]
    qi_tab = jnp.asarray(np.array([p[0] for p in pairs], np.int32))
    kj_tab = jnp.asarray(np.array([p[1] for p in pairs], np.int32))
    HP = FOX_HEADS // 2
    gate_blk0 = 3 * (1024 // LANES)
    grid_spec = pltpu.PrefetchScalarGridSpec(
        num_scalar_prefetch=2,
        grid=(B, HP, len(pairs)),
        in_specs=[pl.BlockSpec((None, 2, tq, LANES), lambda b, p, s, qt, kt: (b, p, qt[s], 0)),
                  pl.BlockSpec((None, 2, tk, LANES), lambda b, p, s, qt, kt: (b, p, kt[s], 0)),
                  pl.BlockSpec((None, 2, tk, LANES), lambda b, p, s, qt, kt: (b, p, kt[s], 0)),
                  pl.BlockSpec((None, tq, LANES), lambda b, p, s, qt, kt: (b, qt[s], gate_blk0 + p))],
        out_specs=pl.BlockSpec((None, tq, LANES), lambda b, p, s, qt, kt: (b, qt[s], p)),
        scratch_shapes=[pltpu.VMEM((2, tq, LANES), F32),
                        pltpu.VMEM((2, tq, LANES), F32)],
    )
    return pl.pallas_call(
        functools.partial(_flash_body, P, tq, tk, rq),
        grid_spec=grid_spec,
        out_shape=jax.ShapeDtypeStruct((B, T, 1024), BF16),
        compiler_params=_cparams(("parallel", "parallel", "arbitrary")),
        name="fox_flash",
    )(qi_tab, kj_tab, qh, kh, vh, gate_src)


def _pad_lanes(a, offset=0):
    n = a.shape[-1]
    pads = [(0, 0)] * (a.ndim - 1) + [(offset, LANES - n - offset)]
    return jnp.pad(a, pads)


def _trunk(x, gdn_conv, gdn_S, ml_C, ml_n, ml_m, fox_k, fox_v, fox_lf, mem_k, mem_v, W):
    B, T, D = x.shape
    M = B * T
    x2 = x.reshape(M, D)
    conv_o, S_o, C_o, n_o, m_o, k_o, v_o, lf_o = [], [], [], [], [], [], [], []
    for layer in range(DEPTH):
        kind, j = layer % N_MIXERS, layer // N_MIXERS
        gains = W["norm_gains"][layer]
        if kind == 0:
            main, small = _prenorm_mm(x2, gains[0], W["gdn_w_in"], j, GDN_MAIN, W["gdn_w_small"], out_dtype=F32)
            o, buf, S = _gdn_mixer(main.reshape(B, T, GDN_MAIN), small.reshape(B, T, LANES),
                                   gdn_conv[j], gdn_S[j], W["gdn_wc_t"][j], W["gdn_par"][j],
                                   W["gdn_onorm"][j])
            conv_o.append(buf)
            S_o.append(S)
            w_out = W["gdn_w_out"]
        elif kind == 1:
            main, small = _prenorm_mm(x2, gains[0], W["ml_w_in"], j, ML_MAIN, W["ml_w_small"])
            o, C, n, m = _mlstm_mixer(main.reshape(B, T, ML_MAIN), small.reshape(B, T, LANES),
                                      ml_C[j], ml_n[j], ml_m[j], W["ml_bias"][j], W["ml_onorm"][j])
            C_o.append(C)
            n_o.append(n)
            m_o.append(m[:, :, 0])
            w_out = W["ml_w_out"]
        else:
            main, small = _prenorm_mm(x2, gains[0], W["fox_w_in"], j, FOX_MAIN, W["fox_w_small"])
            main3 = main.reshape(B, T, FOX_MAIN)
            qh, k32, v32, lfp = _fox_prep(main3, small.reshape(B, T, LANES), W["fox_bf"][j],
                                          W["fox_qn"][j], W["fox_kn"][j])
            P = fox_k[j].shape[1]
            if P == 0:
                kh, vh = _fox_kv(k32, v32, lfp)
                tq, tk, rq = min(1024, T), min(1024, T), 256
            else:
                kh, vh = _fox_kv(k32, v32, lfp, past=(fox_k[j].reshape(B, P, 1024), fox_v[j].reshape(B, P, 1024),
                                                      _pad_lanes(fox_lf[j])), tm=256)
                tq, tk, rq = T, kh.shape[2], T
            o = _fox_flash(qh, kh, vh, main3, P, tq, tk, rq)
            k_o.append(k32.reshape(B, T, FOX_HEADS, FOX_DH))
            v_o.append(v32.reshape(B, T, FOX_HEADS, FOX_DH))
            lf_o.append(lfp[:, :, :FOX_HEADS])
            w_out = W["fox_w_out"]
        x2 = _memattn(o, w_out, j, gains[1], x2.reshape(B, T, D), gains[2], W["x_w_q"],
                      mem_k[layer], mem_v[layer], W["x_w_o"], layer, gains[3]).reshape(M, D)
        x2 = _ffn(x2, gains[4], W["ffn_w_gu"], W["ffn_w_down"], layer, gains[5])
    return (x2.reshape(B, T, D), jnp.stack(conv_o), jnp.stack(S_o), jnp.stack(C_o), jnp.stack(n_o),
            jnp.stack(m_o), jnp.stack(k_o), jnp.stack(v_o), jnp.stack(lf_o))


def kernel(x_prompt, x_sample, state_gdn_conv, state_gdn_S, state_mlstm_C, state_mlstm_n, state_mlstm_m, cache_fox_k, cache_fox_v, cache_fox_logf, cache_mem_k, cache_mem_v, mem_prompt, norm_gains, gdn_w_in, gdn_w_conv, gdn_a_log, gdn_dt_bias, gdn_onorm, gdn_w_out, ml_w_in, ml_b_i, ml_b_f, ml_onorm, ml_w_out, fox_w_in, fox_b_f, fox_qk_norm, fox_w_out, x_w_q, x_w_kv, x_w_o, ffn_w_gu, ffn_w_down):
    B = x_prompt.shape[0]
    DB = x_sample.shape[0]
    n_a, n_b, n_c = gdn_w_in.shape[0], ml_w_in.shape[0], fox_w_in.shape[0]
    W = {
        "norm_gains": norm_gains,
        "gdn_w_in": gdn_w_in.astype(BF16),
        "gdn_w_small": _pad_lanes(gdn_w_in[:, :, GDN_MAIN:]).astype(BF16),
        "gdn_wc_t": jnp.transpose(gdn_w_conv, (0, 2, 1)),
        "gdn_par": jnp.stack([_pad_lanes(gdn_a_log, GDN_HEADS), _pad_lanes(gdn_dt_bias, GDN_HEADS)], axis=1),
        "gdn_onorm": gdn_onorm,
        "gdn_w_out": gdn_w_out.astype(BF16),
        "ml_w_in": ml_w_in.astype(BF16),
        "ml_w_small": _pad_lanes(ml_w_in[:, :, ML_MAIN:]).astype(BF16),
        "ml_bias": _pad_lanes(jnp.concatenate([ml_b_i, ml_b_f], axis=-1))[:, None, :],
        "ml_onorm": ml_onorm,
        "ml_w_out": ml_w_out.astype(BF16),
        "fox_w_in": fox_w_in.astype(BF16),
        "fox_w_small": _pad_lanes(fox_w_in[:, :, FOX_MAIN:]).astype(BF16),
        "fox_bf": _pad_lanes(fox_b_f)[:, None, :],
        "fox_qn": jnp.tile(fox_qk_norm[:, 0, :], (1, FOX_HEADS))[:, None, :],
        "fox_kn": jnp.tile(fox_qk_norm[:, 1, :], (1, FOX_HEADS))[:, None, :],
        "fox_w_out": fox_w_out.astype(BF16),
        "x_w_q": x_w_q.astype(BF16),
        "x_w_o": x_w_o.astype(BF16),
        "ffn_w_gu": ffn_w_gu.astype(BF16),
        "ffn_w_down": ffn_w_down.astype(BF16),
    }

    mem2 = mem_prompt.reshape(B * N_MEM, D_MODEL)
    x_w_kv_b = x_w_kv.astype(BF16)
    kv = [_prenorm_mm(mem2, norm_gains[l, 6], x_w_kv_b, l, 2 * D_MODEL, out_dtype=F32).reshape(B, N_MEM, 2 * D_MODEL)
          for l in range(DEPTH)]
    p_mem_k = jnp.stack([a[:, :, :D_MODEL].reshape(B, N_MEM, X_HEADS, X_DH) for a in kv])
    p_mem_v = jnp.stack([a[:, :, D_MODEL:].reshape(B, N_MEM, X_HEADS, X_DH) for a in kv])

    cmk = cache_mem_k.reshape(DEPTH * DB, N_MEM, D_MODEL)
    cmv = cache_mem_v.reshape(DEPTH * DB, N_MEM, D_MODEL)
    zeros = lambda *s: jnp.zeros(s, F32)
    prompt = _trunk(
        x_prompt,
        zeros(n_a, B, CONV_W - 1, GDN_QKV), zeros(n_a, B, GDN_HEADS, GDN_DK, GDN_DV),
        zeros(n_b, B, ML_HEADS, ML_DQK, ML_DV), zeros(n_b, B, ML_HEADS, ML_DQK), zeros(n_b, B, ML_HEADS),
        zeros(n_c, B, 0, FOX_HEADS, FOX_DH), zeros(n_c, B, 0, FOX_HEADS, FOX_DH), zeros(n_c, B, 0, FOX_HEADS),
        [(a, 0, 0) for a in kv], [(a, 0, 1) for a in kv], W)
    sample = _trunk(
        x_sample, state_gdn_conv, state_gdn_S, state_mlstm_C, state_mlstm_n, state_mlstm_m,
        cache_fox_k, cache_fox_v, cache_fox_logf,
        [(cmk, l * DB, 0) for l in range(DEPTH)], [(cmv, l * DB, 0) for l in range(DEPTH)], W)
    (y_p, p_conv, p_S, p_C, p_n, p_m, p_k, p_v, p_lf) = prompt
    (y_s, s_conv, s_S, s_C, s_n, s_m, s_k, s_v, s_lf) = sample
    return (y_p, y_s, p_conv, p_S, p_C, p_n, p_m, p_k, p_v, p_lf, p_mem_k, p_mem_v,
            s_conv, s_S, s_C, s_n, s_m, s_k, s_v, s_lf)
```

```python
import functools

import numpy as np
import jax
import jax.numpy as jnp
from jax import lax
from jax.experimental import pallas as pl
from jax.experimental.pallas import tpu as pltpu

F32 = jnp.float32
BF16 = jnp.bfloat16

D_MODEL = 1024
DEPTH = 4
N_MIXERS = 3
EPS = 1e-6
CONV_W = 4

GDN_HEADS = 8
GDN_DK = 128
GDN_DV = 128
GDN_QKV = 3072
GDN_MAIN = 4096

ML_HEADS = 4
ML_DQK = 128
ML_DV = 256
ML_MAIN = 3072
GATE_CAP = 15.0

FOX_HEADS = 16
FOX_DH = 64
FOX_MAIN = 4096

N_MEM = 256
X_HEADS = 4
X_DH = 256
D_FF = 2816

LANES = 128
NEG_BIG = -1e30
LOG2E = 1.4426950408889634

VMEM_LIMIT = 48 * 1024 * 1024


def _wspec(li, blk, imap):
    return pl.BlockSpec((None,) + blk, lambda *g: (li,) + imap(*g))


def _cparams(sem):
    return pltpu.CompilerParams(dimension_semantics=sem, vmem_limit_bytes=VMEM_LIMIT)


def _rms(x, g):
    return x * lax.rsqrt(jnp.mean(x * x, axis=-1, keepdims=True) + EPS) * g


def _sigmoid(x):
    return 1.0 / (1.0 + jnp.exp(-x))


def _dot(a, b):
    return jnp.dot(a.astype(BF16), b.astype(BF16), preferred_element_type=F32)


def _dot_nt(a, b):
    return lax.dot_general(a.astype(BF16), b.astype(BF16), (((1,), (1,)), ((), ())),
                           preferred_element_type=F32)


def _dot_tn(a, b):
    return lax.dot_general(a.astype(BF16), b.astype(BF16), (((0,), (0,)), ((), ())),
                           preferred_element_type=F32)


def _split2(x):
    hi = x.astype(BF16)
    lo = (x - hi.astype(F32)).astype(BF16)
    return hi, lo


def _dot_x3(a, b):
    ah, al = _split2(a)
    bh, bl = _split2(b)
    d = lambda u, v: jnp.dot(u, v, preferred_element_type=F32)
    return d(ah, bh) + (d(ah, bl) + d(al, bh))


def _cumsum_rows(tril_b, x):
    h = x.astype(BF16)
    r = x - h.astype(F32)
    m = r.astype(BF16)
    l = (r - m.astype(F32)).astype(BF16)
    d = lambda v: jnp.dot(tril_b, v, preferred_element_type=F32)
    return d(h) + (d(m) + d(l))


def _tri_masks(L):
    row = lax.broadcasted_iota(jnp.int32, (L, L), 0)
    col = lax.broadcasted_iota(jnp.int32, (L, L), 1)
    return row >= col, row > col, row == col


def _prenorm_mm_body(has_small, *refs):
    if has_small:
        x_ref, g_ref, w_ref, ws_ref, o_ref, os_ref, xn_ref = refs
    else:
        x_ref, g_ref, w_ref, o_ref, xn_ref = refs

    @pl.when(pl.program_id(1) == 0)
    def _():
        xb = _rms(x_ref[...], g_ref[...]).astype(BF16)
        xn_ref[...] = xb
        if has_small:
            os_ref[...] = jnp.dot(xb, ws_ref[...], preferred_element_type=F32)

    o_ref[...] = jnp.dot(xn_ref[...], w_ref[...], preferred_element_type=F32).astype(o_ref.dtype)


def _prenorm_mm(x, gain, w_main, li, n_out, w_small=None, out_dtype=BF16, tm=1024, tn=1024):
    M, D = x.shape
    N = n_out
    tm = min(tm, M)
    tn = min(tn, N)
    assert M % tm == 0 and N % tn == 0
    has_small = w_small is not None
    in_specs = [pl.BlockSpec((tm, D), lambda i, j: (i, 0)),
                pl.BlockSpec((1, D), lambda i, j: (0, 0)),
                _wspec(li, (D, tn), lambda i, j: (0, j))]
    out_specs = [pl.BlockSpec((tm, tn), lambda i, j: (i, j))]
    out_shape = [jax.ShapeDtypeStruct((M, N), out_dtype)]
    args = [x, gain.reshape(1, D), w_main]
    if has_small:
        in_specs.append(_wspec(li, (D, LANES), lambda i, j: (0, 0)))
        out_specs.append(pl.BlockSpec((tm, LANES), lambda i, j: (i, 0)))
        out_shape.append(jax.ShapeDtypeStruct((M, LANES), F32))
        args.append(w_small)
    res = pl.pallas_call(
        functools.partial(_prenorm_mm_body, has_small),
        grid=(M // tm, N // tn),
        in_specs=in_specs, out_specs=out_specs, out_shape=out_shape,
        scratch_shapes=[pltpu.VMEM((tm, D), BF16)],
        compiler_params=_cparams(("parallel", "arbitrary")),
        name="prenorm_mm",
    )(*args)
    return res if has_small else res[0]


def _gdn_proj_body(nq, tpb, rs, x_ref, xp_ref, g_ref, w_ref, ws_ref, wc_ref, buf_ref,
                   o_ref, os_ref, nb_ref, xn_ref, xpn_ref):
    i = pl.program_id(0)
    j = pl.program_id(1)
    tm = x_ref.shape[0]

    @pl.when(j == 0)
    def _():
        xb = _rms(x_ref[...], g_ref[...]).astype(BF16)
        xn_ref[...] = xb
        os_ref[...] = jnp.dot(xb, ws_ref[...], preferred_element_type=F32)
        xpn_ref[...] = _rms(xp_ref[...], g_ref[...]).astype(BF16)

    @pl.when(j >= nq)
    def _():
        o_ref[...] = jnp.dot(xn_ref[...], w_ref[...], preferred_element_type=F32)

    @pl.when(j < nq)
    def _():
        w = w_ref[...]
        before = jnp.dot(xpn_ref[...], w, preferred_element_type=F32)[8:16, :]
        prev = jnp.where(i % tpb == 0, buf_ref[...], before)
        nsub = tm // rs
        raws = {0: jnp.dot(xn_ref[0:rs, :], w, preferred_element_type=F32)}
        for r in range(nsub):
            if r + 1 < nsub:
                raws[r + 1] = jnp.dot(xn_ref[(r + 1) * rs:(r + 2) * rs, :], w, preferred_element_type=F32)
            raw = raws.pop(r)
            xb = jnp.concatenate([prev, raw], axis=0)
            z = xb * wc_ref[0:1, :]
            for t in range(1, CONV_W):
                z = pltpu.roll(z, 1, axis=0) + xb * wc_ref[t:t + 1, :]
            y = z[8:, :]
            o_ref[r * rs:(r + 1) * rs, :] = y * _sigmoid(y)
            prev = raw[rs - 8:rs, :]
        nb_ref[...] = prev


def _gdn_proj(x, gain, w_main, li, w_small, wc_t, conv_buf, T, tm=1024, tn=1024, rs=256):
    M, D = x.shape
    B = M // T
    tm = min(tm, T)
    rs = min(rs, tm)
    assert T % tm == 0 and tm % rs == 0 and GDN_QKV % tn == 0 and GDN_MAIN % tn == 0
    nq = GDN_QKV // tn
    tpb = T // tm
    buf8 = jnp.pad(conv_buf, ((0, 0), (8 - (CONV_W - 1), 0), (0, 0)))
    qcol = lambda j: jnp.minimum(j, nq - 1)
    main, small, tails = pl.pallas_call(
        functools.partial(_gdn_proj_body, nq, tpb, rs),
        grid=(M // tm, GDN_MAIN // tn),
        in_specs=[pl.BlockSpec((tm, D), lambda i, j: (i, 0)),
                  pl.BlockSpec((16, D), lambda i, j: (jnp.maximum(i * (tm // 16) - 1, 0), 0)),
                  pl.BlockSpec((1, D), lambda i, j: (0, 0)),
                  _wspec(li, (D, tn), lambda i, j: (0, j)),
                  _wspec(li, (D, LANES), lambda i, j: (0, 0)),
                  pl.BlockSpec((CONV_W, tn), lambda i, j: (0, qcol(j))),
                  pl.BlockSpec((None, 8, tn), lambda i, j: (i // tpb, 0, qcol(j)))],
        out_specs=[pl.BlockSpec((tm, tn), lambda i, j: (i, j)),
                   pl.BlockSpec((tm, LANES), lambda i, j: (i, 0)),
                   pl.BlockSpec((None, 8, tn), lambda i, j: (i, 0, qcol(j)))],
        out_shape=[jax.ShapeDtypeStruct((M, GDN_MAIN), F32),
                   jax.ShapeDtypeStruct((M, LANES), F32),
                   jax.ShapeDtypeStruct((M // tm, 8, GDN_QKV), F32)],
        scratch_shapes=[pltpu.VMEM((tm, D), BF16), pltpu.VMEM((16, D), BF16)],
        compiler_params=_cparams(("arbitrary", "arbitrary")),
        name="gdn_proj",
    )(x, x, gain.reshape(1, D), w_main, w_small, wc_t, buf8)
    return main, small, tails[tpb - 1::tpb]


def _memattn_body(nb, o_ref, wout_ref, g1_ref, x_ref, g2_ref, wq_ref, mk_ref, mv_ref, wo_ref, g3_ref, y_ref):
    tq, D = x_ref.shape[1], x_ref.shape[2]
    R = nb * tq
    o_in = o_ref[...].reshape(R, o_ref.shape[2])
    x = x_ref[...].reshape(R, D)
    x = x + _rms(jnp.dot(o_in, wout_ref[...], preferred_element_type=F32), g1_ref[...])
    h = _rms(x, g2_ref[...]).astype(BF16)
    q = jnp.dot(h, wq_ref[...], preferred_element_type=F32).astype(BF16)
    rows = []
    for bi in range(nb):
        mk = mk_ref[bi].astype(BF16)
        mv = mv_ref[bi].astype(BF16)
        qb = q[bi * tq:(bi + 1) * tq]
        outs = []
        for hd in range(X_HEADS):
            sl = slice(hd * X_DH, (hd + 1) * X_DH)
            s = _dot_nt(qb[:, sl], mk[:, sl]) * (X_DH ** -0.5)
            p = jnp.exp(s - jnp.max(s, axis=-1, keepdims=True))
            p = p * (1.0 / jnp.sum(p, axis=-1, keepdims=True))
            outs.append(jnp.dot(p.astype(BF16), mv[:, sl], preferred_element_type=F32))
        rows.append(jnp.concatenate(outs, axis=-1).astype(BF16))
    o = rows[0] if nb == 1 else jnp.concatenate(rows, axis=0)
    y = jnp.dot(o, wo_ref[...], preferred_element_type=F32)
    y_ref[...] = (x + _rms(y, g3_ref[...])).reshape(nb, tq, D)


def _memattn(o, w_out, lo, g1, x, g2, wq, mem_k, mem_v, wo, lx, g3, tq=512, rows_per_step=256):
    B, T, D = x.shape
    tq = min(tq, T)
    assert T % tq == 0
    nb = max(1, min(B, rows_per_step // tq)) if tq == T else 1
    assert B % nb == 0
    HD = X_HEADS * X_DH
    K = o.shape[-1]
    (mk_arr, koff, kcol), (mv_arr, voff, vcol) = mem_k, mem_v
    assert koff % nb == 0 and voff % nb == 0
    return pl.pallas_call(
        functools.partial(_memattn_body, nb),
        grid=(B // nb, T // tq),
        in_specs=[pl.BlockSpec((nb, tq, K), lambda b, i: (b, i, 0)),
                  _wspec(lo, (K, D), lambda b, i: (0, 0)),
                  pl.BlockSpec((1, D), lambda b, i: (0, 0)),
                  pl.BlockSpec((nb, tq, D), lambda b, i: (b, i, 0)),
                  pl.BlockSpec((1, D), lambda b, i: (0, 0)),
                  _wspec(lx, (D, HD), lambda b, i: (0, 0)),
                  pl.BlockSpec((nb, N_MEM, HD), lambda b, i: (koff // nb + b, 0, kcol)),
                  pl.BlockSpec((nb, N_MEM, HD), lambda b, i: (voff // nb + b, 0, vcol)),
                  _wspec(lx, (HD, D), lambda b, i: (0, 0)),
                  pl.BlockSpec((1, D), lambda b, i: (0, 0))],
        out_specs=pl.BlockSpec((nb, tq, D), lambda b, i: (b, i, 0)),
        out_shape=jax.ShapeDtypeStruct((B, T, D), F32),
        compiler_params=_cparams(("parallel", "parallel")),
        name="memattn",
    )(o, w_out, g1.reshape(1, D), x, g2.reshape(1, D), wq, mk_arr, mv_arr, wo, g3.reshape(1, D))


def _ffn_body(nf, x_ref, g4_ref, wg_ref, wu_ref, wd_ref, g5_ref, y_ref, xn_ref, acc_ref):
    f = pl.program_id(1)

    @pl.when(f == 0)
    def _():
        xn_ref[...] = _rms(x_ref[...], g4_ref[...]).astype(BF16)
        acc_ref[...] = jnp.zeros_like(acc_ref)

    xn = xn_ref[...]
    g = jnp.dot(xn, wg_ref[...], preferred_element_type=F32)
    u = jnp.dot(xn, wu_ref[...], preferred_element_type=F32)
    a = (g * _sigmoid(g)) * u
    acc_ref[...] += jnp.dot(a.astype(BF16), wd_ref[...], preferred_element_type=F32)

    @pl.when(f == nf - 1)
    def _():
        y_ref[...] = x_ref[...] + _rms(acc_ref[...], g5_ref[...])


def _ffn(x, g4, w_gu, w_down, li, g5, tm=1024, tf=1408):
    M, D = x.shape
    tm = min(tm, M)
    assert M % tm == 0 and D_FF % tf == 0
    nf = D_FF // tf
    return pl.pallas_call(
        functools.partial(_ffn_body, nf),
        grid=(M // tm, nf),
        in_specs=[pl.BlockSpec((tm, D), lambda i, f: (i, 0)),
                  pl.BlockSpec((1, D), lambda i, f: (0, 0)),
                  _wspec(li, (D, tf), lambda i, f: (0, f)),
                  _wspec(li, (D, tf), lambda i, f: (0, nf + f)),
                  _wspec(li, (tf, D), lambda i, f: (f, 0)),
                  pl.BlockSpec((1, D), lambda i, f: (0, 0))],
        out_specs=pl.BlockSpec((tm, D), lambda i, f: (i, 0)),
        out_shape=jax.ShapeDtypeStruct((M, D), F32),
        scratch_shapes=[pltpu.VMEM((tm, D), BF16), pltpu.VMEM((tm, D), F32)],
        compiler_params=_cparams(("parallel", "arbitrary")),
        name="ffn",
    )(x, g4.reshape(1, D), w_gu, w_gu, w_down, g5.reshape(1, D))


GDN_CHUNK = 64
GDN_BASE = 8


def _inverse_masks(L):
    row = lax.broadcasted_iota(jnp.int32, (L, L), 0)
    col = lax.broadcasted_iota(jnp.int32, (L, L), 1)
    base = row // GDN_BASE == col // GDN_BASE
    offs = []
    s = GDN_BASE
    while s < L:
        same_pair = row // (2 * s) == col // (2 * s)
        offs.append(jnp.logical_and(same_pair, (row // s) - (col // s) == 1))
        s *= 2
    return base, row == col, offs


def _unit_lower_inverse(As, masks, mm):
    base, diag, offs = masks
    eye = jnp.where(diag, 1.0, 0.0)
    Ps = [jnp.where(base, -A, 0.0) for A in As]
    Ts = [eye + P for P in Ps]
    k = 1
    while k < GDN_BASE // 2:
        Ps = [mm(P, P) for P in Ps]
        Ts = [T + mm(T, P) for T, P in zip(Ts, Ps)]
        k *= 2
    for off in offs:
        Ms = [_dot(T, jnp.where(off, A, 0.0)) for T, A in zip(Ts, As)]
        Ts = [T - _dot(M, T) for T, M in zip(Ts, Ms)]
    return Ts


def _gdn_prep_body(L, nch, qkv_ref, small_ref, par_ref,
                   wk_ref, uv_ref, qt_ref, kt_ref, at_ref, egl_ref):
    y = qkv_ref[...]

    small = small_ref[...]
    beta_all = _sigmoid(small)
    g_all = -jnp.exp(par_ref[0:1, :]) * jax.nn.softplus(small + par_ref[1:2, :])
    incl, strict, _ = _tri_masks(L)
    tril_b = jnp.where(incl, 1.0, 0.0).astype(BF16)
    masks = _inverse_masks(L)
    H = GDN_HEADS

    units = []
    for ci in range(nch):
        rows = slice(ci * L, (ci + 1) * L)
        G_all = _cumsum_rows(tril_b, g_all[rows])
        GT = G_all.T
        egl_ref[ci] = jnp.broadcast_to(jnp.exp(G_all[L - 1:L, :]), (8, LANES))
        for h in range(H):
            qh = y[rows, h * GDN_DK:(h + 1) * GDN_DK]
            kh = y[rows, (H + h) * GDN_DK:(H + h + 1) * GDN_DK]
            vh = y[rows, 2 * H * GDN_DK + h * GDN_DV:2 * H * GDN_DK + (h + 1) * GDN_DV]
            qh = qh * lax.rsqrt(jnp.sum(qh * qh, axis=-1, keepdims=True) + EPS) * (GDN_DK ** -0.5)
            kh = kh * lax.rsqrt(jnp.sum(kh * kh, axis=-1, keepdims=True) + EPS)
            bc = beta_all[rows, h:h + 1]
            Gc = G_all[:, H + h:H + h + 1]
            Gr = GT[H + h:H + h + 1, :]
            eG = jnp.exp(Gc)
            dec = jnp.exp(jnp.where(incl, Gc - Gr, NEG_BIG))
            qkk = _dot_nt(jnp.concatenate([qh, kh], axis=0), kh)
            A = jnp.where(strict, bc * qkk[L:] * dec, 0.0)
            rhs = jnp.concatenate([bc * vh, (bc * eG) * kh], axis=-1)
            cols = slice(h * GDN_DV, (h + 1) * GDN_DV)
            qt_ref[rows, cols] = (eG * qh).astype(BF16)
            kt_ref[rows, cols] = (kh * jnp.exp(Gc[L - 1:L, :] - Gc)).astype(BF16)
            at_ref[rows, cols] = jnp.concatenate(
                [qkk[:L] * dec, jnp.zeros((L, GDN_DV - L), F32)], axis=-1).astype(BF16)
            units.append((rows, cols, A, rhs))

    Ts = _unit_lower_inverse([u[2] for u in units], masks, _dot)
    for (rows, cols, _, rhs), T in zip(units, Ts):
        sol = _dot(T, rhs)
        uv_ref[rows, cols] = sol[:, :GDN_DV]
        wk_ref[rows, cols] = sol[:, GDN_DV:].astype(BF16)


def _gdn_scan_body(L, cb, NS, wk_ref, uv_ref, qt_ref, kt_ref, at_ref, egl_ref, gate_ref, S0_ref, on_ref,
                   o_ref, Sout_ref, S_scr):
    c = pl.program_id(1)

    @pl.when(c == 0)
    def _():
        S_scr[...] = S0_ref[...]

    onorm = on_ref[...]
    H = GDN_HEADS
    for ci in range(cb):
        rows = slice(ci * L, (ci + 1) * L)
        egl = egl_ref[ci]
        hs = [slice(h * GDN_DV, (h + 1) * GDN_DV) for h in range(H)]
        Ss = [S_scr[h] for h in range(H)]
        Xs = [jnp.dot(jnp.concatenate([wk_ref[rows, hs[h]], qt_ref[rows, hs[h]]], axis=0),
                      Ss[h].astype(BF16), preferred_element_type=F32) for h in range(H)]
        Us = [(uv_ref[rows, hs[h]] - Xs[h][:L]).astype(BF16) for h in range(H)]
        os = [Xs[h][L:] + jnp.dot(at_ref[rows, h * GDN_DV:h * GDN_DV + L], Us[h],
                                  preferred_element_type=F32) for h in range(H)]
        for h in range(H):
            S_scr[h] = egl[0:1, H + h:H + h + 1] * Ss[h] + lax.dot_general(
                kt_ref[rows, hs[h]], Us[h], (((0,), (0,)), ((), ())), preferred_element_type=F32)
        for h in range(H):
            gh = gate_ref[rows, hs[h]].astype(F32)
            o_ref[rows, hs[h]] = (_rms(os[h], onorm) * (gh * _sigmoid(gh))).astype(BF16)

    @pl.when(c == NS - 1)
    def _():
        Sout_ref[...] = S_scr[...]


def _gdn_mixer(main, small, S0, par, onorm, nch=4, cb=8):
    B, T, _ = main.shape
    L = min(GDN_CHUNK, T)
    assert T % L == 0
    NC = T // L
    nch = min(nch, NC)
    cb = min(cb, NC)
    assert NC % nch == 0 and NC % cb == 0
    LA = L * nch
    tok = lambda n, j: pl.BlockSpec((None, n, 1024), lambda b, c: (b, c, j))
    sds = lambda dt: jax.ShapeDtypeStruct((B, T, 1024), dt)
    wk, uv, qt, kt, at, egl = pl.pallas_call(
        functools.partial(_gdn_prep_body, L, nch),
        grid=(B, NC // nch),
        in_specs=[pl.BlockSpec((None, LA, GDN_QKV), lambda b, c: (b, c, 0)),
                  pl.BlockSpec((None, LA, LANES), lambda b, c: (b, c, 0)),
                  pl.BlockSpec((2, LANES), lambda b, c: (0, 0))],
        out_specs=[tok(LA, 0), tok(LA, 0), tok(LA, 0), tok(LA, 0), tok(LA, 0),
                   pl.BlockSpec((None, nch, 8, LANES), lambda b, c: (b, c, 0, 0))],
        out_shape=[sds(BF16), sds(F32), sds(BF16), sds(BF16), sds(BF16),
                   jax.ShapeDtypeStruct((B, NC, 8, LANES), F32)],
        compiler_params=_cparams(("parallel", "parallel")),
        name="gdn_prep",
    )(main, small, par)
    NS = NC // cb
    state = pl.BlockSpec((None, GDN_HEADS, GDN_DK, GDN_DV), lambda b, c: (b, 0, 0, 0))
    o, S = pl.pallas_call(
        functools.partial(_gdn_scan_body, L, cb, NS),
        grid=(B, NS),
        in_specs=[tok(cb * L, 0), tok(cb * L, 0), tok(cb * L, 0), tok(cb * L, 0), tok(cb * L, 0),
                  pl.BlockSpec((None, cb, 8, LANES), lambda b, c: (b, c, 0, 0)),
                  tok(cb * L, 3), state,
                  pl.BlockSpec((1, GDN_DV), lambda b, c: (0, 0))],
        out_specs=[tok(cb * L, 0), state],
        out_shape=[sds(BF16), jax.ShapeDtypeStruct((B, GDN_HEADS, GDN_DK, GDN_DV), F32)],
        scratch_shapes=[pltpu.VMEM((GDN_HEADS, GDN_DK, GDN_DV), F32)],
        compiler_params=_cparams(("parallel", "arbitrary")),
        name="gdn_scan",
    )(wk, uv, qt, kt, at, egl, main, S0, onorm.reshape(1, GDN_DV))
    return o, S


ML_CHUNK = 64


def _mlstm_gates(small_ref, bias_ref):
    sc = GATE_CAP * jnp.tanh((small_ref[...] + bias_ref[...]) / GATE_CAP)
    return sc, jax.nn.log_sigmoid(sc)


def _mlstm_prep_body(L, nch, qk_ref, v_ref, small_ref, bias_ref, hl_ref, st_ref, kv_ref, ks_ref):
    ig_all, lf_all = _mlstm_gates(small_ref, bias_ref)
    incl, _, _ = _tri_masks(L)
    tril_b = jnp.where(incl, 1.0, 0.0).astype(BF16)
    lane = lax.broadcasted_iota(jnp.int32, (1, LANES), 1)
    H = ML_HEADS
    units = [(ci, h) for ci in range(nch) for h in range(H)]
    rows_of = lambda ci: slice(ci * L, (ci + 1) * L)
    q_of = lambda ci, h: qk_ref[rows_of(ci), h * ML_DQK:(h + 1) * ML_DQK]
    k_of = lambda ci, h: qk_ref[rows_of(ci), (H + h) * ML_DQK:(H + h + 1) * ML_DQK].astype(F32) * (ML_DQK ** -0.5)
    v_of = lambda ci, h: v_ref[rows_of(ci), h * ML_DV:(h + 1) * ML_DV]

    gates = []
    for ci in range(nch):
        b_all = _cumsum_rows(tril_b, lf_all[rows_of(ci)])
        igc = ig_all[rows_of(ci)]
        gates.append((b_all, igc, b_all.T, igc.T))
    qk = [_dot_nt(q_of(ci, h), k_of(ci, h)) for ci, h in units]
    Wl, dmaxs = [], []
    for (ci, h), s in zip(units, qk):
        b_all, igc, bT, igT = gates[ci]
        D = jnp.where(incl, b_all[:, H + h:H + h + 1] + (igT[h:h + 1, :] - bT[H + h:H + h + 1, :]), NEG_BIG)
        dmax = jnp.max(D, axis=-1, keepdims=True)
        dmaxs.append(dmax)
        Wl.append(jnp.exp(D - dmax) * s)
    for (ci, h), w in zip(units, Wl):
        hl_ref[rows_of(ci), h * ML_DV:(h + 1) * ML_DV] = _dot(w, v_of(ci, h))
    wks = []
    for (ci, h), dmax in zip(units, dmaxs):
        b_all, igc, _, _ = gates[ci]
        b_c = b_all[:, H + h:H + h + 1]
        wks.append(jnp.exp(b_c[L - 1:L, :] - b_c + igc[:, h:h + 1] - dmax[L - 1:L, :]) * k_of(ci, h))
    for (ci, h), wk in zip(units, wks):
        kv_ref[ci, h] = _dot_tn(wk, v_of(ci, h))
        ks_ref[ci, h:h + 1, :] = jnp.sum(wk, axis=0, keepdims=True)
    for ci in range(nch):
        stats = jnp.zeros((L, LANES), F32)
        for h in range(H):
            u = ci * H + h
            stats = jnp.where(lane == h, gates[ci][0][:, H + h:H + h + 1],
                              jnp.where(lane == H + h, dmaxs[u],
                                        jnp.where(lane == 2 * H + h, jnp.sum(Wl[u], axis=-1, keepdims=True), stats)))
        st_ref[rows_of(ci), :] = stats


def _mlstm_scan_body(L, cb, NS, q_ref, hl_ref, st_ref, kv_ref, ks_ref, op_ref, C0_ref, n0_ref, m0_ref, on_ref,
                     o_ref, Cout_ref, nout_ref, mout_ref, C_scr, n_scr, m_scr):
    c = pl.program_id(1)

    @pl.when(c == 0)
    def _():
        C_scr[...] = C0_ref[...]
        n_scr[...] = n0_ref[...]
        m_scr[...] = m0_ref[...]

    onorm = on_ref[...]
    H = ML_HEADS
    units = [(ci, h) for ci in range(cb) for h in range(H)]
    rows_of = lambda ci: slice(ci * L, (ci + 1) * L)
    sts = [st_ref[rows_of(ci), :] for ci in range(cb)]
    pre = {}
    for h in range(H):
        m_prev = m_scr[h:h + 1, 0:1]
        n_prev = n_scr[h:h + 1, :]
        C = C_scr[h]
        for ci in range(cb):
            b_c = sts[ci][:, h:h + 1]
            dmax = sts[ci][:, H + h:H + h + 1]
            mt = jnp.maximum(b_c + m_prev, dmax)
            pre[(ci, h)] = (C, n_prev, m_prev, mt)
            mL = mt[L - 1:L, :]
            fL = jnp.exp(b_c[L - 1:L, :] + m_prev - mL)
            gL = jnp.exp(dmax[L - 1:L, :] - mL)
            C = fL * C + gL * kv_ref[ci, h]
            n_prev = fL * n_prev + gL * ks_ref[ci, h:h + 1, :]
            m_prev = mL
        C_scr[h] = C
        n_scr[h:h + 1, :] = n_prev
        m_scr[h:h + 1, :] = jnp.broadcast_to(m_prev, (1, LANES))

    qs = {u: q_ref[rows_of(u[0]), u[1] * ML_DQK:(u[1] + 1) * ML_DQK].astype(F32) for u in units}
    qC = {u: _dot(qs[u], pre[u][0]) for u in units}
    hs = {}
    for u in units:
        ci, h = u
        _, n_prev, m_prev, mt = pre[u]
        b_c = sts[ci][:, h:h + 1]
        dmax = sts[ci][:, H + h:H + h + 1]
        sw = sts[ci][:, 2 * H + h:2 * H + h + 1]
        e_i = jnp.exp(b_c + m_prev - mt)
        e_l = jnp.exp(dmax - mt)
        num = e_i * qC[u] + e_l * hl_ref[rows_of(ci), h * ML_DV:(h + 1) * ML_DV]
        den = e_i * jnp.sum(qs[u] * n_prev, axis=-1, keepdims=True) + e_l * sw
        hs[u] = num / jnp.maximum(jnp.abs(den), jnp.exp(-mt))
    for u in units:
        ci, h = u
        og = op_ref[rows_of(ci), h * ML_DV:(h + 1) * ML_DV].astype(F32)
        o_ref[rows_of(ci), h * ML_DV:(h + 1) * ML_DV] = (_sigmoid(og) * _rms(hs[u], onorm)).astype(BF16)

    @pl.when(c == NS - 1)
    def _():
        Cout_ref[...] = C_scr[...]
        nout_ref[...] = n_scr[...]
        mout_ref[...] = m_scr[...]


def _mlstm_mixer(main, small, C0, n0, m0, bias, onorm, nch=4, cb=4):
    B, T, _ = main.shape
    L = min(ML_CHUNK, T)
    assert T % L == 0
    NC = T // L
    nch = min(nch, NC)
    cb = min(cb, NC)
    assert NC % nch == 0 and NC % cb == 0
    LA = L * nch
    H = ML_HEADS
    hl, st, kv, ks = pl.pallas_call(
        functools.partial(_mlstm_prep_body, L, nch),
        grid=(B, NC // nch),
        in_specs=[pl.BlockSpec((None, LA, 1024), lambda b, c: (b, c, 0)),
                  pl.BlockSpec((None, LA, 1024), lambda b, c: (b, c, 1)),
                  pl.BlockSpec((None, LA, LANES), lambda b, c: (b, c, 0)),
                  pl.BlockSpec((1, LANES), lambda b, c: (0, 0))],
        out_specs=[pl.BlockSpec((None, LA, 1024), lambda b, c: (b, c, 0)),
                   pl.BlockSpec((None, LA, LANES), lambda b, c: (b, c, 0)),
                   pl.BlockSpec((None, nch, H, ML_DQK, ML_DV), lambda b, c: (b, c, 0, 0, 0)),
                   pl.BlockSpec((None, nch, H, ML_DQK), lambda b, c: (b, c, 0, 0))],
        out_shape=[jax.ShapeDtypeStruct((B, T, 1024), F32),
                   jax.ShapeDtypeStruct((B, T, LANES), F32),
                   jax.ShapeDtypeStruct((B, NC, H, ML_DQK, ML_DV), F32),
                   jax.ShapeDtypeStruct((B, NC, H, ML_DQK), F32)],
        compiler_params=_cparams(("parallel", "parallel")),
        name="mlstm_prep",
    )(main, main, small, bias)
    NS = NC // cb
    m0b = jnp.broadcast_to(m0[:, :, None], (B, H, LANES))
    st_C = pl.BlockSpec((None, H, ML_DQK, ML_DV), lambda b, c: (b, 0, 0, 0))
    st_n = pl.BlockSpec((None, H, ML_DQK), lambda b, c: (b, 0, 0))
    st_m = pl.BlockSpec((None, H, LANES), lambda b, c: (b, 0, 0))
    return pl.pallas_call(
        functools.partial(_mlstm_scan_body, L, cb, NS),
        grid=(B, NS),
        in_specs=[pl.BlockSpec((None, cb * L, H * ML_DQK), lambda b, c: (b, c, 0)),
                  pl.BlockSpec((None, cb * L, 1024), lambda b, c: (b, c, 0)),
                  pl.BlockSpec((None, cb * L, LANES), lambda b, c: (b, c, 0)),
                  pl.BlockSpec((None, cb, H, ML_DQK, ML_DV), lambda b, c: (b, c, 0, 0, 0)),
                  pl.BlockSpec((None, cb, H, ML_DQK), lambda b, c: (b, c, 0, 0)),
                  pl.BlockSpec((None, cb * L, 1024), lambda b, c: (b, c, 2)),
                  st_C, st_n, st_m,
                  pl.BlockSpec((1, ML_DV), lambda b, c: (0, 0))],
        out_specs=[pl.BlockSpec((None, cb * L, 1024), lambda b, c: (b, c, 0)), st_C, st_n, st_m],
        out_shape=[jax.ShapeDtypeStruct((B, T, 1024), BF16),
                   jax.ShapeDtypeStruct((B, H, ML_DQK, ML_DV), F32),
                   jax.ShapeDtypeStruct((B, H, ML_DQK), F32),
                   jax.ShapeDtypeStruct((B, H, LANES), F32)],
        scratch_shapes=[pltpu.VMEM((H, ML_DQK, ML_DV), F32),
                        pltpu.VMEM((H, ML_DQK), F32),
                        pltpu.VMEM((H, LANES), F32)],
        compiler_params=_cparams(("parallel", "arbitrary")),
        name="mlstm_scan",
    )(main, hl, st, kv, ks, main, C0, n0, m0b, onorm.reshape(1, ML_DV))


FOX_BIAS_LANE = FOX_DH


def _fox_prep_body(q_ref, k_ref, v_ref, small_ref, bf_ref, qn_ref, kn_ref,
                   qh_ref, k32_ref, v32_ref, lf_ref):
    r = lax.broadcasted_iota(jnp.int32, (LANES, LANES), 0) // FOX_DH
    cc = lax.broadcasted_iota(jnp.int32, (LANES, LANES), 1) // FOX_DH
    avg = jnp.where(r == cc, 1.0 / FOX_DH, 0.0).astype(BF16)

    def head_rms(x, gain):
        parts = []
        for t in range(x.shape[1] // LANES):
            xt = x[:, t * LANES:(t + 1) * LANES]
            hi, lo = _split2(xt * xt)
            ms = (jnp.dot(hi, avg, preferred_element_type=F32)
                  + jnp.dot(lo, avg, preferred_element_type=F32))
            parts.append(xt * lax.rsqrt(ms + EPS))
        return jnp.concatenate(parts, axis=-1) * gain

    qn = head_rms(q_ref[...].astype(F32), qn_ref[...]) * (LOG2E * FOX_DH ** -0.5)
    lane = lax.broadcasted_iota(jnp.int32, (1, LANES), 1)
    ones3 = jnp.where(lane < FOX_BIAS_LANE + 3, 1.0, 0.0)
    for h in range(FOX_HEADS):
        xt = qn[:, (h // 2) * LANES:(h // 2 + 1) * LANES]
        if h % 2:
            xt = pltpu.roll(xt, FOX_DH, axis=1)
        qh_ref[h] = jnp.where(lane < FOX_DH, xt, ones3).astype(BF16)
    k32_ref[...] = head_rms(k_ref[...].astype(F32), kn_ref[...])
    v32_ref[...] = v_ref[...].astype(F32)
    lf_ref[...] = jax.nn.log_sigmoid(small_ref[...] + bf_ref[...])


def _fox_prep(main, small, bf_row, qn_row, kn_row, tm=512):
    B, T, _ = main.shape
    tm = min(tm, T)
    assert T % tm == 0
    blk = lambda j: pl.BlockSpec((None, tm, 1024), lambda b, i: (b, i, j))
    row = lambda n: pl.BlockSpec((1, n), lambda b, i: (0, 0))
    o1024 = pl.BlockSpec((None, tm, 1024), lambda b, i: (b, i, 0))
    return pl.pallas_call(
        _fox_prep_body,
        grid=(B, T // tm),
        in_specs=[blk(0), blk(1), blk(2),
                  pl.BlockSpec((None, tm, LANES), lambda b, i: (b, i, 0)),
                  row(LANES), row(1024), row(1024)],
        out_specs=[pl.BlockSpec((None, FOX_HEADS, tm, LANES), lambda b, i: (b, 0, i, 0)),
                   o1024, o1024,
                   pl.BlockSpec((None, tm, LANES), lambda b, i: (b, i, 0))],
        out_shape=[jax.ShapeDtypeStruct((B, FOX_HEADS, T, LANES), BF16),
                   jax.ShapeDtypeStruct((B, T, 1024), F32),
                   jax.ShapeDtypeStruct((B, T, 1024), F32),
                   jax.ShapeDtypeStruct((B, T, LANES), F32)],
        compiler_params=_cparams(("parallel", "parallel")),
        name="fox_prep",
    )(main, main, main, small, bf_row, qn_row, kn_row)


def _fox_kv_body(tm, n_past, *refs):
    if n_past:
        kp_ref, vp_ref, lp_ref, k_ref, v_ref, lf_ref, kh_ref, vh_ref, carry = refs
    else:
        k_ref, v_ref, lf_ref, kh_ref, vh_ref, carry = refs
    i = pl.program_id(1)

    @pl.when(i == 0)
    def _():
        carry[...] = jnp.zeros_like(carry)

    k = k_ref[...]
    v = v_ref[...]
    lf = lf_ref[...]
    if n_past:
        past = i < n_past
        k = jnp.where(past, kp_ref[...], k)
        v = jnp.where(past, vp_ref[...], v)
        lf = jnp.where(past, lp_ref[...], lf)
    incl, _, _ = _tri_masks(tm)
    tril_b = jnp.where(incl, 1.0, 0.0).astype(BF16)
    F = _cumsum_rows(tril_b, lf) + carry[0:1, :]
    carry[0:1, :] = F[tm - 1:tm, :]
    nb = -LOG2E * F
    b1 = nb.astype(BF16).astype(F32)
    r1 = nb - b1
    b2 = r1.astype(BF16).astype(F32)
    b3 = r1 - b2
    lane = lax.broadcasted_iota(jnp.int32, (1, LANES), 1)
    ones1 = jnp.where(lane == FOX_BIAS_LANE, 1.0, 0.0)
    for h in range(FOX_HEADS):
        kt = k[:, (h // 2) * LANES:(h // 2 + 1) * LANES]
        vt = v[:, (h // 2) * LANES:(h // 2 + 1) * LANES]
        if h % 2:
            kt = pltpu.roll(kt, FOX_DH, axis=1)
            vt = pltpu.roll(vt, FOX_DH, axis=1)
        bias = jnp.where(lane == FOX_BIAS_LANE, b1[:, h:h + 1],
                         jnp.where(lane == FOX_BIAS_LANE + 1, b2[:, h:h + 1],
                                   jnp.where(lane == FOX_BIAS_LANE + 2, b3[:, h:h + 1], 0.0)))
        kh_ref[h] = jnp.where(lane < FOX_DH, kt, bias).astype(BF16)
        vh_ref[h] = jnp.where(lane < FOX_DH, vt, ones1).astype(BF16)


def _fox_kv(k_new, v_new, lf_new, past=None, tm=512):
    B, T, _ = k_new.shape
    if past is None:
        tm = min(tm, T)
        assert T % tm == 0
        n_past, n_new = 0, T // tm
        args, specs = [], []
    else:
        k_past, v_past, lf_past = past
        P = k_past.shape[1]
        tm = min(tm, P)
        assert P % tm == 0 and T <= tm
        n_past, n_new = P // tm, 1
        pad = lambda a: jnp.pad(a, ((0, 0), (0, tm - T), (0, 0)))
        k_new, v_new, lf_new = pad(k_new), pad(v_new), pad(lf_new)
        clamp = lambda b, i: (b, jnp.minimum(i, n_past - 1), 0)
        args = [k_past, v_past, lf_past]
        specs = [pl.BlockSpec((None, tm, 1024), clamp), pl.BlockSpec((None, tm, 1024), clamp),
                 pl.BlockSpec((None, tm, LANES), clamp)]
    new_idx = lambda b, i: (b, jnp.maximum(i - n_past, 0), 0)
    N = (n_past + n_new) * tm
    hm = pl.BlockSpec((None, FOX_HEADS, tm, LANES), lambda b, i: (b, 0, i, 0))
    return pl.pallas_call(
        functools.partial(_fox_kv_body, tm, n_past),
        grid=(B, n_past + n_new),
        in_specs=specs + [pl.BlockSpec((None, tm, 1024), new_idx),
                          pl.BlockSpec((None, tm, 1024), new_idx),
                          pl.BlockSpec((None, tm, LANES), new_idx)],
        out_specs=[hm, hm],
        out_shape=[jax.ShapeDtypeStruct((B, FOX_HEADS, N, LANES), BF16),
                   jax.ShapeDtypeStruct((B, FOX_HEADS, N, LANES), BF16)],
        scratch_shapes=[pltpu.VMEM((8, LANES), F32)],
        compiler_params=_cparams(("parallel", "arbitrary")),
        name="fox_kv",
    )(*args, k_new, v_new, lf_new)


def _flash_body(P, tq, tk, rq, qi_tab, kj_tab, q_ref, k_ref, v_ref, gate_ref, o_ref, m_scr, acc_scr):
    s_idx = pl.program_id(2)
    qi = qi_tab[s_idx]
    kj = kj_tab[s_idx]
    last_kj = (P + (qi + 1) * tq - 1) // tk
    aligned = tq == tk and P % tk == 0

    @pl.when(kj == 0)
    def _():
        m_scr[...] = jnp.full(m_scr.shape, NEG_BIG, F32)
        acc_scr[...] = jnp.zeros_like(acc_scr)

    def step(masked):
        units = [(hh, r) for r in range(tq // rq) for hh in range(2)]

        def logits(u):
            hh, r = u
            rows = slice(r * rq, (r + 1) * rq)
            nk = min(tk, (r + 1) * rq) if (masked and aligned) else tk
            s = lax.dot_general(q_ref[hh, rows, :], k_ref[hh, 0:nk, :], (((1,), (1,)), ((), ())),
                                preferred_element_type=F32)
            if masked:
                qpos = P + qi * tq + r * rq + lax.broadcasted_iota(jnp.int32, (rq, nk), 0)
                kpos = kj * tk + lax.broadcasted_iota(jnp.int32, (rq, nk), 1)
                s = jnp.where(kpos <= qpos, s, NEG_BIG)
            return s, nk

        def probs(u, s):
            hh, r = u
            rows = slice(r * rq, (r + 1) * rq)
            m_prev = m_scr[hh, rows, :][:, 0:1]
            m_new = jnp.maximum(m_prev, jnp.max(s, axis=-1, keepdims=True))
            m_scr[hh, rows, :] = jnp.broadcast_to(m_new, (rq, LANES))
            return jnp.exp2(m_prev - m_new), jnp.exp2(s - m_new).astype(BF16)

        def accumulate(u, alpha, p, nk):
            hh, r = u
            rows = slice(r * rq, (r + 1) * rq)
            acc_scr[hh, rows, :] = alpha * acc_scr[hh, rows, :] + jnp.dot(
                p, v_ref[hh, 0:nk, :], preferred_element_type=F32)

        n = len(units)
        pend_s = {0: logits(units[0])}
        pend_p = {}
        for i in range(n):
            if i + 1 < n:
                pend_s[i + 1] = logits(units[i + 1])
            s, nk = pend_s.pop(i)
            pend_p[i] = probs(units[i], s) + (nk,)
            if i >= 1:
                accumulate(units[i - 1], *pend_p.pop(i - 1))
        accumulate(units[n - 1], *pend_p.pop(n - 1))

    needs_mask = (kj + 1) * tk - 1 > P + qi * tq

    @pl.when(needs_mask)
    def _():
        step(True)

    @pl.when(jnp.logical_not(needs_mask))
    def _():
        step(False)

    @pl.when(kj == last_kj)
    def _():
        lane = lax.broadcasted_iota(jnp.int32, (1, LANES), 1)
        a0 = acc_scr[0]
        a1 = acc_scr[1]
        o0 = a0 * (1.0 / a0[:, FOX_BIAS_LANE:FOX_BIAS_LANE + 1])
        o1 = a1 * (1.0 / a1[:, FOX_BIAS_LANE:FOX_BIAS_LANE + 1])
        o = jnp.where(lane < FOX_DH, o0, pltpu.roll(o1, FOX_DH, axis=1))
        g = gate_ref[...].astype(F32)
        o_ref[...] = (o * _sigmoid(g)).astype(BF16)


def _fox_flash(qh, kh, vh, gate_src, P, tq, tk, rq):
    B, _, T, _ = qh.shape
    N = kh.shape[2]
    assert T % tq == 0 and N % tk == 0 and tq % rq == 0
    nq = T // tq
    pairs = [(qi, kj) for qi in range(nq) for kj in range((P + (qi + 1) * tq - 1) // tk + 1)]
    qi_tab = jnp.asarray(np.array([p[0] for p in pairs], np.int32))
    kj_tab = jnp.asarray(np.array([p[1] for p in pairs], np.int32))
    HP = FOX_HEADS // 2
    gate_blk0 = 3 * (1024 // LANES)
    grid_spec = pltpu.PrefetchScalarGridSpec(
        num_scalar_prefetch=2,
        grid=(B, HP, len(pairs)),
        in_specs=[pl.BlockSpec((None, 2, tq, LANES), lambda b, p, s, qt, kt: (b, p, qt[s], 0)),
                  pl.BlockSpec((None, 2, tk, LANES), lambda b, p, s, qt, kt: (b, p, kt[s], 0)),
                  pl.BlockSpec((None, 2, tk, LANES), lambda b, p, s, qt, kt: (b, p, kt[s], 0)),
                  pl.BlockSpec((None, tq, LANES), lambda b, p, s, qt, kt: (b, qt[s], gate_blk0 + p))],
        out_specs=pl.BlockSpec((None, tq, LANES), lambda b, p, s, qt, kt: (b, qt[s], p)),
        scratch_shapes=[pltpu.VMEM((2, tq, LANES), F32),
                        pltpu.VMEM((2, tq, LANES), F32)],
    )
    return pl.pallas_call(
        functools.partial(_flash_body, P, tq, tk, rq),
        grid_spec=grid_spec,
        out_shape=jax.ShapeDtypeStruct((B, T, 1024), BF16),
        compiler_params=_cparams(("parallel", "parallel", "arbitrary")),
        name="fox_flash",
    )(qi_tab, kj_tab, qh, kh, vh, gate_src)


def _pad_lanes(a, offset=0):
    n = a.shape[-1]
    pads = [(0, 0)] * (a.ndim - 1) + [(offset, LANES - n - offset)]
    return jnp.pad(a, pads)


def _trunk(x, gdn_conv, gdn_S, ml_C, ml_n, ml_m, fox_k, fox_v, fox_lf, mem_k, mem_v, W):
    B, T, D = x.shape
    M = B * T
    x2 = x.reshape(M, D)
    conv_o, S_o, C_o, n_o, m_o, k_o, v_o, lf_o = [], [], [], [], [], [], [], []
    for layer in range(DEPTH):
        kind, j = layer % N_MIXERS, layer // N_MIXERS
        gains = W["norm_gains"][layer]
        if kind == 0:
            main, small, tail = _gdn_proj(x2, gains[0], W["gdn_w_in"], j, W["gdn_w_small"], W["gdn_wc_t"][j],
                                          gdn_conv[j], T)
            o, S = _gdn_mixer(main.reshape(B, T, GDN_MAIN), small.reshape(B, T, LANES),
                              gdn_S[j], W["gdn_par"][j], W["gdn_onorm"][j])
            conv_o.append(tail[:, 8 - (CONV_W - 1):, :])
            S_o.append(S)
            w_out = W["gdn_w_out"]
        elif kind == 1:
            main, small = _prenorm_mm(x2, gains[0], W["ml_w_in"], j, ML_MAIN, W["ml_w_small"])
            o, C, n, m = _mlstm_mixer(main.reshape(B, T, ML_MAIN), small.reshape(B, T, LANES),
                                      ml_C[j], ml_n[j], ml_m[j], W["ml_bias"][j], W["ml_onorm"][j])
            C_o.append(C)
            n_o.append(n)
            m_o.append(m[:, :, 0])
            w_out = W["ml_w_out"]
        else:
            main, small = _prenorm_mm(x2, gains[0], W["fox_w_in"], j, FOX_MAIN, W["fox_w_small"])
            main3 = main.reshape(B, T, FOX_MAIN)
            qh, k32, v32, lfp = _fox_prep(main3, small.reshape(B, T, LANES), W["fox_bf"][j],
                                          W["fox_qn"][j], W["fox_kn"][j])
            P = fox_k[j].shape[1]
            if P == 0:
                kh, vh = _fox_kv(k32, v32, lfp)
                tq, tk, rq = min(1024, T), min(1024, T), 256
            else:
                kh, vh = _fox_kv(k32, v32, lfp, past=(fox_k[j].reshape(B, P, 1024), fox_v[j].reshape(B, P, 1024),
                                                      _pad_lanes(fox_lf[j])), tm=256)
                tq, tk, rq = T, kh.shape[2], T
            o = _fox_flash(qh, kh, vh, main3, P, tq, tk, rq)
            k_o.append(k32.reshape(B, T, FOX_HEADS, FOX_DH))
            v_o.append(v32.reshape(B, T, FOX_HEADS, FOX_DH))
            lf_o.append(lfp[:, :, :FOX_HEADS])
            w_out = W["fox_w_out"]
        x2 = _memattn(o, w_out, j, gains[1], x2.reshape(B, T, D), gains[2], W["x_w_q"],
                      mem_k[layer], mem_v[layer], W["x_w_o"], layer, gains[3]).reshape(M, D)
        x2 = _ffn(x2, gains[4], W["ffn_w_gu"], W["ffn_w_down"], layer, gains[5])
    return (x2.reshape(B, T, D), jnp.stack(conv_o), jnp.stack(S_o), jnp.stack(C_o), jnp.stack(n_o),
            jnp.stack(m_o), jnp.stack(k_o), jnp.stack(v_o), jnp.stack(lf_o))


def kernel(x_prompt, x_sample, state_gdn_conv, state_gdn_S, state_mlstm_C, state_mlstm_n, state_mlstm_m, cache_fox_k, cache_fox_v, cache_fox_logf, cache_mem_k, cache_mem_v, mem_prompt, norm_gains, gdn_w_in, gdn_w_conv, gdn_a_log, gdn_dt_bias, gdn_onorm, gdn_w_out, ml_w_in, ml_b_i, ml_b_f, ml_onorm, ml_w_out, fox_w_in, fox_b_f, fox_qk_norm, fox_w_out, x_w_q, x_w_kv, x_w_o, ffn_w_gu, ffn_w_down):
    B = x_prompt.shape[0]
    DB = x_sample.shape[0]
    n_a, n_b, n_c = gdn_w_in.shape[0], ml_w_in.shape[0], fox_w_in.shape[0]
    W = {
        "norm_gains": norm_gains,
        "gdn_w_in": gdn_w_in.astype(BF16),
        "gdn_w_small": _pad_lanes(gdn_w_in[:, :, GDN_MAIN:]).astype(BF16),
        "gdn_wc_t": jnp.transpose(gdn_w_conv, (0, 2, 1)),
        "gdn_par": jnp.stack([_pad_lanes(gdn_a_log, GDN_HEADS), _pad_lanes(gdn_dt_bias, GDN_HEADS)], axis=1),
        "gdn_onorm": gdn_onorm,
        "gdn_w_out": gdn_w_out.astype(BF16),
        "ml_w_in": ml_w_in.astype(BF16),
        "ml_w_small": _pad_lanes(ml_w_in[:, :, ML_MAIN:]).astype(BF16),
        "ml_bias": _pad_lanes(jnp.concatenate([ml_b_i, ml_b_f], axis=-1))[:, None, :],
        "ml_onorm": ml_onorm,
        "ml_w_out": ml_w_out.astype(BF16),
        "fox_w_in": fox_w_in.astype(BF16),
        "fox_w_small": _pad_lanes(fox_w_in[:, :, FOX_MAIN:]).astype(BF16),
        "fox_bf": _pad_lanes(fox_b_f)[:, None, :],
        "fox_qn": jnp.tile(fox_qk_norm[:, 0, :], (1, FOX_HEADS))[:, None, :],
        "fox_kn": jnp.tile(fox_qk_norm[:, 1, :], (1, FOX_HEADS))[:, None, :],
        "fox_w_out": fox_w_out.astype(BF16),
        "x_w_q": x_w_q.astype(BF16),
        "x_w_o": x_w_o.astype(BF16),
        "ffn_w_gu": ffn_w_gu.astype(BF16),
        "ffn_w_down": ffn_w_down.astype(BF16),
    }

    mem2 = mem_prompt.reshape(B * N_MEM, D_MODEL)
    x_w_kv_b = x_w_kv.astype(BF16)
    kv = [_prenorm_mm(mem2, norm_gains[l, 6], x_w_kv_b, l, 2 * D_MODEL, out_dtype=F32).reshape(B, N_MEM, 2 * D_MODEL)
          for l in range(DEPTH)]
    p_mem_k = jnp.stack([a[:, :, :D_MODEL].reshape(B, N_MEM, X_HEADS, X_DH) for a in kv])
    p_mem_v = jnp.stack([a[:, :, D_MODEL:].reshape(B, N_MEM, X_HEADS, X_DH) for a in kv])

    cmk = cache_mem_k.reshape(DEPTH * DB, N_MEM, D_MODEL)
    cmv = cache_mem_v.reshape(DEPTH * DB, N_MEM, D_MODEL)
    zeros = lambda *s: jnp.zeros(s, F32)
    prompt = _trunk(
        x_prompt,
        zeros(n_a, B, CONV_W - 1, GDN_QKV), zeros(n_a, B, GDN_HEADS, GDN_DK, GDN_DV),
        zeros(n_b, B, ML_HEADS, ML_DQK, ML_DV), zeros(n_b, B, ML_HEADS, ML_DQK), zeros(n_b, B, ML_HEADS),
        zeros(n_c, B, 0, FOX_HEADS, FOX_DH), zeros(n_c, B, 0, FOX_HEADS, FOX_DH), zeros(n_c, B, 0, FOX_HEADS),
        [(a, 0, 0) for a in kv], [(a, 0, 1) for a in kv], W)
    sample = _trunk(
        x_sample, state_gdn_conv, state_gdn_S, state_mlstm_C, state_mlstm_n, state_mlstm_m,
        cache_fox_k, cache_fox_v, cache_fox_logf,
        [(cmk, l * DB, 0) for l in range(DEPTH)], [(cmv, l * DB, 0) for l in range(DEPTH)], W)
    (y_p, p_conv, p_S, p_C, p_n, p_m, p_k, p_v, p_lf) = prompt
    (y_s, s_conv, s_S, s_C, s_n, s_m, s_k, s_v, s_lf) = sample
    return (y_p, y_s, p_conv, p_S, p_C, p_n, p_m, p_k, p_v, p_lf, p_mem_k, p_mem_v,
            s_conv, s_S, s_C, s_n, s_m, s_k, s_v, s_lf)
```

```python
import functools

import numpy as np
import jax
import jax.numpy as jnp
from jax import lax
from jax.experimental import pallas as pl
from jax.experimental.pallas import tpu as pltpu

F32 = jnp.float32
BF16 = jnp.bfloat16

D_MODEL = 1024
DEPTH = 4
N_MIXERS = 3
EPS = 1e-6
CONV_W = 4

GDN_HEADS = 8
GDN_DK = 128
GDN_DV = 128
GDN_QKV = 3072
GDN_MAIN = 4096

ML_HEADS = 4
ML_DQK = 128
ML_DV = 256
ML_MAIN = 3072
GATE_CAP = 15.0

FOX_HEADS = 16
FOX_DH = 64
FOX_MAIN = 4096

N_MEM = 256
X_HEADS = 4
X_DH = 256
D_FF = 2816

LANES = 128
NEG_BIG = -1e30
LOG2E = 1.4426950408889634

VMEM_LIMIT = 48 * 1024 * 1024


def _wspec(li, blk, imap):
    return pl.BlockSpec((None,) + blk, lambda *g: (li,) + imap(*g))


def _cparams(sem):
    return pltpu.CompilerParams(dimension_semantics=sem, vmem_limit_bytes=VMEM_LIMIT)


def _rms(x, g):
    return x * lax.rsqrt(jnp.mean(x * x, axis=-1, keepdims=True) + EPS) * g


def _sigmoid(x):
    return 1.0 / (1.0 + jnp.exp(-x))


def _dot(a, b):
    return jnp.dot(a.astype(BF16), b.astype(BF16), preferred_element_type=F32)


def _dot_nt(a, b):
    return lax.dot_general(a.astype(BF16), b.astype(BF16), (((1,), (1,)), ((), ())),
                           preferred_element_type=F32)


def _dot_tn(a, b):
    return lax.dot_general(a.astype(BF16), b.astype(BF16), (((0,), (0,)), ((), ())),
                           preferred_element_type=F32)


def _split2(x):
    hi = x.astype(BF16)
    lo = (x - hi.astype(F32)).astype(BF16)
    return hi, lo


def _dot_x3(a, b):
    ah, al = _split2(a)
    bh, bl = _split2(b)
    d = lambda u, v: jnp.dot(u, v, preferred_element_type=F32)
    return d(ah, bh) + (d(ah, bl) + d(al, bh))


def _cumsum_rows(tril_b, x):
    h = x.astype(BF16)
    r = x - h.astype(F32)
    m = r.astype(BF16)
    l = (r - m.astype(F32)).astype(BF16)
    d = lambda v: jnp.dot(tril_b, v, preferred_element_type=F32)
    return d(h) + (d(m) + d(l))


def _tri_masks(L):
    row = lax.broadcasted_iota(jnp.int32, (L, L), 0)
    col = lax.broadcasted_iota(jnp.int32, (L, L), 1)
    return row >= col, row > col, row == col


def _prenorm_mm_body(has_small, *refs):
    if has_small:
        x_ref, g_ref, w_ref, ws_ref, o_ref, os_ref, xn_ref = refs
    else:
        x_ref, g_ref, w_ref, o_ref, xn_ref = refs

    @pl.when(pl.program_id(1) == 0)
    def _():
        xb = _rms(x_ref[...], g_ref[...]).astype(BF16)
        xn_ref[...] = xb
        if has_small:
            os_ref[...] = jnp.dot(xb, ws_ref[...], preferred_element_type=F32)

    o_ref[...] = jnp.dot(xn_ref[...], w_ref[...], preferred_element_type=F32).astype(o_ref.dtype)


def _prenorm_mm(x, gain, w_main, li, n_out, w_small=None, out_dtype=BF16, tm=1024, tn=1024):
    M, D = x.shape
    N = n_out
    tm = min(tm, M)
    tn = min(tn, N)
    assert M % tm == 0 and N % tn == 0
    has_small = w_small is not None
    in_specs = [pl.BlockSpec((tm, D), lambda i, j: (i, 0)),
                pl.BlockSpec((1, D), lambda i, j: (0, 0)),
                _wspec(li, (D, tn), lambda i, j: (0, j))]
    out_specs = [pl.BlockSpec((tm, tn), lambda i, j: (i, j))]
    out_shape = [jax.ShapeDtypeStruct((M, N), out_dtype)]
    args = [x, gain.reshape(1, D), w_main]
    if has_small:
        in_specs.append(_wspec(li, (D, LANES), lambda i, j: (0, 0)))
        out_specs.append(pl.BlockSpec((tm, LANES), lambda i, j: (i, 0)))
        out_shape.append(jax.ShapeDtypeStruct((M, LANES), F32))
        args.append(w_small)
    res = pl.pallas_call(
        functools.partial(_prenorm_mm_body, has_small),
        grid=(M // tm, N // tn),
        in_specs=in_specs, out_specs=out_specs, out_shape=out_shape,
        scratch_shapes=[pltpu.VMEM((tm, D), BF16)],
        compiler_params=_cparams(("parallel", "arbitrary")),
        name="prenorm_mm",
    )(*args)
    return res if has_small else res[0]


def _memattn_body(nb, o_ref, wout_ref, g1_ref, x_ref, g2_ref, wq_ref, mk_ref, mv_ref, wo_ref, g3_ref, y_ref):
    tq, D = x_ref.shape[1], x_ref.shape[2]
    R = nb * tq
    o_in = o_ref[...].reshape(R, o_ref.shape[2])
    x = x_ref[...].reshape(R, D)
    x = x + _rms(jnp.dot(o_in, wout_ref[...], preferred_element_type=F32), g1_ref[...])
    h = _rms(x, g2_ref[...]).astype(BF16)
    q = jnp.dot(h, wq_ref[...], preferred_element_type=F32).astype(BF16)
    rows = []
    for bi in range(nb):
        mk = mk_ref[bi].astype(BF16)
        mv = mv_ref[bi].astype(BF16)
        qb = q[bi * tq:(bi + 1) * tq]
        outs = []
        for hd in range(X_HEADS):
            sl = slice(hd * X_DH, (hd + 1) * X_DH)
            s = _dot_nt(qb[:, sl], mk[:, sl]) * (X_DH ** -0.5)
            p = jnp.exp(s - jnp.max(s, axis=-1, keepdims=True))
            p = p * (1.0 / jnp.sum(p, axis=-1, keepdims=True))
            outs.append(jnp.dot(p.astype(BF16), mv[:, sl], preferred_element_type=F32))
        rows.append(jnp.concatenate(outs, axis=-1).astype(BF16))
    o = rows[0] if nb == 1 else jnp.concatenate(rows, axis=0)
    y = jnp.dot(o, wo_ref[...], preferred_element_type=F32)
    y_ref[...] = (x + _rms(y, g3_ref[...])).reshape(nb, tq, D)


def _memattn(o, w_out, lo, g1, x, g2, wq, mem_k, mem_v, wo, lx, g3, tq=512, rows_per_step=256):
    B, T, D = x.shape
    tq = min(tq, T)
    assert T % tq == 0
    nb = max(1, min(B, rows_per_step // tq)) if tq == T else 1
    assert B % nb == 0
    HD = X_HEADS * X_DH
    K = o.shape[-1]
    (mk_arr, koff, kcol), (mv_arr, voff, vcol) = mem_k, mem_v
    assert koff % nb == 0 and voff % nb == 0
    return pl.pallas_call(
        functools.partial(_memattn_body, nb),
        grid=(B // nb, T // tq),
        in_specs=[pl.BlockSpec((nb, tq, K), lambda b, i: (b, i, 0)),
                  _wspec(lo, (K, D), lambda b, i: (0, 0)),
                  pl.BlockSpec((1, D), lambda b, i: (0, 0)),
                  pl.BlockSpec((nb, tq, D), lambda b, i: (b, i, 0)),
                  pl.BlockSpec((1, D), lambda b, i: (0, 0)),
                  _wspec(lx, (D, HD), lambda b, i: (0, 0)),
                  pl.BlockSpec((nb, N_MEM, HD), lambda b, i: (koff // nb + b, 0, kcol)),
                  pl.BlockSpec((nb, N_MEM, HD), lambda b, i: (voff // nb + b, 0, vcol)),
                  _wspec(lx, (HD, D), lambda b, i: (0, 0)),
                  pl.BlockSpec((1, D), lambda b, i: (0, 0))],
        out_specs=pl.BlockSpec((nb, tq, D), lambda b, i: (b, i, 0)),
        out_shape=jax.ShapeDtypeStruct((B, T, D), F32),
        compiler_params=_cparams(("parallel", "parallel")),
        name="memattn",
    )(o, w_out, g1.reshape(1, D), x, g2.reshape(1, D), wq, mk_arr, mv_arr, wo, g3.reshape(1, D))


def _ffn_body(nf, x_ref, g4_ref, wg_ref, wu_ref, wd_ref, g5_ref, y_ref, xn_ref, acc_ref):
    f = pl.program_id(1)

    @pl.when(f == 0)
    def _():
        xn_ref[...] = _rms(x_ref[...], g4_ref[...]).astype(BF16)
        acc_ref[...] = jnp.zeros_like(acc_ref)

    xn = xn_ref[...]
    g = jnp.dot(xn, wg_ref[...], preferred_element_type=F32)
    u = jnp.dot(xn, wu_ref[...], preferred_element_type=F32)
    a = (g * _sigmoid(g)) * u
    acc_ref[...] += jnp.dot(a.astype(BF16), wd_ref[...], preferred_element_type=F32)

    @pl.when(f == nf - 1)
    def _():
        y_ref[...] = x_ref[...] + _rms(acc_ref[...], g5_ref[...])


def _ffn(x, g4, w_gu, w_down, li, g5, tm=1024, tf=1408):
    M, D = x.shape
    tm = min(tm, M)
    assert M % tm == 0 and D_FF % tf == 0
    nf = D_FF // tf
    return pl.pallas_call(
        functools.partial(_ffn_body, nf),
        grid=(M // tm, nf),
        in_specs=[pl.BlockSpec((tm, D), lambda i, f: (i, 0)),
                  pl.BlockSpec((1, D), lambda i, f: (0, 0)),
                  _wspec(li, (D, tf), lambda i, f: (0, f)),
                  _wspec(li, (D, tf), lambda i, f: (0, nf + f)),
                  _wspec(li, (tf, D), lambda i, f: (f, 0)),
                  pl.BlockSpec((1, D), lambda i, f: (0, 0))],
        out_specs=pl.BlockSpec((tm, D), lambda i, f: (i, 0)),
        out_shape=jax.ShapeDtypeStruct((M, D), F32),
        scratch_shapes=[pltpu.VMEM((tm, D), BF16), pltpu.VMEM((tm, D), F32)],
        compiler_params=_cparams(("parallel", "arbitrary")),
        name="ffn",
    )(x, g4.reshape(1, D), w_gu, w_gu, w_down, g5.reshape(1, D))


GDN_CHUNK = 64
GDN_BASE = 8


def _inverse_masks(L):
    row = lax.broadcasted_iota(jnp.int32, (L, L), 0)
    col = lax.broadcasted_iota(jnp.int32, (L, L), 1)
    base = row // GDN_BASE == col // GDN_BASE
    offs = []
    s = GDN_BASE
    while s < L:
        same_pair = row // (2 * s) == col // (2 * s)
        offs.append(jnp.logical_and(same_pair, (row // s) - (col // s) == 1))
        s *= 2
    return base, row == col, offs


def _unit_lower_inverse(As, masks, mm):
    base, diag, offs = masks
    eye = jnp.where(diag, 1.0, 0.0)
    Ps = [jnp.where(base, -A, 0.0) for A in As]
    Ts = [eye + P for P in Ps]
    k = 1
    while k < GDN_BASE // 2:
        Ps = [mm(P, P) for P in Ps]
        Ts = [T + mm(T, P) for T, P in zip(Ts, Ps)]
        k *= 2
    for off in offs:
        Ms = [_dot(T, jnp.where(off, A, 0.0)) for T, A in zip(Ts, As)]
        Ts = [T - _dot(M, T) for T, M in zip(Ts, Ms)]
    return Ts


def _gdn_prep_body(L, nch, qkv_ref, prev_ref, small_ref, buf_ref, wc_ref, par_ref,
                   wk_ref, uv_ref, qt_ref, kt_ref, at_ref, egl_ref, xbuf):
    c = pl.program_id(1)
    LA = L * nch

    @pl.when(c == 0)
    def _():
        xbuf[0:8, :] = jnp.zeros((8, GDN_QKV), F32)
        xbuf[5:8, :] = buf_ref[...]

    @pl.when(c > 0)
    def _():
        xbuf[0:8, :] = prev_ref[8:16, :].astype(F32)

    xbuf[8:8 + LA, :] = qkv_ref[...].astype(F32)
    xb = xbuf[...]
    z = xb * wc_ref[0:1, :]
    for j in range(1, CONV_W):
        z = pltpu.roll(z, 1, axis=0) + xb * wc_ref[j:j + 1, :]
    y = z[8:, :]
    y = y * _sigmoid(y)

    small = small_ref[...]
    beta_all = _sigmoid(small)
    g_all = -jnp.exp(par_ref[0:1, :]) * jax.nn.softplus(small + par_ref[1:2, :])
    incl, strict, _ = _tri_masks(L)
    tril_b = jnp.where(incl, 1.0, 0.0).astype(BF16)
    masks = _inverse_masks(L)
    H = GDN_HEADS

    units = []
    for ci in range(nch):
        rows = slice(ci * L, (ci + 1) * L)
        G_all = _cumsum_rows(tril_b, g_all[rows])
        GT = G_all.T
        egl_ref[ci] = jnp.broadcast_to(jnp.exp(G_all[L - 1:L, :]), (8, LANES))
        for h in range(H):
            qh = y[rows, h * GDN_DK:(h + 1) * GDN_DK]
            kh = y[rows, (H + h) * GDN_DK:(H + h + 1) * GDN_DK]
            vh = y[rows, 2 * H * GDN_DK + h * GDN_DV:2 * H * GDN_DK + (h + 1) * GDN_DV]
            qh = qh * lax.rsqrt(jnp.sum(qh * qh, axis=-1, keepdims=True) + EPS) * (GDN_DK ** -0.5)
            kh = kh * lax.rsqrt(jnp.sum(kh * kh, axis=-1, keepdims=True) + EPS)
            bc = beta_all[rows, h:h + 1]
            Gc = G_all[:, H + h:H + h + 1]
            Gr = GT[H + h:H + h + 1, :]
            eG = jnp.exp(Gc)
            dec = jnp.exp(jnp.where(incl, Gc - Gr, NEG_BIG))
            qkk = _dot_nt(jnp.concatenate([qh, kh], axis=0), kh)
            A = jnp.where(strict, bc * qkk[L:] * dec, 0.0)
            rhs = jnp.concatenate([bc * vh, (bc * eG) * kh], axis=-1)
            cols = slice(h * GDN_DV, (h + 1) * GDN_DV)
            qt_ref[rows, cols] = (eG * qh).astype(BF16)
            kt_ref[rows, cols] = (kh * jnp.exp(Gc[L - 1:L, :] - Gc)).astype(BF16)
            at_ref[rows, cols] = jnp.concatenate(
                [qkk[:L] * dec, jnp.zeros((L, GDN_DV - L), F32)], axis=-1).astype(BF16)
            units.append((rows, cols, A, rhs))

    Ts = _unit_lower_inverse([u[2] for u in units], masks, _dot)
    for (rows, cols, _, rhs), T in zip(units, Ts):
        sol = _dot(T, rhs)
        uv_ref[rows, cols] = sol[:, :GDN_DV]
        wk_ref[rows, cols] = sol[:, GDN_DV:].astype(BF16)


def _gdn_scan_body(L, cb, NS, wk_ref, uv_ref, qt_ref, kt_ref, at_ref, egl_ref, gate_ref, S0_ref, on_ref,
                   o_ref, Sout_ref, S_scr):
    c = pl.program_id(1)

    @pl.when(c == 0)
    def _():
        S_scr[...] = S0_ref[...]

    onorm = on_ref[...]
    H = GDN_HEADS
    for ci in range(cb):
        rows = slice(ci * L, (ci + 1) * L)
        egl = egl_ref[ci]
        hs = [slice(h * GDN_DV, (h + 1) * GDN_DV) for h in range(H)]
        Ss = [S_scr[h] for h in range(H)]
        Xs = [jnp.dot(jnp.concatenate([wk_ref[rows, hs[h]], qt_ref[rows, hs[h]]], axis=0),
                      Ss[h].astype(BF16), preferred_element_type=F32) for h in range(H)]
        Us = [(uv_ref[rows, hs[h]] - Xs[h][:L]).astype(BF16) for h in range(H)]
        os = [Xs[h][L:] + jnp.dot(at_ref[rows, h * GDN_DV:h * GDN_DV + L], Us[h],
                                  preferred_element_type=F32) for h in range(H)]
        for h in range(H):
            S_scr[h] = egl[0:1, H + h:H + h + 1] * Ss[h] + lax.dot_general(
                kt_ref[rows, hs[h]], Us[h], (((0,), (0,)), ((), ())), preferred_element_type=F32)
        for h in range(H):
            gh = gate_ref[rows, hs[h]].astype(F32)
            o_ref[rows, hs[h]] = (_rms(os[h], onorm) * (gh * _sigmoid(gh))).astype(BF16)

    @pl.when(c == NS - 1)
    def _():
        Sout_ref[...] = S_scr[...]


def _gdn_mixer(main, small, conv_buf, S0, wc_t, par, onorm, nch=4, cb=8):
    B, T, _ = main.shape
    L = min(GDN_CHUNK, T)
    assert T % L == 0
    NC = T // L
    nch = min(nch, NC)
    cb = min(cb, NC)
    assert NC % nch == 0 and NC % cb == 0
    LA = L * nch
    tok = lambda n, j: pl.BlockSpec((None, n, 1024), lambda b, c: (b, c, j))
    sds = lambda dt: jax.ShapeDtypeStruct((B, T, 1024), dt)
    wk, uv, qt, kt, at, egl = pl.pallas_call(
        functools.partial(_gdn_prep_body, L, nch),
        grid=(B, NC // nch),
        in_specs=[pl.BlockSpec((None, LA, GDN_QKV), lambda b, c: (b, c, 0)),
                  pl.BlockSpec((None, 16, GDN_QKV), lambda b, c: (b, jnp.maximum(c * (LA // 16) - 1, 0), 0)),
                  pl.BlockSpec((None, LA, LANES), lambda b, c: (b, c, 0)),
                  pl.BlockSpec((None, CONV_W - 1, GDN_QKV), lambda b, c: (b, 0, 0)),
                  pl.BlockSpec((CONV_W, GDN_QKV), lambda b, c: (0, 0)),
                  pl.BlockSpec((2, LANES), lambda b, c: (0, 0))],
        out_specs=[tok(LA, 0), tok(LA, 0), tok(LA, 0), tok(LA, 0), tok(LA, 0),
                   pl.BlockSpec((None, nch, 8, LANES), lambda b, c: (b, c, 0, 0))],
        out_shape=[sds(BF16), sds(F32), sds(BF16), sds(BF16), sds(BF16),
                   jax.ShapeDtypeStruct((B, NC, 8, LANES), F32)],
        scratch_shapes=[pltpu.VMEM((LA + 8, GDN_QKV), F32)],
        compiler_params=_cparams(("parallel", "parallel")),
        name="gdn_prep",
    )(main, main, small, conv_buf, wc_t, par)
    NS = NC // cb
    state = pl.BlockSpec((None, GDN_HEADS, GDN_DK, GDN_DV), lambda b, c: (b, 0, 0, 0))
    o, S = pl.pallas_call(
        functools.partial(_gdn_scan_body, L, cb, NS),
        grid=(B, NS),
        in_specs=[tok(cb * L, 0), tok(cb * L, 0), tok(cb * L, 0), tok(cb * L, 0), tok(cb * L, 0),
                  pl.BlockSpec((None, cb, 8, LANES), lambda b, c: (b, c, 0, 0)),
                  tok(cb * L, 3), state,
                  pl.BlockSpec((1, GDN_DV), lambda b, c: (0, 0))],
        out_specs=[tok(cb * L, 0), state],
        out_shape=[sds(BF16), jax.ShapeDtypeStruct((B, GDN_HEADS, GDN_DK, GDN_DV), F32)],
        scratch_shapes=[pltpu.VMEM((GDN_HEADS, GDN_DK, GDN_DV), F32)],
        compiler_params=_cparams(("parallel", "arbitrary")),
        name="gdn_scan",
    )(wk, uv, qt, kt, at, egl, main, S0, onorm.reshape(1, GDN_DV))
    new_buf = main[:, T - (CONV_W - 1):, :GDN_QKV].astype(F32)
    return o, new_buf, S


ML_CHUNK = 64


def _mlstm_gates(small_ref, bias_ref):
    sc = GATE_CAP * jnp.tanh((small_ref[...] + bias_ref[...]) / GATE_CAP)
    return sc, jax.nn.log_sigmoid(sc)


def _mlstm_prep_body(L, nch, qk_ref, v_ref, small_ref, bias_ref, hl_ref, st_ref, kv_ref, ks_ref):
    ig_all, lf_all = _mlstm_gates(small_ref, bias_ref)
    incl, _, _ = _tri_masks(L)
    tril_b = jnp.where(incl, 1.0, 0.0).astype(BF16)
    lane = lax.broadcasted_iota(jnp.int32, (1, LANES), 1)
    H = ML_HEADS
    units = [(ci, h) for ci in range(nch) for h in range(H)]
    rows_of = lambda ci: slice(ci * L, (ci + 1) * L)
    q_of = lambda ci, h: qk_ref[rows_of(ci), h * ML_DQK:(h + 1) * ML_DQK]
    k_of = lambda ci, h: qk_ref[rows_of(ci), (H + h) * ML_DQK:(H + h + 1) * ML_DQK].astype(F32) * (ML_DQK ** -0.5)
    v_of = lambda ci, h: v_ref[rows_of(ci), h * ML_DV:(h + 1) * ML_DV]

    gates = []
    for ci in range(nch):
        b_all = _cumsum_rows(tril_b, lf_all[rows_of(ci)])
        igc = ig_all[rows_of(ci)]
        gates.append((b_all, igc, b_all.T, igc.T))
    qk = [_dot_nt(q_of(ci, h), k_of(ci, h)) for ci, h in units]
    Wl, dmaxs = [], []
    for (ci, h), s in zip(units, qk):
        b_all, igc, bT, igT = gates[ci]
        D = jnp.where(incl, b_all[:, H + h:H + h + 1] + (igT[h:h + 1, :] - bT[H + h:H + h + 1, :]), NEG_BIG)
        dmax = jnp.max(D, axis=-1, keepdims=True)
        dmaxs.append(dmax)
        Wl.append(jnp.exp(D - dmax) * s)
    for (ci, h), w in zip(units, Wl):
        hl_ref[rows_of(ci), h * ML_DV:(h + 1) * ML_DV] = _dot(w, v_of(ci, h))
    wks = []
    for (ci, h), dmax in zip(units, dmaxs):
        b_all, igc, _, _ = gates[ci]
        b_c = b_all[:, H + h:H + h + 1]
        wks.append(jnp.exp(b_c[L - 1:L, :] - b_c + igc[:, h:h + 1] - dmax[L - 1:L, :]) * k_of(ci, h))
    for (ci, h), wk in zip(units, wks):
        kv_ref[ci, h] = _dot_tn(wk, v_of(ci, h))
        ks_ref[ci, h:h + 1, :] = jnp.sum(wk, axis=0, keepdims=True)
    for ci in range(nch):
        stats = jnp.zeros((L, LANES), F32)
        for h in range(H):
            u = ci * H + h
            stats = jnp.where(lane == h, gates[ci][0][:, H + h:H + h + 1],
                              jnp.where(lane == H + h, dmaxs[u],
                                        jnp.where(lane == 2 * H + h, jnp.sum(Wl[u], axis=-1, keepdims=True), stats)))
        st_ref[rows_of(ci), :] = stats


def _mlstm_scan_body(L, cb, NS, q_ref, hl_ref, st_ref, kv_ref, ks_ref, op_ref, C0_ref, n0_ref, m0_ref, on_ref,
                     o_ref, Cout_ref, nout_ref, mout_ref, C_scr, n_scr, m_scr):
    c = pl.program_id(1)

    @pl.when(c == 0)
    def _():
        C_scr[...] = C0_ref[...]
        n_scr[...] = n0_ref[...]
        m_scr[...] = m0_ref[...]

    onorm = on_ref[...]
    H = ML_HEADS
    units = [(ci, h) for ci in range(cb) for h in range(H)]
    rows_of = lambda ci: slice(ci * L, (ci + 1) * L)
    sts = [st_ref[rows_of(ci), :] for ci in range(cb)]
    pre = {}
    for h in range(H):
        m_prev = m_scr[h:h + 1, 0:1]
        n_prev = n_scr[h:h + 1, :]
        C = C_scr[h]
        for ci in range(cb):
            b_c = sts[ci][:, h:h + 1]
            dmax = sts[ci][:, H + h:H + h + 1]
            mt = jnp.maximum(b_c + m_prev, dmax)
            pre[(ci, h)] = (C, n_prev, m_prev, mt)
            mL = mt[L - 1:L, :]
            fL = jnp.exp(b_c[L - 1:L, :] + m_prev - mL)
            gL = jnp.exp(dmax[L - 1:L, :] - mL)
            C = fL * C + gL * kv_ref[ci, h]
            n_prev = fL * n_prev + gL * ks_ref[ci, h:h + 1, :]
            m_prev = mL
        C_scr[h] = C
        n_scr[h:h + 1, :] = n_prev
        m_scr[h:h + 1, :] = jnp.broadcast_to(m_prev, (1, LANES))

    qs = {u: q_ref[rows_of(u[0]), u[1] * ML_DQK:(u[1] + 1) * ML_DQK].astype(F32) for u in units}
    qC = {u: _dot(qs[u], pre[u][0]) for u in units}
    hs = {}
    for u in units:
        ci, h = u
        _, n_prev, m_prev, mt = pre[u]
        b_c = sts[ci][:, h:h + 1]
        dmax = sts[ci][:, H + h:H + h + 1]
        sw = sts[ci][:, 2 * H + h:2 * H + h + 1]
        e_i = jnp.exp(b_c + m_prev - mt)
        e_l = jnp.exp(dmax - mt)
        num = e_i * qC[u] + e_l * hl_ref[rows_of(ci), h * ML_DV:(h + 1) * ML_DV]
        den = e_i * jnp.sum(qs[u] * n_prev, axis=-1, keepdims=True) + e_l * sw
        hs[u] = num / jnp.maximum(jnp.abs(den), jnp.exp(-mt))
    for u in units:
        ci, h = u
        og = op_ref[rows_of(ci), h * ML_DV:(h + 1) * ML_DV].astype(F32)
        o_ref[rows_of(ci), h * ML_DV:(h + 1) * ML_DV] = (_sigmoid(og) * _rms(hs[u], onorm)).astype(BF16)

    @pl.when(c == NS - 1)
    def _():
        Cout_ref[...] = C_scr[...]
        nout_ref[...] = n_scr[...]
        mout_ref[...] = m_scr[...]


def _mlstm_mixer(main, small, C0, n0, m0, bias, onorm, nch=4, cb=4):
    B, T, _ = main.shape
    L = min(ML_CHUNK, T)
    assert T % L == 0
    NC = T // L
    nch = min(nch, NC)
    cb = min(cb, NC)
    assert NC % nch == 0 and NC % cb == 0
    LA = L * nch
    H = ML_HEADS
    hl, st, kv, ks = pl.pallas_call(
        functools.partial(_mlstm_prep_body, L, nch),
        grid=(B, NC // nch),
        in_specs=[pl.BlockSpec((None, LA, 1024), lambda b, c: (b, c, 0)),
                  pl.BlockSpec((None, LA, 1024), lambda b, c: (b, c, 1)),
                  pl.BlockSpec((None, LA, LANES), lambda b, c: (b, c, 0)),
                  pl.BlockSpec((1, LANES), lambda b, c: (0, 0))],
        out_specs=[pl.BlockSpec((None, LA, 1024), lambda b, c: (b, c, 0)),
                   pl.BlockSpec((None, LA, LANES), lambda b, c: (b, c, 0)),
                   pl.BlockSpec((None, nch, H, ML_DQK, ML_DV), lambda b, c: (b, c, 0, 0, 0)),
                   pl.BlockSpec((None, nch, H, ML_DQK), lambda b, c: (b, c, 0, 0))],
        out_shape=[jax.ShapeDtypeStruct((B, T, 1024), F32),
                   jax.ShapeDtypeStruct((B, T, LANES), F32),
                   jax.ShapeDtypeStruct((B, NC, H, ML_DQK, ML_DV), F32),
                   jax.ShapeDtypeStruct((B, NC, H, ML_DQK), F32)],
        compiler_params=_cparams(("parallel", "parallel")),
        name="mlstm_prep",
    )(main, main, small, bias)
    NS = NC // cb
    m0b = jnp.broadcast_to(m0[:, :, None], (B, H, LANES))
    st_C = pl.BlockSpec((None, H, ML_DQK, ML_DV), lambda b, c: (b, 0, 0, 0))
    st_n = pl.BlockSpec((None, H, ML_DQK), lambda b, c: (b, 0, 0))
    st_m = pl.BlockSpec((None, H, LANES), lambda b, c: (b, 0, 0))
    return pl.pallas_call(
        functools.partial(_mlstm_scan_body, L, cb, NS),
        grid=(B, NS),
        in_specs=[pl.BlockSpec((None, cb * L, H * ML_DQK), lambda b, c: (b, c, 0)),
                  pl.BlockSpec((None, cb * L, 1024), lambda b, c: (b, c, 0)),
                  pl.BlockSpec((None, cb * L, LANES), lambda b, c: (b, c, 0)),
                  pl.BlockSpec((None, cb, H, ML_DQK, ML_DV), lambda b, c: (b, c, 0, 0, 0)),
                  pl.BlockSpec((None, cb, H, ML_DQK), lambda b, c: (b, c, 0, 0)),
                  pl.BlockSpec((None, cb * L, 1024), lambda b, c: (b, c, 2)),
                  st_C, st_n, st_m,
                  pl.BlockSpec((1, ML_DV), lambda b, c: (0, 0))],
        out_specs=[pl.BlockSpec((None, cb * L, 1024), lambda b, c: (b, c, 0)), st_C, st_n, st_m],
        out_shape=[jax.ShapeDtypeStruct((B, T, 1024), BF16),
                   jax.ShapeDtypeStruct((B, H, ML_DQK, ML_DV), F32),
                   jax.ShapeDtypeStruct((B, H, ML_DQK), F32),
                   jax.ShapeDtypeStruct((B, H, LANES), F32)],
        scratch_shapes=[pltpu.VMEM((H, ML_DQK, ML_DV), F32),
                        pltpu.VMEM((H, ML_DQK), F32),
                        pltpu.VMEM((H, LANES), F32)],
        compiler_params=_cparams(("parallel", "arbitrary")),
        name="mlstm_scan",
    )(main, hl, st, kv, ks, main, C0, n0, m0b, onorm.reshape(1, ML_DV))


FOX_BIAS_LANE = FOX_DH


def _fox_prep_body(q_ref, k_ref, v_ref, small_ref, bf_ref, qn_ref, kn_ref,
                   qh_ref, k32_ref, v32_ref, lf_ref):
    r = lax.broadcasted_iota(jnp.int32, (LANES, LANES), 0) // FOX_DH
    cc = lax.broadcasted_iota(jnp.int32, (LANES, LANES), 1) // FOX_DH
    avg = jnp.where(r == cc, 1.0 / FOX_DH, 0.0).astype(BF16)

    def head_rms(x, gain):
        parts = []
        for t in range(x.shape[1] // LANES):
            xt = x[:, t * LANES:(t + 1) * LANES]
            hi, lo = _split2(xt * xt)
            ms = (jnp.dot(hi, avg, preferred_element_type=F32)
                  + jnp.dot(lo, avg, preferred_element_type=F32))
            parts.append(xt * lax.rsqrt(ms + EPS))
        return jnp.concatenate(parts, axis=-1) * gain

    qn = head_rms(q_ref[...].astype(F32), qn_ref[...]) * (LOG2E * FOX_DH ** -0.5)
    lane = lax.broadcasted_iota(jnp.int32, (1, LANES), 1)
    ones3 = jnp.where(lane < FOX_BIAS_LANE + 3, 1.0, 0.0)
    for h in range(FOX_HEADS):
        xt = qn[:, (h // 2) * LANES:(h // 2 + 1) * LANES]
        if h % 2:
            xt = pltpu.roll(xt, FOX_DH, axis=1)
        qh_ref[h] = jnp.where(lane < FOX_DH, xt, ones3).astype(BF16)
    k32_ref[...] = head_rms(k_ref[...].astype(F32), kn_ref[...])
    v32_ref[...] = v_ref[...].astype(F32)
    lf_ref[...] = jax.nn.log_sigmoid(small_ref[...] + bf_ref[...])


def _fox_prep(main, small, bf_row, qn_row, kn_row, tm=512):
    B, T, _ = main.shape
    tm = min(tm, T)
    assert T % tm == 0
    blk = lambda j: pl.BlockSpec((None, tm, 1024), lambda b, i: (b, i, j))
    row = lambda n: pl.BlockSpec((1, n), lambda b, i: (0, 0))
    o1024 = pl.BlockSpec((None, tm, 1024), lambda b, i: (b, i, 0))
    return pl.pallas_call(
        _fox_prep_body,
        grid=(B, T // tm),
        in_specs=[blk(0), blk(1), blk(2),
                  pl.BlockSpec((None, tm, LANES), lambda b, i: (b, i, 0)),
                  row(LANES), row(1024), row(1024)],
        out_specs=[pl.BlockSpec((None, FOX_HEADS, tm, LANES), lambda b, i: (b, 0, i, 0)),
                   o1024, o1024,
                   pl.BlockSpec((None, tm, LANES), lambda b, i: (b, i, 0))],
        out_shape=[jax.ShapeDtypeStruct((B, FOX_HEADS, T, LANES), BF16),
                   jax.ShapeDtypeStruct((B, T, 1024), F32),
                   jax.ShapeDtypeStruct((B, T, 1024), F32),
                   jax.ShapeDtypeStruct((B, T, LANES), F32)],
        compiler_params=_cparams(("parallel", "parallel")),
        name="fox_prep",
    )(main, main, main, small, bf_row, qn_row, kn_row)


def _fox_kv_body(tm, n_past, *refs):
    if n_past:
        kp_ref, vp_ref, lp_ref, k_ref, v_ref, lf_ref, kh_ref, vh_ref, carry = refs
    else:
        k_ref, v_ref, lf_ref, kh_ref, vh_ref, carry = refs
    i = pl.program_id(1)

    @pl.when(i == 0)
    def _():
        carry[...] = jnp.zeros_like(carry)

    k = k_ref[...]
    v = v_ref[...]
    lf = lf_ref[...]
    if n_past:
        past = i < n_past
        k = jnp.where(past, kp_ref[...], k)
        v = jnp.where(past, vp_ref[...], v)
        lf = jnp.where(past, lp_ref[...], lf)
    incl, _, _ = _tri_masks(tm)
    tril_b = jnp.where(incl, 1.0, 0.0).astype(BF16)
    F = _cumsum_rows(tril_b, lf) + carry[0:1, :]
    carry[0:1, :] = F[tm - 1:tm, :]
    nb = -LOG2E * F
    b1 = nb.astype(BF16).astype(F32)
    r1 = nb - b1
    b2 = r1.astype(BF16).astype(F32)
    b3 = r1 - b2
    lane = lax.broadcasted_iota(jnp.int32, (1, LANES), 1)
    ones1 = jnp.where(lane == FOX_BIAS_LANE, 1.0, 0.0)
    for h in range(FOX_HEADS):
        kt = k[:, (h // 2) * LANES:(h // 2 + 1) * LANES]
        vt = v[:, (h // 2) * LANES:(h // 2 + 1) * LANES]
        if h % 2:
            kt = pltpu.roll(kt, FOX_DH, axis=1)
            vt = pltpu.roll(vt, FOX_DH, axis=1)
        bias = jnp.where(lane == FOX_BIAS_LANE, b1[:, h:h + 1],
                         jnp.where(lane == FOX_BIAS_LANE + 1, b2[:, h:h + 1],
                                   jnp.where(lane == FOX_BIAS_LANE + 2, b3[:, h:h + 1], 0.0)))
        kh_ref[h] = jnp.where(lane < FOX_DH, kt, bias).astype(BF16)
        vh_ref[h] = jnp.where(lane < FOX_DH, vt, ones1).astype(BF16)


def _fox_kv(k_new, v_new, lf_new, past=None, tm=512):
    B, T, _ = k_new.shape
    if past is None:
        tm = min(tm, T)
        assert T % tm == 0
        n_past, n_new = 0, T // tm
        args, specs = [], []
    else:
        k_past, v_past, lf_past = past
        P = k_past.shape[1]
        tm = min(tm, P)
        assert P % tm == 0 and T <= tm
        n_past, n_new = P // tm, 1
        pad = lambda a: jnp.pad(a, ((0, 0), (0, tm - T), (0, 0)))
        k_new, v_new, lf_new = pad(k_new), pad(v_new), pad(lf_new)
        clamp = lambda b, i: (b, jnp.minimum(i, n_past - 1), 0)
        args = [k_past, v_past, lf_past]
        specs = [pl.BlockSpec((None, tm, 1024), clamp), pl.BlockSpec((None, tm, 1024), clamp),
                 pl.BlockSpec((None, tm, LANES), clamp)]
    new_idx = lambda b, i: (b, jnp.maximum(i - n_past, 0), 0)
    N = (n_past + n_new) * tm
    hm = pl.BlockSpec((None, FOX_HEADS, tm, LANES), lambda b, i: (b, 0, i, 0))
    return pl.pallas_call(
        functools.partial(_fox_kv_body, tm, n_past),
        grid=(B, n_past + n_new),
        in_specs=specs + [pl.BlockSpec((None, tm, 1024), new_idx),
                          pl.BlockSpec((None, tm, 1024), new_idx),
                          pl.BlockSpec((None, tm, LANES), new_idx)],
        out_specs=[hm, hm],
        out_shape=[jax.ShapeDtypeStruct((B, FOX_HEADS, N, LANES), BF16),
                   jax.ShapeDtypeStruct((B, FOX_HEADS, N, LANES), BF16)],
        scratch_shapes=[pltpu.VMEM((8, LANES), F32)],
        compiler_params=_cparams(("parallel", "arbitrary")),
        name="fox_kv",
    )(*args, k_new, v_new, lf_new)


def _flash_body(P, tq, tk, rq, hp, qi_tab, kj_tab, q_ref, k_ref, v_ref, gate_ref, o_ref, m_scr, acc_scr):
    s_idx = pl.program_id(2)
    qi = qi_tab[s_idx]
    kj = kj_tab[s_idx]
    last_kj = (P + (qi + 1) * tq - 1) // tk
    aligned = tq == tk and P % tk == 0

    @pl.when(kj == 0)
    def _():
        m_scr[...] = jnp.full(m_scr.shape, NEG_BIG, F32)
        acc_scr[...] = jnp.zeros_like(acc_scr)

    def step(masked):
        units = [(hh, r) for r in range(tq // rq) for hh in range(2 * hp)]

        def logits(u):
            hh, r = u
            rows = slice(r * rq, (r + 1) * rq)
            nk = min(tk, (r + 1) * rq) if (masked and aligned) else tk
            s = lax.dot_general(q_ref[hh, rows, :], k_ref[hh, 0:nk, :], (((1,), (1,)), ((), ())),
                                preferred_element_type=F32)
            if masked:
                qpos = P + qi * tq + r * rq + lax.broadcasted_iota(jnp.int32, (rq, nk), 0)
                kpos = kj * tk + lax.broadcasted_iota(jnp.int32, (rq, nk), 1)
                s = jnp.where(kpos <= qpos, s, NEG_BIG)
            return s, nk

        def probs(u, s):
            hh, r = u
            rows = slice(r * rq, (r + 1) * rq)
            m_prev = m_scr[hh, rows, :][:, 0:1]
            m_new = jnp.maximum(m_prev, jnp.max(s, axis=-1, keepdims=True))
            m_scr[hh, rows, :] = jnp.broadcast_to(m_new, (rq, LANES))
            return jnp.exp2(m_prev - m_new), jnp.exp2(s - m_new).astype(BF16)

        def accumulate(u, alpha, p, nk):
            hh, r = u
            rows = slice(r * rq, (r + 1) * rq)
            acc_scr[hh, rows, :] = alpha * acc_scr[hh, rows, :] + jnp.dot(
                p, v_ref[hh, 0:nk, :], preferred_element_type=F32)

        n = len(units)
        pend_s = {0: logits(units[0])}
        pend_p = {}
        for i in range(n):
            if i + 1 < n:
                pend_s[i + 1] = logits(units[i + 1])
            s, nk = pend_s.pop(i)
            pend_p[i] = probs(units[i], s) + (nk,)
            if i >= 1:
                accumulate(units[i - 1], *pend_p.pop(i - 1))
        accumulate(units[n - 1], *pend_p.pop(n - 1))

    needs_mask = (kj + 1) * tk - 1 > P + qi * tq

    @pl.when(needs_mask)
    def _():
        step(True)

    @pl.when(jnp.logical_not(needs_mask))
    def _():
        step(False)

    @pl.when(kj == last_kj)
    def _():
        lane = lax.broadcasted_iota(jnp.int32, (1, LANES), 1)
        for pp in range(hp):
            a0 = acc_scr[2 * pp]
            a1 = acc_scr[2 * pp + 1]
            o0 = a0 * (1.0 / a0[:, FOX_BIAS_LANE:FOX_BIAS_LANE + 1])
            o1 = a1 * (1.0 / a1[:, FOX_BIAS_LANE:FOX_BIAS_LANE + 1])
            o = jnp.where(lane < FOX_DH, o0, pltpu.roll(o1, FOX_DH, axis=1))
            g = gate_ref[:, pp * LANES:(pp + 1) * LANES].astype(F32)
            o_ref[:, pp * LANES:(pp + 1) * LANES] = (o * _sigmoid(g)).astype(BF16)


def _fox_flash(qh, kh, vh, gate_src, P, tq, tk, rq, hp=1):
    B, _, T, _ = qh.shape
    N = kh.shape[2]
    assert T % tq == 0 and N % tk == 0 and tq % rq == 0 and (FOX_HEADS // 2) % hp == 0
    nq = T // tq
    pairs = [(qi, kj) for qi in range(nq) for kj in range((P + (qi + 1) * tq - 1) // tk + 1)]
    qi_tab = jnp.asarray(np.array([p[0] for p in pairs], np.int32))
    kj_tab = jnp.asarray(np.array([p[1] for p in pairs], np.int32))
    HP = FOX_HEADS // (2 * hp)
    gate_blk0 = 3 * (1024 // (LANES * hp))
    grid_spec = pltpu.PrefetchScalarGridSpec(
        num_scalar_prefetch=2,
        grid=(B, HP, len(pairs)),
        in_specs=[pl.BlockSpec((None, 2 * hp, tq, LANES), lambda b, p, s, qt, kt: (b, p, qt[s], 0)),
                  pl.BlockSpec((None, 2 * hp, tk, LANES), lambda b, p, s, qt, kt: (b, p, kt[s], 0)),
                  pl.BlockSpec((None, 2 * hp, tk, LANES), lambda b, p, s, qt, kt: (b, p, kt[s], 0)),
                  pl.BlockSpec((None, tq, LANES * hp), lambda b, p, s, qt, kt: (b, qt[s], gate_blk0 + p))],
        out_specs=pl.BlockSpec((None, tq, LANES * hp), lambda b, p, s, qt, kt: (b, qt[s], p)),
        scratch_shapes=[pltpu.VMEM((2 * hp, tq, LANES), F32),
                        pltpu.VMEM((2 * hp, tq, LANES), F32)],
    )
    return pl.pallas_call(
        functools.partial(_flash_body, P, tq, tk, rq, hp),
        grid_spec=grid_spec,
        out_shape=jax.ShapeDtypeStruct((B, T, 1024), BF16),
        compiler_params=_cparams(("parallel", "parallel", "arbitrary")),
        name="fox_flash",
    )(qi_tab, kj_tab, qh, kh, vh, gate_src)


def _pad_lanes(a, offset=0):
    n = a.shape[-1]
    pads = [(0, 0)] * (a.ndim - 1) + [(offset, LANES - n - offset)]
    return jnp.pad(a, pads)


def _trunk(x, gdn_conv, gdn_S, ml_C, ml_n, ml_m, fox_k, fox_v, fox_lf, mem_k, mem_v, W):
    B, T, D = x.shape
    M = B * T
    x2 = x.reshape(M, D)
    conv_o, S_o, C_o, n_o, m_o, k_o, v_o, lf_o = [], [], [], [], [], [], [], []
    for layer in range(DEPTH):
        kind, j = layer % N_MIXERS, layer // N_MIXERS
        gains = W["norm_gains"][layer]
        if kind == 0:
            main, small = _prenorm_mm(x2, gains[0], W["gdn_w_in"], j, GDN_MAIN, W["gdn_w_small"], out_dtype=F32)
            o, buf, S = _gdn_mixer(main.reshape(B, T, GDN_MAIN), small.reshape(B, T, LANES),
                                   gdn_conv[j], gdn_S[j], W["gdn_wc_t"][j], W["gdn_par"][j],
                                   W["gdn_onorm"][j])
            conv_o.append(buf)
            S_o.append(S)
            w_out = W["gdn_w_out"]
        elif kind == 1:
            main, small = _prenorm_mm(x2, gains[0], W["ml_w_in"], j, ML_MAIN, W["ml_w_small"])
            o, C, n, m = _mlstm_mixer(main.reshape(B, T, ML_MAIN), small.reshape(B, T, LANES),
                                      ml_C[j], ml_n[j], ml_m[j], W["ml_bias"][j], W["ml_onorm"][j])
            C_o.append(C)
            n_o.append(n)
            m_o.append(m[:, :, 0])
            w_out = W["ml_w_out"]
        else:
            main, small = _prenorm_mm(x2, gains[0], W["fox_w_in"], j, FOX_MAIN, W["fox_w_small"])
            main3 = main.reshape(B, T, FOX_MAIN)
            qh, k32, v32, lfp = _fox_prep(main3, small.reshape(B, T, LANES), W["fox_bf"][j],
                                          W["fox_qn"][j], W["fox_kn"][j])
            P = fox_k[j].shape[1]
            if P == 0:
                kh, vh = _fox_kv(k32, v32, lfp)
                tq, tk, rq, hp = min(1024, T), min(1024, T), 256, 1
            else:
                kh, vh = _fox_kv(k32, v32, lfp, past=(fox_k[j].reshape(B, P, 1024), fox_v[j].reshape(B, P, 1024),
                                                      _pad_lanes(fox_lf[j])), tm=256)
                tq, tk, rq, hp = T, kh.shape[2], T, 4
            o = _fox_flash(qh, kh, vh, main3, P, tq, tk, rq, hp)
            k_o.append(k32.reshape(B, T, FOX_HEADS, FOX_DH))
            v_o.append(v32.reshape(B, T, FOX_HEADS, FOX_DH))
            lf_o.append(lfp[:, :, :FOX_HEADS])
            w_out = W["fox_w_out"]
        x2 = _memattn(o, w_out, j, gains[1], x2.reshape(B, T, D), gains[2], W["x_w_q"],
                      mem_k[layer], mem_v[layer], W["x_w_o"], layer, gains[3]).reshape(M, D)
        x2 = _ffn(x2, gains[4], W["ffn_w_gu"], W["ffn_w_down"], layer, gains[5])
    return (x2.reshape(B, T, D), jnp.stack(conv_o), jnp.stack(S_o), jnp.stack(C_o), jnp.stack(n_o),
            jnp.stack(m_o), jnp.stack(k_o), jnp.stack(v_o), jnp.stack(lf_o))


def kernel(x_prompt, x_sample, state_gdn_conv, state_gdn_S, state_mlstm_C, state_mlstm_n, state_mlstm_m, cache_fox_k, cache_fox_v, cache_fox_logf, cache_mem_k, cache_mem_v, mem_prompt, norm_gains, gdn_w_in, gdn_w_conv, gdn_a_log, gdn_dt_bias, gdn_onorm, gdn_w_out, ml_w_in, ml_b_i, ml_b_f, ml_onorm, ml_w_out, fox_w_in, fox_b_f, fox_qk_norm, fox_w_out, x_w_q, x_w_kv, x_w_o, ffn_w_gu, ffn_w_down):
    B = x_prompt.shape[0]
    DB = x_sample.shape[0]
    n_a, n_b, n_c = gdn_w_in.shape[0], ml_w_in.shape[0], fox_w_in.shape[0]
    W = {
        "norm_gains": norm_gains,
        "gdn_w_in": gdn_w_in.astype(BF16),
        "gdn_w_small": _pad_lanes(gdn_w_in[:, :, GDN_MAIN:]).astype(BF16),
        "gdn_wc_t": jnp.transpose(gdn_w_conv, (0, 2, 1)),
        "gdn_par": jnp.stack([_pad_lanes(gdn_a_log, GDN_HEADS), _pad_lanes(gdn_dt_bias, GDN_HEADS)], axis=1),
        "gdn_onorm": gdn_onorm,
        "gdn_w_out": gdn_w_out.astype(BF16),
        "ml_w_in": ml_w_in.astype(BF16),
        "ml_w_small": _pad_lanes(ml_w_in[:, :, ML_MAIN:]).astype(BF16),
        "ml_bias": _pad_lanes(jnp.concatenate([ml_b_i, ml_b_f], axis=-1))[:, None, :],
        "ml_onorm": ml_onorm,
        "ml_w_out": ml_w_out.astype(BF16),
        "fox_w_in": fox_w_in.astype(BF16),
        "fox_w_small": _pad_lanes(fox_w_in[:, :, FOX_MAIN:]).astype(BF16),
        "fox_bf": _pad_lanes(fox_b_f)[:, None, :],
        "fox_qn": jnp.tile(fox_qk_norm[:, 0, :], (1, FOX_HEADS))[:, None, :],
        "fox_kn": jnp.tile(fox_qk_norm[:, 1, :], (1, FOX_HEADS))[:, None, :],
        "fox_w_out": fox_w_out.astype(BF16),
        "x_w_q": x_w_q.astype(BF16),
        "x_w_o": x_w_o.astype(BF16),
        "ffn_w_gu": ffn_w_gu.astype(BF16),
        "ffn_w_down": ffn_w_down.astype(BF16),
    }

    mem2 = mem_prompt.reshape(B * N_MEM, D_MODEL)
    x_w_kv_b = x_w_kv.astype(BF16)
    kv = [_prenorm_mm(mem2, norm_gains[l, 6], x_w_kv_b, l, 2 * D_MODEL, out_dtype=F32).reshape(B, N_MEM, 2 * D_MODEL)
          for l in range(DEPTH)]
    p_mem_k = jnp.stack([a[:, :, :D_MODEL].reshape(B, N_MEM, X_HEADS, X_DH) for a in kv])
    p_mem_v = jnp.stack([a[:, :, D_MODEL:].reshape(B, N_MEM, X_HEADS, X_DH) for a in kv])

    cmk = cache_mem_k.reshape(DEPTH * DB, N_MEM, D_MODEL)
    cmv = cache_mem_v.reshape(DEPTH * DB, N_MEM, D_MODEL)
    zeros = lambda *s: jnp.zeros(s, F32)
    prompt = _trunk(
        x_prompt,
        zeros(n_a, B, CONV_W - 1, GDN_QKV), zeros(n_a, B, GDN_HEADS, GDN_DK, GDN_DV),
        zeros(n_b, B, ML_HEADS, ML_DQK, ML_DV), zeros(n_b, B, ML_HEADS, ML_DQK), zeros(n_b, B, ML_HEADS),
        zeros(n_c, B, 0, FOX_HEADS, FOX_DH), zeros(n_c, B, 0, FOX_HEADS, FOX_DH), zeros(n_c, B, 0, FOX_HEADS),
        [(a, 0, 0) for a in kv], [(a, 0, 1) for a in kv], W)
    sample = _trunk(
        x_sample, state_gdn_conv, state_gdn_S, state_mlstm_C, state_mlstm_n, state_mlstm_m,
        cache_fox_k, cache_fox_v, cache_fox_logf,
        [(cmk, l * DB, 0) for l in range(DEPTH)], [(cmv, l * DB, 0) for l in range(DEPTH)], W)
    (y_p, p_conv, p_S, p_C, p_n, p_m, p_k, p_v, p_lf) = prompt
    (y_s, s_conv, s_S, s_C, s_n, s_m, s_k, s_v, s_lf) = sample
    return (y_p, y_s, p_conv, p_S, p_C, p_n, p_m, p_k, p_v, p_lf, p_mem_k, p_mem_v,
            s_conv, s_S, s_C, s_n, s_m, s_k, s_v, s_lf)
```

```python
import functools

import numpy as np
import jax
import jax.numpy as jnp
from jax import lax
from jax.experimental import pallas as pl
from jax.experimental.pallas import tpu as pltpu

F32 = jnp.float32
BF16 = jnp.bfloat16

D_MODEL = 1024
DEPTH = 4
N_MIXERS = 3
EPS = 1e-6
CONV_W = 4

GDN_HEADS = 8
GDN_DK = 128
GDN_DV = 128
GDN_QKV = 3072
GDN_MAIN = 4096

ML_HEADS = 4
ML_DQK = 128
ML_DV = 256
ML_MAIN = 3072
GATE_CAP = 15.0

FOX_HEADS = 16
FOX_DH = 64
FOX_MAIN = 4096

N_MEM = 256
X_HEADS = 4
X_DH = 256
D_FF = 2816

LANES = 128
NEG_BIG = -1e30
LOG2E = 1.4426950408889634

VMEM_LIMIT = 48 * 1024 * 1024


def _wspec(li, blk, imap):
    return pl.BlockSpec((None,) + blk, lambda *g: (li,) + imap(*g))


def _cparams(sem):
    return pltpu.CompilerParams(dimension_semantics=sem, vmem_limit_bytes=VMEM_LIMIT)


def _rms(x, g):
    return x * lax.rsqrt(jnp.mean(x * x, axis=-1, keepdims=True) + EPS) * g


def _sigmoid(x):
    return 1.0 / (1.0 + jnp.exp(-x))


def _dot(a, b):
    return jnp.dot(a.astype(BF16), b.astype(BF16), preferred_element_type=F32)


def _dot_nt(a, b):
    return lax.dot_general(a.astype(BF16), b.astype(BF16), (((1,), (1,)), ((), ())),
                           preferred_element_type=F32)


def _dot_tn(a, b):
    return lax.dot_general(a.astype(BF16), b.astype(BF16), (((0,), (0,)), ((), ())),
                           preferred_element_type=F32)


def _split2(x):
    hi = x.astype(BF16)
    lo = (x - hi.astype(F32)).astype(BF16)
    return hi, lo


def _dot_x3(a, b):
    ah, al = _split2(a)
    bh, bl = _split2(b)
    d = lambda u, v: jnp.dot(u, v, preferred_element_type=F32)
    return d(ah, bh) + (d(ah, bl) + d(al, bh))


def _cumsum_rows(tril_b, x):
    h = x.astype(BF16)
    r = x - h.astype(F32)
    m = r.astype(BF16)
    l = (r - m.astype(F32)).astype(BF16)
    d = lambda v: jnp.dot(tril_b, v, preferred_element_type=F32)
    return d(h) + (d(m) + d(l))


def _tri_masks(L):
    row = lax.broadcasted_iota(jnp.int32, (L, L), 0)
    col = lax.broadcasted_iota(jnp.int32, (L, L), 1)
    return row >= col, row > col, row == col


def _prenorm_mm_body(has_small, *refs):
    if has_small:
        x_ref, g_ref, w_ref, ws_ref, o_ref, os_ref, xn_ref = refs
    else:
        x_ref, g_ref, w_ref, o_ref, xn_ref = refs

    @pl.when(pl.program_id(1) == 0)
    def _():
        xb = _rms(x_ref[...], g_ref[...]).astype(BF16)
        xn_ref[...] = xb
        if has_small:
            os_ref[...] = jnp.dot(xb, ws_ref[...], preferred_element_type=F32)

    o_ref[...] = jnp.dot(xn_ref[...], w_ref[...], preferred_element_type=F32).astype(o_ref.dtype)


def _prenorm_mm(x, gain, w_main, li, n_out, w_small=None, out_dtype=BF16, tm=1024, tn=1024):
    M, D = x.shape
    N = n_out
    tm = min(tm, M)
    tn = min(tn, N)
    assert M % tm == 0 and N % tn == 0
    has_small = w_small is not None
    in_specs = [pl.BlockSpec((tm, D), lambda i, j: (i, 0)),
                pl.BlockSpec((1, D), lambda i, j: (0, 0)),
                _wspec(li, (D, tn), lambda i, j: (0, j))]
    out_specs = [pl.BlockSpec((tm, tn), lambda i, j: (i, j))]
    out_shape = [jax.ShapeDtypeStruct((M, N), out_dtype)]
    args = [x, gain.reshape(1, D), w_main]
    if has_small:
        in_specs.append(_wspec(li, (D, LANES), lambda i, j: (0, 0)))
        out_specs.append(pl.BlockSpec((tm, LANES), lambda i, j: (i, 0)))
        out_shape.append(jax.ShapeDtypeStruct((M, LANES), F32))
        args.append(w_small)
    res = pl.pallas_call(
        functools.partial(_prenorm_mm_body, has_small),
        grid=(M // tm, N // tn),
        in_specs=in_specs, out_specs=out_specs, out_shape=out_shape,
        scratch_shapes=[pltpu.VMEM((tm, D), BF16)],
        compiler_params=_cparams(("parallel", "arbitrary")),
        name="prenorm_mm",
    )(*args)
    return res if has_small else res[0]


def _memattn_body(nb, o_ref, wout_ref, g1_ref, x_ref, g2_ref, wq_ref, mk_ref, mv_ref, wo_ref, g3_ref, y_ref):
    tq, D = x_ref.shape[1], x_ref.shape[2]
    R = nb * tq
    o_in = o_ref[...].reshape(R, o_ref.shape[2])
    x = x_ref[...].reshape(R, D)
    x = x + _rms(jnp.dot(o_in, wout_ref[...], preferred_element_type=F32), g1_ref[...])
    h = _rms(x, g2_ref[...]).astype(BF16)
    q = jnp.dot(h, wq_ref[...], preferred_element_type=F32).astype(BF16)
    rows = []
    for bi in range(nb):
        mk = mk_ref[bi].astype(BF16)
        mv = mv_ref[bi].astype(BF16)
        qb = q[bi * tq:(bi + 1) * tq]
        outs = []
        for hd in range(X_HEADS):
            sl = slice(hd * X_DH, (hd + 1) * X_DH)
            s = _dot_nt(qb[:, sl], mk[:, sl]) * (X_DH ** -0.5)
            p = jnp.exp(s - jnp.max(s, axis=-1, keepdims=True))
            p = p * (1.0 / jnp.sum(p, axis=-1, keepdims=True))
            outs.append(jnp.dot(p.astype(BF16), mv[:, sl], preferred_element_type=F32))
        rows.append(jnp.concatenate(outs, axis=-1).astype(BF16))
    o = rows[0] if nb == 1 else jnp.concatenate(rows, axis=0)
    y = jnp.dot(o, wo_ref[...], preferred_element_type=F32)
    y_ref[...] = (x + _rms(y, g3_ref[...])).reshape(nb, tq, D)


def _memattn(o, w_out, lo, g1, x, g2, wq, mem_k, mem_v, wo, lx, g3, tq=512, rows_per_step=256):
    B, T, D = x.shape
    tq = min(tq, T)
    assert T % tq == 0
    nb = max(1, min(B, rows_per_step // tq)) if tq == T else 1
    assert B % nb == 0
    HD = X_HEADS * X_DH
    K = o.shape[-1]
    (mk_arr, koff, kcol), (mv_arr, voff, vcol) = mem_k, mem_v
    assert koff % nb == 0 and voff % nb == 0
    return pl.pallas_call(
        functools.partial(_memattn_body, nb),
        grid=(B // nb, T // tq),
        in_specs=[pl.BlockSpec((nb, tq, K), lambda b, i: (b, i, 0)),
                  _wspec(lo, (K, D), lambda b, i: (0, 0)),
                  pl.BlockSpec((1, D), lambda b, i: (0, 0)),
                  pl.BlockSpec((nb, tq, D), lambda b, i: (b, i, 0)),
                  pl.BlockSpec((1, D), lambda b, i: (0, 0)),
                  _wspec(lx, (D, HD), lambda b, i: (0, 0)),
                  pl.BlockSpec((nb, N_MEM, HD), lambda b, i: (koff // nb + b, 0, kcol)),
                  pl.BlockSpec((nb, N_MEM, HD), lambda b, i: (voff // nb + b, 0, vcol)),
                  _wspec(lx, (HD, D), lambda b, i: (0, 0)),
                  pl.BlockSpec((1, D), lambda b, i: (0, 0))],
        out_specs=pl.BlockSpec((nb, tq, D), lambda b, i: (b, i, 0)),
        out_shape=jax.ShapeDtypeStruct((B, T, D), F32),
        compiler_params=_cparams(("parallel", "parallel")),
        name="memattn",
    )(o, w_out, g1.reshape(1, D), x, g2.reshape(1, D), wq, mk_arr, mv_arr, wo, g3.reshape(1, D))


def _ffn_body(nf, x_ref, g4_ref, wg_ref, wu_ref, wd_ref, g5_ref, y_ref, xn_ref, acc_ref):
    f = pl.program_id(1)

    @pl.when(f == 0)
    def _():
        xn_ref[...] = _rms(x_ref[...], g4_ref[...]).astype(BF16)
        acc_ref[...] = jnp.zeros_like(acc_ref)

    xn = xn_ref[...]
    g = jnp.dot(xn, wg_ref[...], preferred_element_type=F32)
    u = jnp.dot(xn, wu_ref[...], preferred_element_type=F32)
    a = (g * _sigmoid(g)) * u
    acc_ref[...] += jnp.dot(a.astype(BF16), wd_ref[...], preferred_element_type=F32)

    @pl.when(f == nf - 1)
    def _():
        y_ref[...] = x_ref[...] + _rms(acc_ref[...], g5_ref[...])


def _ffn(x, g4, w_gu, w_down, li, g5, tm=1024, tf=1408):
    M, D = x.shape
    tm = min(tm, M)
    assert M % tm == 0 and D_FF % tf == 0
    nf = D_FF // tf
    return pl.pallas_call(
        functools.partial(_ffn_body, nf),
        grid=(M // tm, nf),
        in_specs=[pl.BlockSpec((tm, D), lambda i, f: (i, 0)),
                  pl.BlockSpec((1, D), lambda i, f: (0, 0)),
                  _wspec(li, (D, tf), lambda i, f: (0, f)),
                  _wspec(li, (D, tf), lambda i, f: (0, nf + f)),
                  _wspec(li, (tf, D), lambda i, f: (f, 0)),
                  pl.BlockSpec((1, D), lambda i, f: (0, 0))],
        out_specs=pl.BlockSpec((tm, D), lambda i, f: (i, 0)),
        out_shape=jax.ShapeDtypeStruct((M, D), F32),
        scratch_shapes=[pltpu.VMEM((tm, D), BF16), pltpu.VMEM((tm, D), F32)],
        compiler_params=_cparams(("parallel", "arbitrary")),
        name="ffn",
    )(x, g4.reshape(1, D), w_gu, w_gu, w_down, g5.reshape(1, D))


GDN_CHUNK = 64
GDN_BASE = 8


def _inverse_masks(L):
    row = lax.broadcasted_iota(jnp.int32, (L, L), 0)
    col = lax.broadcasted_iota(jnp.int32, (L, L), 1)
    base = row // GDN_BASE == col // GDN_BASE
    offs = []
    s = GDN_BASE
    while s < L:
        same_pair = row // (2 * s) == col // (2 * s)
        offs.append(jnp.logical_and(same_pair, (row // s) - (col // s) == 1))
        s *= 2
    return base, row == col, offs


def _unit_lower_inverse(As, masks, mm):
    base, diag, offs = masks
    eye = jnp.where(diag, 1.0, 0.0)
    Ps = [jnp.where(base, -A, 0.0) for A in As]
    Ts = [eye + P for P in Ps]
    k = 1
    while k < GDN_BASE // 2:
        Ps = [mm(P, P) for P in Ps]
        Ts = [T + mm(T, P) for T, P in zip(Ts, Ps)]
        k *= 2
    for off in offs:
        Ms = [_dot(T, jnp.where(off, A, 0.0)) for T, A in zip(Ts, As)]
        Ts = [T - _dot(M, T) for T, M in zip(Ts, Ms)]
    return Ts


def _gdn_prep_body(L, nch, qkv_ref, prev_ref, small_ref, buf_ref, wc_ref, par_ref,
                   wk_ref, uv_ref, qt_ref, kt_ref, at_ref, egl_ref, xbuf):
    c = pl.program_id(1)
    LA = L * nch

    @pl.when(c == 0)
    def _():
        xbuf[0:8, :] = jnp.zeros((8, GDN_QKV), F32)
        xbuf[5:8, :] = buf_ref[...]

    @pl.when(c > 0)
    def _():
        xbuf[0:8, :] = prev_ref[8:16, :].astype(F32)

    xbuf[8:8 + LA, :] = qkv_ref[...].astype(F32)
    xb = xbuf[...]
    z = xb * wc_ref[0:1, :]
    for j in range(1, CONV_W):
        z = pltpu.roll(z, 1, axis=0) + xb * wc_ref[j:j + 1, :]
    y = z[8:, :]
    y = y * _sigmoid(y)

    small = small_ref[...]
    beta_all = _sigmoid(small)
    g_all = -jnp.exp(par_ref[0:1, :]) * jax.nn.softplus(small + par_ref[1:2, :])
    incl, strict, _ = _tri_masks(L)
    tril_b = jnp.where(incl, 1.0, 0.0).astype(BF16)
    masks = _inverse_masks(L)
    H = GDN_HEADS

    units = []
    for ci in range(nch):
        rows = slice(ci * L, (ci + 1) * L)
        G_all = _cumsum_rows(tril_b, g_all[rows])
        GT = G_all.T
        egl_ref[ci] = jnp.broadcast_to(jnp.exp(G_all[L - 1:L, :]), (8, LANES))
        for h in range(H):
            qh = y[rows, h * GDN_DK:(h + 1) * GDN_DK]
            kh = y[rows, (H + h) * GDN_DK:(H + h + 1) * GDN_DK]
            vh = y[rows, 2 * H * GDN_DK + h * GDN_DV:2 * H * GDN_DK + (h + 1) * GDN_DV]
            qh = qh * lax.rsqrt(jnp.sum(qh * qh, axis=-1, keepdims=True) + EPS) * (GDN_DK ** -0.5)
            kh = kh * lax.rsqrt(jnp.sum(kh * kh, axis=-1, keepdims=True) + EPS)
            bc = beta_all[rows, h:h + 1]
            Gc = G_all[:, H + h:H + h + 1]
            Gr = GT[H + h:H + h + 1, :]
            eG = jnp.exp(Gc)
            dec = jnp.exp(jnp.where(incl, Gc - Gr, NEG_BIG))
            qkk = _dot_nt(jnp.concatenate([qh, kh], axis=0), kh)
            A = jnp.where(strict, bc * qkk[L:] * dec, 0.0)
            rhs = jnp.concatenate([bc * vh, (bc * eG) * kh], axis=-1)
            cols = slice(h * GDN_DV, (h + 1) * GDN_DV)
            qt_ref[rows, cols] = (eG * qh).astype(BF16)
            kt_ref[rows, cols] = (kh * jnp.exp(Gc[L - 1:L, :] - Gc)).astype(BF16)
            at_ref[rows, cols] = jnp.concatenate(
                [qkk[:L] * dec, jnp.zeros((L, GDN_DV - L), F32)], axis=-1).astype(BF16)
            units.append((rows, cols, A, rhs))

    Ts = _unit_lower_inverse([u[2] for u in units], masks, _dot)
    for (rows, cols, _, rhs), T in zip(units, Ts):
        sol = _dot(T, rhs)
        uv_ref[rows, cols] = sol[:, :GDN_DV]
        wk_ref[rows, cols] = sol[:, GDN_DV:].astype(BF16)


def _gdn_scan_body(L, cb, NS, wk_ref, uv_ref, qt_ref, kt_ref, at_ref, egl_ref, gate_ref, S0_ref, on_ref,
                   o_ref, Sout_ref, S_scr):
    c = pl.program_id(1)

    @pl.when(c == 0)
    def _():
        S_scr[...] = S0_ref[...]

    onorm = on_ref[...]
    H = GDN_HEADS
    for ci in range(cb):
        rows = slice(ci * L, (ci + 1) * L)
        egl = egl_ref[ci]
        hs = [slice(h * GDN_DV, (h + 1) * GDN_DV) for h in range(H)]
        Ss = [S_scr[h] for h in range(H)]
        Xs = [jnp.dot(jnp.concatenate([wk_ref[rows, hs[h]], qt_ref[rows, hs[h]]], axis=0),
                      Ss[h].astype(BF16), preferred_element_type=F32) for h in range(H)]
        Us = [(uv_ref[rows, hs[h]] - Xs[h][:L]).astype(BF16) for h in range(H)]
        os = [Xs[h][L:] + jnp.dot(at_ref[rows, h * GDN_DV:h * GDN_DV + L], Us[h],
                                  preferred_element_type=F32) for h in range(H)]
        for h in range(H):
            S_scr[h] = egl[0:1, H + h:H + h + 1] * Ss[h] + lax.dot_general(
                kt_ref[rows, hs[h]], Us[h], (((0,), (0,)), ((), ())), preferred_element_type=F32)
        for h in range(H):
            gh = gate_ref[rows, hs[h]].astype(F32)
            o_ref[rows, hs[h]] = (_rms(os[h], onorm) * (gh * _sigmoid(gh))).astype(BF16)

    @pl.when(c == NS - 1)
    def _():
        Sout_ref[...] = S_scr[...]


def _gdn_mixer(main, small, conv_buf, S0, wc_t, par, onorm, nch=4, cb=8):
    B, T, _ = main.shape
    L = min(GDN_CHUNK, T)
    assert T % L == 0
    NC = T // L
    nch = min(nch, NC)
    cb = min(cb, NC)
    assert NC % nch == 0 and NC % cb == 0
    LA = L * nch
    tok = lambda n, j: pl.BlockSpec((None, n, 1024), lambda b, c: (b, c, j))
    sds = lambda dt: jax.ShapeDtypeStruct((B, T, 1024), dt)
    wk, uv, qt, kt, at, egl = pl.pallas_call(
        functools.partial(_gdn_prep_body, L, nch),
        grid=(B, NC // nch),
        in_specs=[pl.BlockSpec((None, LA, GDN_QKV), lambda b, c: (b, c, 0)),
                  pl.BlockSpec((None, 16, GDN_QKV), lambda b, c: (b, jnp.maximum(c * (LA // 16) - 1, 0), 0)),
                  pl.BlockSpec((None, LA, LANES), lambda b, c: (b, c, 0)),
                  pl.BlockSpec((None, CONV_W - 1, GDN_QKV), lambda b, c: (b, 0, 0)),
                  pl.BlockSpec((CONV_W, GDN_QKV), lambda b, c: (0, 0)),
                  pl.BlockSpec((2, LANES), lambda b, c: (0, 0))],
        out_specs=[tok(LA, 0), tok(LA, 0), tok(LA, 0), tok(LA, 0), tok(LA, 0),
                   pl.BlockSpec((None, nch, 8, LANES), lambda b, c: (b, c, 0, 0))],
        out_shape=[sds(BF16), sds(F32), sds(BF16), sds(BF16), sds(BF16),
                   jax.ShapeDtypeStruct((B, NC, 8, LANES), F32)],
        scratch_shapes=[pltpu.VMEM((LA + 8, GDN_QKV), F32)],
        compiler_params=_cparams(("parallel", "parallel")),
        name="gdn_prep",
    )(main, main, small, conv_buf, wc_t, par)
    NS = NC // cb
    state = pl.BlockSpec((None, GDN_HEADS, GDN_DK, GDN_DV), lambda b, c: (b, 0, 0, 0))
    o, S = pl.pallas_call(
        functools.partial(_gdn_scan_body, L, cb, NS),
        grid=(B, NS),
        in_specs=[tok(cb * L, 0), tok(cb * L, 0), tok(cb * L, 0), tok(cb * L, 0), tok(cb * L, 0),
                  pl.BlockSpec((None, cb, 8, LANES), lambda b, c: (b, c, 0, 0)),
                  tok(cb * L, 3), state,
                  pl.BlockSpec((1, GDN_DV), lambda b, c: (0, 0))],
        out_specs=[tok(cb * L, 0), state],
        out_shape=[sds(BF16), jax.ShapeDtypeStruct((B, GDN_HEADS, GDN_DK, GDN_DV), F32)],
        scratch_shapes=[pltpu.VMEM((GDN_HEADS, GDN_DK, GDN_DV), F32)],
        compiler_params=_cparams(("parallel", "arbitrary")),
        name="gdn_scan",
    )(wk, uv, qt, kt, at, egl, main, S0, onorm.reshape(1, GDN_DV))
    new_buf = main[:, T - (CONV_W - 1):, :GDN_QKV].astype(F32)
    return o, new_buf, S


ML_CHUNK = 64


def _mlstm_gates(small_ref, bias_ref):
    sc = GATE_CAP * jnp.tanh((small_ref[...] + bias_ref[...]) / GATE_CAP)
    return sc, jax.nn.log_sigmoid(sc)


def _mlstm_prep_body(L, nch, qk_ref, v_ref, small_ref, bias_ref, hl_ref, st_ref, kv_ref, ks_ref):
    ig_all, lf_all = _mlstm_gates(small_ref, bias_ref)
    incl, _, _ = _tri_masks(L)
    tril_b = jnp.where(incl, 1.0, 0.0).astype(BF16)
    lane = lax.broadcasted_iota(jnp.int32, (1, LANES), 1)
    H = ML_HEADS
    units = [(ci, h) for ci in range(nch) for h in range(H)]
    rows_of = lambda ci: slice(ci * L, (ci + 1) * L)
    q_of = lambda ci, h: qk_ref[rows_of(ci), h * ML_DQK:(h + 1) * ML_DQK]
    k_of = lambda ci, h: qk_ref[rows_of(ci), (H + h) * ML_DQK:(H + h + 1) * ML_DQK].astype(F32) * (ML_DQK ** -0.5)
    v_of = lambda ci, h: v_ref[rows_of(ci), h * ML_DV:(h + 1) * ML_DV]

    gates = []
    for ci in range(nch):
        b_all = _cumsum_rows(tril_b, lf_all[rows_of(ci)])
        igc = ig_all[rows_of(ci)]
        gates.append((b_all, igc, b_all.T, igc.T))
    qk = [_dot_nt(q_of(ci, h), k_of(ci, h)) for ci, h in units]
    Wl, dmaxs = [], []
    for (ci, h), s in zip(units, qk):
        b_all, igc, bT, igT = gates[ci]
        D = jnp.where(incl, b_all[:, H + h:H + h + 1] + (igT[h:h + 1, :] - bT[H + h:H + h + 1, :]), NEG_BIG)
        dmax = jnp.max(D, axis=-1, keepdims=True)
        dmaxs.append(dmax)
        Wl.append(jnp.exp(D - dmax) * s)
    for (ci, h), w in zip(units, Wl):
        hl_ref[rows_of(ci), h * ML_DV:(h + 1) * ML_DV] = _dot(w, v_of(ci, h))
    wks = []
    for (ci, h), dmax in zip(units, dmaxs):
        b_all, igc, _, _ = gates[ci]
        b_c = b_all[:, H + h:H + h + 1]
        wks.append(jnp.exp(b_c[L - 1:L, :] - b_c + igc[:, h:h + 1] - dmax[L - 1:L, :]) * k_of(ci, h))
    for (ci, h), wk in zip(units, wks):
        kv_ref[ci, h] = _dot_tn(wk, v_of(ci, h))
        ks_ref[ci, h:h + 1, :] = jnp.sum(wk, axis=0, keepdims=True)
    for ci in range(nch):
        stats = jnp.zeros((L, LANES), F32)
        for h in range(H):
            u = ci * H + h
            stats = jnp.where(lane == h, gates[ci][0][:, H + h:H + h + 1],
                              jnp.where(lane == H + h, dmaxs[u],
                                        jnp.where(lane == 2 * H + h, jnp.sum(Wl[u], axis=-1, keepdims=True), stats)))
        st_ref[rows_of(ci), :] = stats


def _mlstm_scan_body(L, cb, NS, q_ref, hl_ref, st_ref, kv_ref, ks_ref, op_ref, C0_ref, n0_ref, m0_ref, on_ref,
                     o_ref, Cout_ref, nout_ref, mout_ref, C_scr, n_scr, m_scr):
    c = pl.program_id(1)

    @pl.when(c == 0)
    def _():
        C_scr[...] = C0_ref[...]
        n_scr[...] = n0_ref[...]
        m_scr[...] = m0_ref[...]

    onorm = on_ref[...]
    H = ML_HEADS
    units = [(ci, h) for ci in range(cb) for h in range(H)]
    rows_of = lambda ci: slice(ci * L, (ci + 1) * L)
    sts = [st_ref[rows_of(ci), :] for ci in range(cb)]
    pre = {}
    for h in range(H):
        m_prev = m_scr[h:h + 1, 0:1]
        n_prev = n_scr[h:h + 1, :]
        C = C_scr[h]
        for ci in range(cb):
            b_c = sts[ci][:, h:h + 1]
            dmax = sts[ci][:, H + h:H + h + 1]
            mt = jnp.maximum(b_c + m_prev, dmax)
            pre[(ci, h)] = (C, n_prev, m_prev, mt)
            mL = mt[L - 1:L, :]
            fL = jnp.exp(b_c[L - 1:L, :] + m_prev - mL)
            gL = jnp.exp(dmax[L - 1:L, :] - mL)
            C = fL * C + gL * kv_ref[ci, h]
            n_prev = fL * n_prev + gL * ks_ref[ci, h:h + 1, :]
            m_prev = mL
        C_scr[h] = C
        n_scr[h:h + 1, :] = n_prev
        m_scr[h:h + 1, :] = jnp.broadcast_to(m_prev, (1, LANES))

    qs = {u: q_ref[rows_of(u[0]), u[1] * ML_DQK:(u[1] + 1) * ML_DQK].astype(F32) for u in units}
    qC = {u: _dot(qs[u], pre[u][0]) for u in units}
    hs = {}
    for u in units:
        ci, h = u
        _, n_prev, m_prev, mt = pre[u]
        b_c = sts[ci][:, h:h + 1]
        dmax = sts[ci][:, H + h:H + h + 1]
        sw = sts[ci][:, 2 * H + h:2 * H + h + 1]
        e_i = jnp.exp(b_c + m_prev - mt)
        e_l = jnp.exp(dmax - mt)
        num = e_i * qC[u] + e_l * hl_ref[rows_of(ci), h * ML_DV:(h + 1) * ML_DV]
        den = e_i * jnp.sum(qs[u] * n_prev, axis=-1, keepdims=True) + e_l * sw
        hs[u] = num / jnp.maximum(jnp.abs(den), jnp.exp(-mt))
    for u in units:
        ci, h = u
        og = op_ref[rows_of(ci), h * ML_DV:(h + 1) * ML_DV].astype(F32)
        o_ref[rows_of(ci), h * ML_DV:(h + 1) * ML_DV] = (_sigmoid(og) * _rms(hs[u], onorm)).astype(BF16)

    @pl.when(c == NS - 1)
    def _():
        Cout_ref[...] = C_scr[...]
        nout_ref[...] = n_scr[...]
        mout_ref[...] = m_scr[...]


def _mlstm_mixer(main, small, C0, n0, m0, bias, onorm, nch=4, cb=4):
    B, T, _ = main.shape
    L = min(ML_CHUNK, T)
    assert T % L == 0
    NC = T // L
    nch = min(nch, NC)
    cb = min(cb, NC)
    assert NC % nch == 0 and NC % cb == 0
    LA = L * nch
    H = ML_HEADS
    hl, st, kv, ks = pl.pallas_call(
        functools.partial(_mlstm_prep_body, L, nch),
        grid=(B, NC // nch),
        in_specs=[pl.BlockSpec((None, LA, 1024), lambda b, c: (b, c, 0)),
                  pl.BlockSpec((None, LA, 1024), lambda b, c: (b, c, 1)),
                  pl.BlockSpec((None, LA, LANES), lambda b, c: (b, c, 0)),
                  pl.BlockSpec((1, LANES), lambda b, c: (0, 0))],
        out_specs=[pl.BlockSpec((None, LA, 1024), lambda b, c: (b, c, 0)),
                   pl.BlockSpec((None, LA, LANES), lambda b, c: (b, c, 0)),
                   pl.BlockSpec((None, nch, H, ML_DQK, ML_DV), lambda b, c: (b, c, 0, 0, 0)),
                   pl.BlockSpec((None, nch, H, ML_DQK), lambda b, c: (b, c, 0, 0))],
        out_shape=[jax.ShapeDtypeStruct((B, T, 1024), F32),
                   jax.ShapeDtypeStruct((B, T, LANES), F32),
                   jax.ShapeDtypeStruct((B, NC, H, ML_DQK, ML_DV), F32),
                   jax.ShapeDtypeStruct((B, NC, H, ML_DQK), F32)],
        compiler_params=_cparams(("parallel", "parallel")),
        name="mlstm_prep",
    )(main, main, small, bias)
    NS = NC // cb
    m0b = jnp.broadcast_to(m0[:, :, None], (B, H, LANES))
    st_C = pl.BlockSpec((None, H, ML_DQK, ML_DV), lambda b, c: (b, 0, 0, 0))
    st_n = pl.BlockSpec((None, H, ML_DQK), lambda b, c: (b, 0, 0))
    st_m = pl.BlockSpec((None, H, LANES), lambda b, c: (b, 0, 0))
    return pl.pallas_call(
        functools.partial(_mlstm_scan_body, L, cb, NS),
        grid=(B, NS),
        in_specs=[pl.BlockSpec((None, cb * L, H * ML_DQK), lambda b, c: (b, c, 0)),
                  pl.BlockSpec((None, cb * L, 1024), lambda b, c: (b, c, 0)),
                  pl.BlockSpec((None, cb * L, LANES), lambda b, c: (b, c, 0)),
                  pl.BlockSpec((None, cb, H, ML_DQK, ML_DV), lambda b, c: (b, c, 0, 0, 0)),
                  pl.BlockSpec((None, cb, H, ML_DQK), lambda b, c: (b, c, 0, 0)),
                  pl.BlockSpec((None, cb * L, 1024), lambda b, c: (b, c, 2)),
                  st_C, st_n, st_m,
                  pl.BlockSpec((1, ML_DV), lambda b, c: (0, 0))],
        out_specs=[pl.BlockSpec((None, cb * L, 1024), lambda b, c: (b, c, 0)), st_C, st_n, st_m],
        out_shape=[jax.ShapeDtypeStruct((B, T, 1024), BF16),
                   jax.ShapeDtypeStruct((B, H, ML_DQK, ML_DV), F32),
                   jax.ShapeDtypeStruct((B, H, ML_DQK), F32),
                   jax.ShapeDtypeStruct((B, H, LANES), F32)],
        scratch_shapes=[pltpu.VMEM((H, ML_DQK, ML_DV), F32),
                        pltpu.VMEM((H, ML_DQK), F32),
                        pltpu.VMEM((H, LANES), F32)],
        compiler_params=_cparams(("parallel", "arbitrary")),
        name="mlstm_scan",
    )(main, hl, st, kv, ks, main, C0, n0, m0b, onorm.reshape(1, ML_DV))


FOX_BIAS_LANE = FOX_DH


def _fox_prep_body(q_ref, k_ref, v_ref, small_ref, bf_ref, qn_ref, kn_ref,
                   qh_ref, k32_ref, v32_ref, lf_ref):
    r = lax.broadcasted_iota(jnp.int32, (LANES, LANES), 0) // FOX_DH
    cc = lax.broadcasted_iota(jnp.int32, (LANES, LANES), 1) // FOX_DH
    avg = jnp.where(r == cc, 1.0 / FOX_DH, 0.0).astype(BF16)

    def head_rms(x, gain):
        parts = []
        for t in range(x.shape[1] // LANES):
            xt = x[:, t * LANES:(t + 1) * LANES]
            hi, lo = _split2(xt * xt)
            ms = (jnp.dot(hi, avg, preferred_element_type=F32)
                  + jnp.dot(lo, avg, preferred_element_type=F32))
            parts.append(xt * lax.rsqrt(ms + EPS))
        return jnp.concatenate(parts, axis=-1) * gain

    qn = head_rms(q_ref[...].astype(F32), qn_ref[...]) * (LOG2E * FOX_DH ** -0.5)
    lane = lax.broadcasted_iota(jnp.int32, (1, LANES), 1)
    ones3 = jnp.where(lane < FOX_BIAS_LANE + 3, 1.0, 0.0)
    for h in range(FOX_HEADS):
        xt = qn[:, (h // 2) * LANES:(h // 2 + 1) * LANES]
        if h % 2:
            xt = pltpu.roll(xt, FOX_DH, axis=1)
        qh_ref[h] = jnp.where(lane < FOX_DH, xt, ones3).astype(BF16)
    k32_ref[...] = head_rms(k_ref[...].astype(F32), kn_ref[...])
    v32_ref[...] = v_ref[...].astype(F32)
    lf_ref[...] = jax.nn.log_sigmoid(small_ref[...] + bf_ref[...])


def _fox_prep(main, small, bf_row, qn_row, kn_row, tm=512):
    B, T, _ = main.shape
    tm = min(tm, T)
    assert T % tm == 0
    blk = lambda j: pl.BlockSpec((None, tm, 1024), lambda b, i: (b, i, j))
    row = lambda n: pl.BlockSpec((1, n), lambda b, i: (0, 0))
    o1024 = pl.BlockSpec((None, tm, 1024), lambda b, i: (b, i, 0))
    return pl.pallas_call(
        _fox_prep_body,
        grid=(B, T // tm),
        in_specs=[blk(0), blk(1), blk(2),
                  pl.BlockSpec((None, tm, LANES), lambda b, i: (b, i, 0)),
                  row(LANES), row(1024), row(1024)],
        out_specs=[pl.BlockSpec((None, FOX_HEADS, tm, LANES), lambda b, i: (b, 0, i, 0)),
                   o1024, o1024,
                   pl.BlockSpec((None, tm, LANES), lambda b, i: (b, i, 0))],
        out_shape=[jax.ShapeDtypeStruct((B, FOX_HEADS, T, LANES), BF16),
                   jax.ShapeDtypeStruct((B, T, 1024), F32),
                   jax.ShapeDtypeStruct((B, T, 1024), F32),
                   jax.ShapeDtypeStruct((B, T, LANES), F32)],
        compiler_params=_cparams(("parallel", "parallel")),
        name="fox_prep",
    )(main, main, main, small, bf_row, qn_row, kn_row)


def _fox_kv_body(tm, n_past, *refs):
    if n_past:
        kp_ref, vp_ref, lp_ref, k_ref, v_ref, lf_ref, wk_ref, wv_ref, kh_ref, vh_ref, carry = refs
    else:
        k_ref, v_ref, lf_ref, wk_ref, wv_ref, kh_ref, vh_ref, carry = refs
    i = pl.program_id(1)

    @pl.when(i == 0)
    def _():
        carry[...] = jnp.zeros_like(carry)

    k = k_ref[...]
    v = v_ref[...]
    lf = lf_ref[...]
    if n_past:
        past = i < n_past
        k = jnp.where(past, kp_ref[...], k)
        v = jnp.where(past, vp_ref[...], v)
        lf = jnp.where(past, lp_ref[...], lf)
    incl, _, _ = _tri_masks(tm)
    tril_b = jnp.where(incl, 1.0, 0.0).astype(BF16)
    F = _cumsum_rows(tril_b, lf) + carry[0:1, :]
    carry[0:1, :] = F[tm - 1:tm, :]
    nb = -LOG2E * F
    b1 = nb.astype(BF16).astype(F32)
    r1 = nb - b1
    b2 = r1.astype(BF16).astype(F32)
    b3 = r1 - b2
    lane = lax.broadcasted_iota(jnp.int32, (1, LANES), 1)
    ones1 = jnp.where(lane == FOX_BIAS_LANE, 1.0, 0.0)
    parts = jnp.where(lane < FOX_HEADS, b1,
                      jnp.where(lane < 2 * FOX_HEADS, pltpu.roll(b2, FOX_HEADS, axis=1),
                                jnp.where(lane < 3 * FOX_HEADS, pltpu.roll(b3, 2 * FOX_HEADS, axis=1), 0.0)))
    parts_b = parts.astype(BF16)
    kb = k.astype(BF16)
    vb = v.astype(BF16)
    for h in range(FOX_HEADS):
        tile = slice((h // 2) * LANES, (h // 2 + 1) * LANES)
        kh_ref[h] = jnp.dot(jnp.concatenate([kb[:, tile], parts_b], axis=-1), wk_ref[h],
                            preferred_element_type=F32).astype(BF16)
        vh_ref[h] = (jnp.dot(vb[:, tile], wv_ref[h % 2], preferred_element_type=F32) + ones1).astype(BF16)


def _fox_kv(k_new, v_new, lf_new, past=None, tm=512):
    B, T, _ = k_new.shape
    if past is None:
        tm = min(tm, T)
        assert T % tm == 0
        n_past, n_new = 0, T // tm
        args, specs = [], []
    else:
        k_past, v_past, lf_past = past
        P = k_past.shape[1]
        tm = min(tm, P)
        assert P % tm == 0 and T <= tm
        n_past, n_new = P // tm, 1
        pad = lambda a: jnp.pad(a, ((0, 0), (0, tm - T), (0, 0)))
        k_new, v_new, lf_new = pad(k_new), pad(v_new), pad(lf_new)
        clamp = lambda b, i: (b, jnp.minimum(i, n_past - 1), 0)
        args = [k_past, v_past, lf_past]
        specs = [pl.BlockSpec((None, tm, 1024), clamp), pl.BlockSpec((None, tm, 1024), clamp),
                 pl.BlockSpec((None, tm, LANES), clamp)]
    new_idx = lambda b, i: (b, jnp.maximum(i - n_past, 0), 0)
    N = (n_past + n_new) * tm
    hm = pl.BlockSpec((None, FOX_HEADS, tm, LANES), lambda b, i: (b, 0, i, 0))
    wk = np.zeros((FOX_HEADS, 2 * LANES, LANES), np.float32)
    wv = np.zeros((2, LANES, LANES), np.float32)
    for h in range(FOX_HEADS):
        for d in range(FOX_DH):
            wk[h, FOX_DH * (h % 2) + d, d] = 1.0
            wv[h % 2, FOX_DH * (h % 2) + d, d] = 1.0
        for g in range(3):
            wk[h, LANES + g * FOX_HEADS + h, FOX_BIAS_LANE + g] = 1.0
    wk, wv = jnp.asarray(wk, BF16), jnp.asarray(wv, BF16)
    return pl.pallas_call(
        functools.partial(_fox_kv_body, tm, n_past),
        grid=(B, n_past + n_new),
        in_specs=specs + [pl.BlockSpec((None, tm, 1024), new_idx),
                          pl.BlockSpec((None, tm, 1024), new_idx),
                          pl.BlockSpec((None, tm, LANES), new_idx),
                          pl.BlockSpec((FOX_HEADS, 2 * LANES, LANES), lambda b, i: (0, 0, 0)),
                          pl.BlockSpec((2, LANES, LANES), lambda b, i: (0, 0, 0))],
        out_specs=[hm, hm],
        out_shape=[jax.ShapeDtypeStruct((B, FOX_HEADS, N, LANES), BF16),
                   jax.ShapeDtypeStruct((B, FOX_HEADS, N, LANES), BF16)],
        scratch_shapes=[pltpu.VMEM((8, LANES), F32)],
        compiler_params=_cparams(("parallel", "arbitrary")),
        name="fox_kv",
    )(*args, k_new, v_new, lf_new, wk, wv)


def _flash_body(P, tq, tk, rq, hp, qi_tab, kj_tab, q_ref, k_ref, v_ref, gate_ref, o_ref, m_scr, acc_scr):
    s_idx = pl.program_id(2)
    qi = qi_tab[s_idx]
    kj = kj_tab[s_idx]
    last_kj = (P + (qi + 1) * tq - 1) // tk
    aligned = tq == tk and P % tk == 0

    @pl.when(kj == 0)
    def _():
        m_scr[...] = jnp.full(m_scr.shape, NEG_BIG, F32)
        acc_scr[...] = jnp.zeros_like(acc_scr)

    def step(masked):
        units = [(hh, r) for r in range(tq // rq) for hh in range(2 * hp)]

        def logits(u):
            hh, r = u
            rows = slice(r * rq, (r + 1) * rq)
            nk = min(tk, (r + 1) * rq) if (masked and aligned) else tk
            s = lax.dot_general(q_ref[hh, rows, :], k_ref[hh, 0:nk, :], (((1,), (1,)), ((), ())),
                                preferred_element_type=F32)
            if masked:
                qpos = P + qi * tq + r * rq + lax.broadcasted_iota(jnp.int32, (rq, nk), 0)
                kpos = kj * tk + lax.broadcasted_iota(jnp.int32, (rq, nk), 1)
                s = jnp.where(kpos <= qpos, s, NEG_BIG)
            return s, nk

        def probs(u, s):
            hh, r = u
            rows = slice(r * rq, (r + 1) * rq)
            m_prev = m_scr[hh, rows, :][:, 0:1]
            m_new = jnp.maximum(m_prev, jnp.max(s, axis=-1, keepdims=True))
            m_scr[hh, rows, :] = jnp.broadcast_to(m_new, (rq, LANES))
            return jnp.exp2(m_prev - m_new), jnp.exp2(s - m_new).astype(BF16)

        def accumulate(u, alpha, p, nk):
            hh, r = u
            rows = slice(r * rq, (r + 1) * rq)
            acc_scr[hh, rows, :] = alpha * acc_scr[hh, rows, :] + jnp.dot(
                p, v_ref[hh, 0:nk, :], preferred_element_type=F32)

        n = len(units)
        pend_s = {0: logits(units[0])}
        pend_p = {}
        for i in range(n):
            if i + 1 < n:
                pend_s[i + 1] = logits(units[i + 1])
            s, nk = pend_s.pop(i)
            pend_p[i] = probs(units[i], s) + (nk,)
            if i >= 1:
                accumulate(units[i - 1], *pend_p.pop(i - 1))
        accumulate(units[n - 1], *pend_p.pop(n - 1))

    needs_mask = (kj + 1) * tk - 1 > P + qi * tq

    @pl.when(needs_mask)
    def _():
        step(True)

    @pl.when(jnp.logical_not(needs_mask))
    def _():
        step(False)

    @pl.when(kj == last_kj)
    def _():
        lane = lax.broadcasted_iota(jnp.int32, (1, LANES), 1)
        for pp in range(hp):
            a0 = acc_scr[2 * pp]
            a1 = acc_scr[2 * pp + 1]
            o0 = a0 * (1.0 / a0[:, FOX_BIAS_LANE:FOX_BIAS_LANE + 1])
            o1 = a1 * (1.0 / a1[:, FOX_BIAS_LANE:FOX_BIAS_LANE + 1])
            o = jnp.where(lane < FOX_DH, o0, pltpu.roll(o1, FOX_DH, axis=1))
            g = gate_ref[:, pp * LANES:(pp + 1) * LANES].astype(F32)
            o_ref[:, pp * LANES:(pp + 1) * LANES] = (o * _sigmoid(g)).astype(BF16)


def _fox_flash(qh, kh, vh, gate_src, P, tq, tk, rq, hp=1):
    B, _, T, _ = qh.shape
    N = kh.shape[2]
    assert T % tq == 0 and N % tk == 0 and tq % rq == 0 and (FOX_HEADS // 2) % hp == 0
    nq = T // tq
    pairs = [(qi, kj) for qi in range(nq) for kj in range((P + (qi + 1) * tq - 1) // tk + 1)]
    qi_tab = jnp.asarray(np.array([p[0] for p in pairs], np.int32))
    kj_tab = jnp.asarray(np.array([p[1] for p in pairs], np.int32))
    HP = FOX_HEADS // (2 * hp)
    gate_blk0 = 3 * (1024 // (LANES * hp))
    grid_spec = pltpu.PrefetchScalarGridSpec(
        num_scalar_prefetch=2,
        grid=(B, HP, len(pairs)),
        in_specs=[pl.BlockSpec((None, 2 * hp, tq, LANES), lambda b, p, s, qt, kt: (b, p, qt[s], 0)),
                  pl.BlockSpec((None, 2 * hp, tk, LANES), lambda b, p, s, qt, kt: (b, p, kt[s], 0)),
                  pl.BlockSpec((None, 2 * hp, tk, LANES), lambda b, p, s, qt, kt: (b, p, kt[s], 0)),
                  pl.BlockSpec((None, tq, LANES * hp), lambda b, p, s, qt, kt: (b, qt[s], gate_blk0 + p))],
        out_specs=pl.BlockSpec((None, tq, LANES * hp), lambda b, p, s, qt, kt: (b, qt[s], p)),
        scratch_shapes=[pltpu.VMEM((2 * hp, tq, LANES), F32),
                        pltpu.VMEM((2 * hp, tq, LANES), F32)],
    )
    return pl.pallas_call(
        functools.partial(_flash_body, P, tq, tk, rq, hp),
        grid_spec=grid_spec,
        out_shape=jax.ShapeDtypeStruct((B, T, 1024), BF16),
        compiler_params=_cparams(("parallel", "parallel", "arbitrary")),
        name="fox_flash",
    )(qi_tab, kj_tab, qh, kh, vh, gate_src)


def _pad_lanes(a, offset=0):
    n = a.shape[-1]
    pads = [(0, 0)] * (a.ndim - 1) + [(offset, LANES - n - offset)]
    return jnp.pad(a, pads)


def _trunk(x, gdn_conv, gdn_S, ml_C, ml_n, ml_m, fox_k, fox_v, fox_lf, mem_k, mem_v, W):
    B, T, D = x.shape
    M = B * T
    x2 = x.reshape(M, D)
    conv_o, S_o, C_o, n_o, m_o, k_o, v_o, lf_o = [], [], [], [], [], [], [], []
    for layer in range(DEPTH):
        kind, j = layer % N_MIXERS, layer // N_MIXERS
        gains = W["norm_gains"][layer]
        if kind == 0:
            main, small = _prenorm_mm(x2, gains[0], W["gdn_w_in"], j, GDN_MAIN, W["gdn_w_small"], out_dtype=F32)
            o, buf, S = _gdn_mixer(main.reshape(B, T, GDN_MAIN), small.reshape(B, T, LANES),
                                   gdn_conv[j], gdn_S[j], W["gdn_wc_t"][j], W["gdn_par"][j],
                                   W["gdn_onorm"][j])
            conv_o.append(buf)
            S_o.append(S)
            w_out = W["gdn_w_out"]
        elif kind == 1:
            main, small = _prenorm_mm(x2, gains[0], W["ml_w_in"], j, ML_MAIN, W["ml_w_small"])
            o, C, n, m = _mlstm_mixer(main.reshape(B, T, ML_MAIN), small.reshape(B, T, LANES),
                                      ml_C[j], ml_n[j], ml_m[j], W["ml_bias"][j], W["ml_onorm"][j])
            C_o.append(C)
            n_o.append(n)
            m_o.append(m[:, :, 0])
            w_out = W["ml_w_out"]
        else:
            main, small = _prenorm_mm(x2, gains[0], W["fox_w_in"], j, FOX_MAIN, W["fox_w_small"])
            main3 = main.reshape(B, T, FOX_MAIN)
            qh, k32, v32, lfp = _fox_prep(main3, small.reshape(B, T, LANES), W["fox_bf"][j],
                                          W["fox_qn"][j], W["fox_kn"][j])
            P = fox_k[j].shape[1]
            if P == 0:
                kh, vh = _fox_kv(k32, v32, lfp)
                tq, tk, rq, hp = min(1024, T), min(1024, T), 256, 1
            else:
                kh, vh = _fox_kv(k32, v32, lfp, past=(fox_k[j].reshape(B, P, 1024), fox_v[j].reshape(B, P, 1024),
                                                      _pad_lanes(fox_lf[j])), tm=256)
                tq, tk, rq, hp = T, kh.shape[2], T, 4
            o = _fox_flash(qh, kh, vh, main3, P, tq, tk, rq, hp)
            k_o.append(k32.reshape(B, T, FOX_HEADS, FOX_DH))
            v_o.append(v32.reshape(B, T, FOX_HEADS, FOX_DH))
            lf_o.append(lfp[:, :, :FOX_HEADS])
            w_out = W["fox_w_out"]
        x2 = _memattn(o, w_out, j, gains[1], x2.reshape(B, T, D), gains[2], W["x_w_q"],
                      mem_k[layer], mem_v[layer], W["x_w_o"], layer, gains[3]).reshape(M, D)
        x2 = _ffn(x2, gains[4], W["ffn_w_gu"], W["ffn_w_down"], layer, gains[5])
    return (x2.reshape(B, T, D), jnp.stack(conv_o), jnp.stack(S_o), jnp.stack(C_o), jnp.stack(n_o),
            jnp.stack(m_o), jnp.stack(k_o), jnp.stack(v_o), jnp.stack(lf_o))


def kernel(x_prompt, x_sample, state_gdn_conv, state_gdn_S, state_mlstm_C, state_mlstm_n, state_mlstm_m, cache_fox_k, cache_fox_v, cache_fox_logf, cache_mem_k, cache_mem_v, mem_prompt, norm_gains, gdn_w_in, gdn_w_conv, gdn_a_log, gdn_dt_bias, gdn_onorm, gdn_w_out, ml_w_in, ml_b_i, ml_b_f, ml_onorm, ml_w_out, fox_w_in, fox_b_f, fox_qk_norm, fox_w_out, x_w_q, x_w_kv, x_w_o, ffn_w_gu, ffn_w_down):
    B = x_prompt.shape[0]
    DB = x_sample.shape[0]
    n_a, n_b, n_c = gdn_w_in.shape[0], ml_w_in.shape[0], fox_w_in.shape[0]
    W = {
        "norm_gains": norm_gains,
        "gdn_w_in": gdn_w_in.astype(BF16),
        "gdn_w_small": _pad_lanes(gdn_w_in[:, :, GDN_MAIN:]).astype(BF16),
        "gdn_wc_t": jnp.transpose(gdn_w_conv, (0, 2, 1)),
        "gdn_par": jnp.stack([_pad_lanes(gdn_a_log, GDN_HEADS), _pad_lanes(gdn_dt_bias, GDN_HEADS)], axis=1),
        "gdn_onorm": gdn_onorm,
        "gdn_w_out": gdn_w_out.astype(BF16),
        "ml_w_in": ml_w_in.astype(BF16),
        "ml_w_small": _pad_lanes(ml_w_in[:, :, ML_MAIN:]).astype(BF16),
        "ml_bias": _pad_lanes(jnp.concatenate([ml_b_i, ml_b_f], axis=-1))[:, None, :],
        "ml_onorm": ml_onorm,
        "ml_w_out": ml_w_out.astype(BF16),
        "fox_w_in": fox_w_in.astype(BF16),
        "fox_w_small": _pad_lanes(fox_w_in[:, :, FOX_MAIN:]).astype(BF16),
        "fox_bf": _pad_lanes(fox_b_f)[:, None, :],
        "fox_qn": jnp.tile(fox_qk_norm[:, 0, :], (1, FOX_HEADS))[:, None, :],
        "fox_kn": jnp.tile(fox_qk_norm[:, 1, :], (1, FOX_HEADS))[:, None, :],
        "fox_w_out": fox_w_out.astype(BF16),
        "x_w_q": x_w_q.astype(BF16),
        "x_w_o": x_w_o.astype(BF16),
        "ffn_w_gu": ffn_w_gu.astype(BF16),
        "ffn_w_down": ffn_w_down.astype(BF16),
    }

    mem2 = mem_prompt.reshape(B * N_MEM, D_MODEL)
    x_w_kv_b = x_w_kv.astype(BF16)
    kv = [_prenorm_mm(mem2, norm_gains[l, 6], x_w_kv_b, l, 2 * D_MODEL, out_dtype=F32).reshape(B, N_MEM, 2 * D_MODEL)
          for l in range(DEPTH)]
    p_mem_k = jnp.stack([a[:, :, :D_MODEL].reshape(B, N_MEM, X_HEADS, X_DH) for a in kv])
    p_mem_v = jnp.stack([a[:, :, D_MODEL:].reshape(B, N_MEM, X_HEADS, X_DH) for a in kv])

    cmk = cache_mem_k.reshape(DEPTH * DB, N_MEM, D_MODEL)
    cmv = cache_mem_v.reshape(DEPTH * DB, N_MEM, D_MODEL)
    zeros = lambda *s: jnp.zeros(s, F32)
    prompt = _trunk(
        x_prompt,
        zeros(n_a, B, CONV_W - 1, GDN_QKV), zeros(n_a, B, GDN_HEADS, GDN_DK, GDN_DV),
        zeros(n_b, B, ML_HEADS, ML_DQK, ML_DV), zeros(n_b, B, ML_HEADS, ML_DQK), zeros(n_b, B, ML_HEADS),
        zeros(n_c, B, 0, FOX_HEADS, FOX_DH), zeros(n_c, B, 0, FOX_HEADS, FOX_DH), zeros(n_c, B, 0, FOX_HEADS),
        [(a, 0, 0) for a in kv], [(a, 0, 1) for a in kv], W)
    sample = _trunk(
        x_sample, state_gdn_conv, state_gdn_S, state_mlstm_C, state_mlstm_n, state_mlstm_m,
        cache_fox_k, cache_fox_v, cache_fox_logf,
        [(cmk, l * DB, 0) for l in range(DEPTH)], [(cmv, l * DB, 0) for l in range(DEPTH)], W)
    (y_p, p_conv, p_S, p_C, p_n, p_m, p_k, p_v, p_lf) = prompt
    (y_s, s_conv, s_S, s_C, s_n, s_m, s_k, s_v, s_lf) = sample
    return (y_p, y_s, p_conv, p_S, p_C, p_n, p_m, p_k, p_v, p_lf, p_mem_k, p_mem_v,
            s_conv, s_S, s_C, s_n, s_m, s_k, s_v, s_lf)
```

```python
import functools

import numpy as np
import jax
import jax.numpy as jnp
from jax import lax
from jax.experimental import pallas as pl
from jax.experimental.pallas import tpu as pltpu

F32 = jnp.float32
BF16 = jnp.bfloat16

D_MODEL = 1024
DEPTH = 4
N_MIXERS = 3
EPS = 1e-6
CONV_W = 4

GDN_HEADS = 8
GDN_DK = 128
GDN_DV = 128
GDN_QKV = 3072
GDN_MAIN = 4096

ML_HEADS = 4
ML_DQK = 128
ML_DV = 256
ML_MAIN = 3072
GATE_CAP = 15.0

FOX_HEADS = 16
FOX_DH = 64
FOX_MAIN = 4096

N_MEM = 256
X_HEADS = 4
X_DH = 256
D_FF = 2816

LANES = 128
NEG_BIG = -1e30
LOG2E = 1.4426950408889634

VMEM_LIMIT = 48 * 1024 * 1024


def _wspec(li, blk, imap):
    return pl.BlockSpec((None,) + blk, lambda *g: (li,) + imap(*g))


def _cparams(sem):
    return pltpu.CompilerParams(dimension_semantics=sem, vmem_limit_bytes=VMEM_LIMIT)


def _rms(x, g):
    return x * lax.rsqrt(jnp.mean(x * x, axis=-1, keepdims=True) + EPS) * g


def _sigmoid(x):
    return 1.0 / (1.0 + jnp.exp(-x))


def _dot(a, b):
    return jnp.dot(a.astype(BF16), b.astype(BF16), preferred_element_type=F32)


def _dot_nt(a, b):
    return lax.dot_general(a.astype(BF16), b.astype(BF16), (((1,), (1,)), ((), ())),
                           preferred_element_type=F32)


def _dot_tn(a, b):
    return lax.dot_general(a.astype(BF16), b.astype(BF16), (((0,), (0,)), ((), ())),
                           preferred_element_type=F32)


def _split2(x):
    hi = x.astype(BF16)
    lo = (x - hi.astype(F32)).astype(BF16)
    return hi, lo


def _dot_x3(a, b):
    ah, al = _split2(a)
    bh, bl = _split2(b)
    d = lambda u, v: jnp.dot(u, v, preferred_element_type=F32)
    return d(ah, bh) + (d(ah, bl) + d(al, bh))


def _cumsum_rows(tril_b, x):
    h = x.astype(BF16)
    r = x - h.astype(F32)
    m = r.astype(BF16)
    l = (r - m.astype(F32)).astype(BF16)
    d = lambda v: jnp.dot(tril_b, v, preferred_element_type=F32)
    return d(h) + (d(m) + d(l))


def _tri_masks(L):
    row = lax.broadcasted_iota(jnp.int32, (L, L), 0)
    col = lax.broadcasted_iota(jnp.int32, (L, L), 1)
    return row >= col, row > col, row == col


def _prenorm_mm_body(has_small, *refs):
    if has_small:
        x_ref, g_ref, w_ref, ws_ref, o_ref, os_ref, xn_ref = refs
    else:
        x_ref, g_ref, w_ref, o_ref, xn_ref = refs

    @pl.when(pl.program_id(1) == 0)
    def _():
        xb = _rms(x_ref[...], g_ref[...]).astype(BF16)
        xn_ref[...] = xb
        if has_small:
            os_ref[...] = jnp.dot(xb, ws_ref[...], preferred_element_type=F32)

    o_ref[...] = jnp.dot(xn_ref[...], w_ref[...], preferred_element_type=F32).astype(o_ref.dtype)


def _prenorm_mm(x, gain, w_main, li, n_out, w_small=None, out_dtype=BF16, tm=1024, tn=1024):
    M, D = x.shape
    N = n_out
    tm = min(tm, M)
    tn = min(tn, N)
    assert M % tm == 0 and N % tn == 0
    has_small = w_small is not None
    in_specs = [pl.BlockSpec((tm, D), lambda i, j: (i, 0)),
                pl.BlockSpec((1, D), lambda i, j: (0, 0)),
                _wspec(li, (D, tn), lambda i, j: (0, j))]
    out_specs = [pl.BlockSpec((tm, tn), lambda i, j: (i, j))]
    out_shape = [jax.ShapeDtypeStruct((M, N), out_dtype)]
    args = [x, gain.reshape(1, D), w_main]
    if has_small:
        in_specs.append(_wspec(li, (D, LANES), lambda i, j: (0, 0)))
        out_specs.append(pl.BlockSpec((tm, LANES), lambda i, j: (i, 0)))
        out_shape.append(jax.ShapeDtypeStruct((M, LANES), F32))
        args.append(w_small)
    res = pl.pallas_call(
        functools.partial(_prenorm_mm_body, has_small),
        grid=(M // tm, N // tn),
        in_specs=in_specs, out_specs=out_specs, out_shape=out_shape,
        scratch_shapes=[pltpu.VMEM((tm, D), BF16)],
        compiler_params=_cparams(("parallel", "arbitrary")),
        name="prenorm_mm",
    )(*args)
    return res if has_small else res[0]


def _memattn_body(nb, o_ref, wout_ref, g1_ref, x_ref, g2_ref, wq_ref, mk_ref, mv_ref, wo_ref, g3_ref, y_ref):
    tq, D = x_ref.shape[1], x_ref.shape[2]
    R = nb * tq
    o_in = o_ref[...].reshape(R, o_ref.shape[2])
    x = x_ref[...].reshape(R, D)
    x = x + _rms(jnp.dot(o_in, wout_ref[...], preferred_element_type=F32), g1_ref[...])
    h = _rms(x, g2_ref[...]).astype(BF16)
    q = jnp.dot(h, wq_ref[...], preferred_element_type=F32).astype(BF16)
    rows = []
    for bi in range(nb):
        mk = mk_ref[bi].astype(BF16)
        mv = mv_ref[bi].astype(BF16)
        qb = q[bi * tq:(bi + 1) * tq]
        outs = []
        for hd in range(X_HEADS):
            sl = slice(hd * X_DH, (hd + 1) * X_DH)
            s = _dot_nt(qb[:, sl], mk[:, sl]) * (X_DH ** -0.5)
            p = jnp.exp(s - jnp.max(s, axis=-1, keepdims=True))
            p = p * (1.0 / jnp.sum(p, axis=-1, keepdims=True))
            outs.append(jnp.dot(p.astype(BF16), mv[:, sl], preferred_element_type=F32))
        rows.append(jnp.concatenate(outs, axis=-1).astype(BF16))
    o = rows[0] if nb == 1 else jnp.concatenate(rows, axis=0)
    y = jnp.dot(o, wo_ref[...], preferred_element_type=F32)
    y_ref[...] = (x + _rms(y, g3_ref[...])).reshape(nb, tq, D)


def _memattn(o, w_out, lo, g1, x, g2, wq, mem_k, mem_v, wo, lx, g3, tq=1024, rows_per_step=256):
    B, T, D = x.shape
    tq = min(tq, T)
    assert T % tq == 0
    nb = max(1, min(B, rows_per_step // tq)) if tq == T else 1
    assert B % nb == 0
    HD = X_HEADS * X_DH
    K = o.shape[-1]
    (mk_arr, koff, kcol), (mv_arr, voff, vcol) = mem_k, mem_v
    assert koff % nb == 0 and voff % nb == 0
    return pl.pallas_call(
        functools.partial(_memattn_body, nb),
        grid=(B // nb, T // tq),
        in_specs=[pl.BlockSpec((nb, tq, K), lambda b, i: (b, i, 0)),
                  _wspec(lo, (K, D), lambda b, i: (0, 0)),
                  pl.BlockSpec((1, D), lambda b, i: (0, 0)),
                  pl.BlockSpec((nb, tq, D), lambda b, i: (b, i, 0)),
                  pl.BlockSpec((1, D), lambda b, i: (0, 0)),
                  _wspec(lx, (D, HD), lambda b, i: (0, 0)),
                  pl.BlockSpec((nb, N_MEM, HD), lambda b, i: (koff // nb + b, 0, kcol)),
                  pl.BlockSpec((nb, N_MEM, HD), lambda b, i: (voff // nb + b, 0, vcol)),
                  _wspec(lx, (HD, D), lambda b, i: (0, 0)),
                  pl.BlockSpec((1, D), lambda b, i: (0, 0))],
        out_specs=pl.BlockSpec((nb, tq, D), lambda b, i: (b, i, 0)),
        out_shape=jax.ShapeDtypeStruct((B, T, D), F32),
        compiler_params=_cparams(("parallel", "parallel")),
        name="memattn",
    )(o, w_out, g1.reshape(1, D), x, g2.reshape(1, D), wq, mk_arr, mv_arr, wo, g3.reshape(1, D))


def _ffn_body(nf, x_ref, g4_ref, wg_ref, wu_ref, wd_ref, g5_ref, y_ref, xn_ref, acc_ref):
    f = pl.program_id(1)

    @pl.when(f == 0)
    def _():
        xn_ref[...] = _rms(x_ref[...], g4_ref[...]).astype(BF16)
        acc_ref[...] = jnp.zeros_like(acc_ref)

    xn = xn_ref[...]
    g = jnp.dot(xn, wg_ref[...], preferred_element_type=F32)
    u = jnp.dot(xn, wu_ref[...], preferred_element_type=F32)
    a = (g * _sigmoid(g)) * u
    acc_ref[...] += jnp.dot(a.astype(BF16), wd_ref[...], preferred_element_type=F32)

    @pl.when(f == nf - 1)
    def _():
        y_ref[...] = x_ref[...] + _rms(acc_ref[...], g5_ref[...])


def _ffn(x, g4, w_gu, w_down, li, g5, tm=1024, tf=1408):
    M, D = x.shape
    tm = min(tm, M)
    assert M % tm == 0 and D_FF % tf == 0
    nf = D_FF // tf
    return pl.pallas_call(
        functools.partial(_ffn_body, nf),
        grid=(M // tm, nf),
        in_specs=[pl.BlockSpec((tm, D), lambda i, f: (i, 0)),
                  pl.BlockSpec((1, D), lambda i, f: (0, 0)),
                  _wspec(li, (D, tf), lambda i, f: (0, f)),
                  _wspec(li, (D, tf), lambda i, f: (0, nf + f)),
                  _wspec(li, (tf, D), lambda i, f: (f, 0)),
                  pl.BlockSpec((1, D), lambda i, f: (0, 0))],
        out_specs=pl.BlockSpec((tm, D), lambda i, f: (i, 0)),
        out_shape=jax.ShapeDtypeStruct((M, D), F32),
        scratch_shapes=[pltpu.VMEM((tm, D), BF16), pltpu.VMEM((tm, D), F32)],
        compiler_params=_cparams(("parallel", "arbitrary")),
        name="ffn",
    )(x, g4.reshape(1, D), w_gu, w_gu, w_down, g5.reshape(1, D))


GDN_CHUNK = 64
GDN_BASE = 8


def _inverse_masks(L):
    row = lax.broadcasted_iota(jnp.int32, (L, L), 0)
    col = lax.broadcasted_iota(jnp.int32, (L, L), 1)
    base = row // GDN_BASE == col // GDN_BASE
    offs = []
    s = GDN_BASE
    while s < L:
        same_pair = row // (2 * s) == col // (2 * s)
        offs.append(jnp.logical_and(same_pair, (row // s) - (col // s) == 1))
        s *= 2
    return base, row == col, offs


def _unit_lower_inverse(As, masks, mm):
    base, diag, offs = masks
    eye = jnp.where(diag, 1.0, 0.0)
    Ps = [jnp.where(base, -A, 0.0) for A in As]
    Ts = [eye + P for P in Ps]
    k = 1
    while k < GDN_BASE // 2:
        Ps = [mm(P, P) for P in Ps]
        Ts = [T + mm(T, P) for T, P in zip(Ts, Ps)]
        k *= 2
    for off in offs:
        Ms = [_dot(T, jnp.where(off, A, 0.0)) for T, A in zip(Ts, As)]
        Ts = [T - _dot(M, T) for T, M in zip(Ts, Ms)]
    return Ts


def _gdn_prep_body(L, nch, qkv_ref, prev_ref, small_ref, buf_ref, wc_ref, par_ref,
                   wk_ref, uv_ref, qt_ref, kt_ref, at_ref, egl_ref, xbuf):
    c = pl.program_id(1)
    LA = L * nch

    @pl.when(c == 0)
    def _():
        xbuf[0:8, :] = jnp.zeros((8, GDN_QKV), F32)
        xbuf[5:8, :] = buf_ref[...]

    @pl.when(c > 0)
    def _():
        xbuf[0:8, :] = prev_ref[8:16, :].astype(F32)

    xbuf[8:8 + LA, :] = qkv_ref[...].astype(F32)
    xb = xbuf[...]
    z = xb * wc_ref[0:1, :]
    for j in range(1, CONV_W):
        z = pltpu.roll(z, 1, axis=0) + xb * wc_ref[j:j + 1, :]
    y = z[8:, :]
    y = y * _sigmoid(y)

    small = small_ref[...]
    beta_all = _sigmoid(small)
    g_all = -jnp.exp(par_ref[0:1, :]) * jax.nn.softplus(small + par_ref[1:2, :])
    incl, strict, _ = _tri_masks(L)
    tril_b = jnp.where(incl, 1.0, 0.0).astype(BF16)
    masks = _inverse_masks(L)
    H = GDN_HEADS

    units = []
    for ci in range(nch):
        rows = slice(ci * L, (ci + 1) * L)
        G_all = _cumsum_rows(tril_b, g_all[rows])
        GT = G_all.T
        egl_ref[ci] = jnp.broadcast_to(jnp.exp(G_all[L - 1:L, :]), (8, LANES))
        for h in range(H):
            qh = y[rows, h * GDN_DK:(h + 1) * GDN_DK]
            kh = y[rows, (H + h) * GDN_DK:(H + h + 1) * GDN_DK]
            vh = y[rows, 2 * H * GDN_DK + h * GDN_DV:2 * H * GDN_DK + (h + 1) * GDN_DV]
            qh = qh * lax.rsqrt(jnp.sum(qh * qh, axis=-1, keepdims=True) + EPS) * (GDN_DK ** -0.5)
            kh = kh * lax.rsqrt(jnp.sum(kh * kh, axis=-1, keepdims=True) + EPS)
            bc = beta_all[rows, h:h + 1]
            Gc = G_all[:, H + h:H + h + 1]
            Gr = GT[H + h:H + h + 1, :]
            eG = jnp.exp(Gc)
            dec = jnp.exp(jnp.where(incl, Gc - Gr, NEG_BIG))
            qkk = _dot_nt(jnp.concatenate([qh, kh], axis=0), kh)
            A = jnp.where(strict, bc * qkk[L:] * dec, 0.0)
            rhs = jnp.concatenate([bc * vh, (bc * eG) * kh], axis=-1)
            cols = slice(h * GDN_DV, (h + 1) * GDN_DV)
            qt_ref[rows, cols] = (eG * qh).astype(BF16)
            kt_ref[rows, cols] = (kh * jnp.exp(Gc[L - 1:L, :] - Gc)).astype(BF16)
            at_ref[rows, cols] = jnp.concatenate(
                [qkk[:L] * dec, jnp.zeros((L, GDN_DV - L), F32)], axis=-1).astype(BF16)
            units.append((rows, cols, A, rhs))

    Ts = _unit_lower_inverse([u[2] for u in units], masks, _dot)
    for (rows, cols, _, rhs), T in zip(units, Ts):
        sol = _dot(T, rhs)
        uv_ref[rows, cols] = sol[:, :GDN_DV]
        wk_ref[rows, cols] = sol[:, GDN_DV:].astype(BF16)


def _gdn_scan_body(L, cb, NS, wk_ref, uv_ref, qt_ref, kt_ref, at_ref, egl_ref, gate_ref, S0_ref, on_ref,
                   o_ref, Sout_ref, S_scr):
    c = pl.program_id(1)

    @pl.when(c == 0)
    def _():
        S_scr[...] = S0_ref[...]

    onorm = on_ref[...]
    H = GDN_HEADS
    for ci in range(cb):
        rows = slice(ci * L, (ci + 1) * L)
        egl = egl_ref[ci]
        hs = [slice(h * GDN_DV, (h + 1) * GDN_DV) for h in range(H)]
        Ss = [S_scr[h] for h in range(H)]
        Xs = [jnp.dot(jnp.concatenate([wk_ref[rows, hs[h]], qt_ref[rows, hs[h]]], axis=0),
                      Ss[h].astype(BF16), preferred_element_type=F32) for h in range(H)]
        Us = [(uv_ref[rows, hs[h]] - Xs[h][:L]).astype(BF16) for h in range(H)]
        os = [Xs[h][L:] + jnp.dot(at_ref[rows, h * GDN_DV:h * GDN_DV + L], Us[h],
                                  preferred_element_type=F32) for h in range(H)]
        for h in range(H):
            S_scr[h] = egl[0:1, H + h:H + h + 1] * Ss[h] + lax.dot_general(
                kt_ref[rows, hs[h]], Us[h], (((0,), (0,)), ((), ())), preferred_element_type=F32)
        for h in range(H):
            gh = gate_ref[rows, hs[h]].astype(F32)
            o_ref[rows, hs[h]] = (_rms(os[h], onorm) * (gh * _sigmoid(gh))).astype(BF16)

    @pl.when(c == NS - 1)
    def _():
        Sout_ref[...] = S_scr[...]


def _gdn_mixer(main, small, conv_buf, S0, wc_t, par, onorm, nch=4, cb=8):
    B, T, _ = main.shape
    L = min(GDN_CHUNK, T)
    assert T % L == 0
    NC = T // L
    nch = min(nch, NC)
    cb = min(cb, NC)
    assert NC % nch == 0 and NC % cb == 0
    LA = L * nch
    tok = lambda n, j: pl.BlockSpec((None, n, 1024), lambda b, c: (b, c, j))
    sds = lambda dt: jax.ShapeDtypeStruct((B, T, 1024), dt)
    wk, uv, qt, kt, at, egl = pl.pallas_call(
        functools.partial(_gdn_prep_body, L, nch),
        grid=(B, NC // nch),
        in_specs=[pl.BlockSpec((None, LA, GDN_QKV), lambda b, c: (b, c, 0)),
                  pl.BlockSpec((None, 16, GDN_QKV), lambda b, c: (b, jnp.maximum(c * (LA // 16) - 1, 0), 0)),
                  pl.BlockSpec((None, LA, LANES), lambda b, c: (b, c, 0)),
                  pl.BlockSpec((None, CONV_W - 1, GDN_QKV), lambda b, c: (b, 0, 0)),
                  pl.BlockSpec((CONV_W, GDN_QKV), lambda b, c: (0, 0)),
                  pl.BlockSpec((2, LANES), lambda b, c: (0, 0))],
        out_specs=[tok(LA, 0), tok(LA, 0), tok(LA, 0), tok(LA, 0), tok(LA, 0),
                   pl.BlockSpec((None, nch, 8, LANES), lambda b, c: (b, c, 0, 0))],
        out_shape=[sds(BF16), sds(F32), sds(BF16), sds(BF16), sds(BF16),
                   jax.ShapeDtypeStruct((B, NC, 8, LANES), F32)],
        scratch_shapes=[pltpu.VMEM((LA + 8, GDN_QKV), F32)],
        compiler_params=_cparams(("parallel", "parallel")),
        name="gdn_prep",
    )(main, main, small, conv_buf, wc_t, par)
    NS = NC // cb
    state = pl.BlockSpec((None, GDN_HEADS, GDN_DK, GDN_DV), lambda b, c: (b, 0, 0, 0))
    o, S = pl.pallas_call(
        functools.partial(_gdn_scan_body, L, cb, NS),
        grid=(B, NS),
        in_specs=[tok(cb * L, 0), tok(cb * L, 0), tok(cb * L, 0), tok(cb * L, 0), tok(cb * L, 0),
                  pl.BlockSpec((None, cb, 8, LANES), lambda b, c: (b, c, 0, 0)),
                  tok(cb * L, 3), state,
                  pl.BlockSpec((1, GDN_DV), lambda b, c: (0, 0))],
        out_specs=[tok(cb * L, 0), state],
        out_shape=[sds(BF16), jax.ShapeDtypeStruct((B, GDN_HEADS, GDN_DK, GDN_DV), F32)],
        scratch_shapes=[pltpu.VMEM((GDN_HEADS, GDN_DK, GDN_DV), F32)],
        compiler_params=_cparams(("parallel", "arbitrary")),
        name="gdn_scan",
    )(wk, uv, qt, kt, at, egl, main, S0, onorm.reshape(1, GDN_DV))
    new_buf = main[:, T - (CONV_W - 1):, :GDN_QKV].astype(F32)
    return o, new_buf, S


ML_CHUNK = 64


def _mlstm_gates(small_ref, bias_ref):
    sc = GATE_CAP * jnp.tanh((small_ref[...] + bias_ref[...]) / GATE_CAP)
    return sc, jax.nn.log_sigmoid(sc)


def _mlstm_prep_body(L, nch, qk_ref, v_ref, small_ref, bias_ref, hl_ref, st_ref, kv_ref, ks_ref):
    ig_all, lf_all = _mlstm_gates(small_ref, bias_ref)
    incl, _, _ = _tri_masks(L)
    tril_b = jnp.where(incl, 1.0, 0.0).astype(BF16)
    lane = lax.broadcasted_iota(jnp.int32, (1, LANES), 1)
    H = ML_HEADS
    units = [(ci, h) for ci in range(nch) for h in range(H)]
    rows_of = lambda ci: slice(ci * L, (ci + 1) * L)
    q_of = lambda ci, h: qk_ref[rows_of(ci), h * ML_DQK:(h + 1) * ML_DQK]
    k_of = lambda ci, h: qk_ref[rows_of(ci), (H + h) * ML_DQK:(H + h + 1) * ML_DQK].astype(F32) * (ML_DQK ** -0.5)
    v_of = lambda ci, h: v_ref[rows_of(ci), h * ML_DV:(h + 1) * ML_DV]

    gates = []
    for ci in range(nch):
        b_all = _cumsum_rows(tril_b, lf_all[rows_of(ci)])
        igc = ig_all[rows_of(ci)]
        gates.append((b_all, igc, b_all.T, igc.T))
    qk = [_dot_nt(q_of(ci, h), k_of(ci, h)) for ci, h in units]
    Wl, dmaxs = [], []
    for (ci, h), s in zip(units, qk):
        b_all, igc, bT, igT = gates[ci]
        D = jnp.where(incl, b_all[:, H + h:H + h + 1] + (igT[h:h + 1, :] - bT[H + h:H + h + 1, :]), NEG_BIG)
        dmax = jnp.max(D, axis=-1, keepdims=True)
        dmaxs.append(dmax)
        Wl.append(jnp.exp(D - dmax) * s)
    for (ci, h), w in zip(units, Wl):
        hl_ref[rows_of(ci), h * ML_DV:(h + 1) * ML_DV] = _dot(w, v_of(ci, h))
    wks = []
    for (ci, h), dmax in zip(units, dmaxs):
        b_all, igc, _, _ = gates[ci]
        b_c = b_all[:, H + h:H + h + 1]
        wks.append(jnp.exp(b_c[L - 1:L, :] - b_c + igc[:, h:h + 1] - dmax[L - 1:L, :]) * k_of(ci, h))
    for (ci, h), wk in zip(units, wks):
        kv_ref[ci, h] = _dot_tn(wk, v_of(ci, h))
        ks_ref[ci, h:h + 1, :] = jnp.sum(wk, axis=0, keepdims=True)
    for ci in range(nch):
        stats = jnp.zeros((L, LANES), F32)
        for h in range(H):
            u = ci * H + h
            stats = jnp.where(lane == h, gates[ci][0][:, H + h:H + h + 1],
                              jnp.where(lane == H + h, dmaxs[u],
                                        jnp.where(lane == 2 * H + h, jnp.sum(Wl[u], axis=-1, keepdims=True), stats)))
        st_ref[rows_of(ci), :] = stats


def _mlstm_scan_body(L, cb, NS, q_ref, hl_ref, st_ref, kv_ref, ks_ref, op_ref, C0_ref, n0_ref, m0_ref, on_ref,
                     o_ref, Cout_ref, nout_ref, mout_ref, C_scr, n_scr, m_scr):
    c = pl.program_id(1)

    @pl.when(c == 0)
    def _():
        C_scr[...] = C0_ref[...]
        n_scr[...] = n0_ref[...]
        m_scr[...] = m0_ref[...]

    onorm = on_ref[...]
    H = ML_HEADS
    units = [(ci, h) for ci in range(cb) for h in range(H)]
    rows_of = lambda ci: slice(ci * L, (ci + 1) * L)
    sts = [st_ref[rows_of(ci), :] for ci in range(cb)]
    pre = {}
    for h in range(H):
        m_prev = m_scr[h:h + 1, 0:1]
        n_prev = n_scr[h:h + 1, :]
        C = C_scr[h]
        for ci in range(cb):
            b_c = sts[ci][:, h:h + 1]
            dmax = sts[ci][:, H + h:H + h + 1]
            mt = jnp.maximum(b_c + m_prev, dmax)
            pre[(ci, h)] = (C, n_prev, m_prev, mt)
            mL = mt[L - 1:L, :]
            fL = jnp.exp(b_c[L - 1:L, :] + m_prev - mL)
            gL = jnp.exp(dmax[L - 1:L, :] - mL)
            C = fL * C + gL * kv_ref[ci, h]
            n_prev = fL * n_prev + gL * ks_ref[ci, h:h + 1, :]
            m_prev = mL
        C_scr[h] = C
        n_scr[h:h + 1, :] = n_prev
        m_scr[h:h + 1, :] = jnp.broadcast_to(m_prev, (1, LANES))

    qs = {u: q_ref[rows_of(u[0]), u[1] * ML_DQK:(u[1] + 1) * ML_DQK].astype(F32) for u in units}
    qC = {u: _dot(qs[u], pre[u][0]) for u in units}
    hs = {}
    for u in units:
        ci, h = u
        _, n_prev, m_prev, mt = pre[u]
        b_c = sts[ci][:, h:h + 1]
        dmax = sts[ci][:, H + h:H + h + 1]
        sw = sts[ci][:, 2 * H + h:2 * H + h + 1]
        e_i = jnp.exp(b_c + m_prev - mt)
        e_l = jnp.exp(dmax - mt)
        num = e_i * qC[u] + e_l * hl_ref[rows_of(ci), h * ML_DV:(h + 1) * ML_DV]
        den = e_i * jnp.sum(qs[u] * n_prev, axis=-1, keepdims=True) + e_l * sw
        hs[u] = num / jnp.maximum(jnp.abs(den), jnp.exp(-mt))
    for u in units:
        ci, h = u
        og = op_ref[rows_of(ci), h * ML_DV:(h + 1) * ML_DV].astype(F32)
        o_ref[rows_of(ci), h * ML_DV:(h + 1) * ML_DV] = (_sigmoid(og) * _rms(hs[u], onorm)).astype(BF16)

    @pl.when(c == NS - 1)
    def _():
        Cout_ref[...] = C_scr[...]
        nout_ref[...] = n_scr[...]
        mout_ref[...] = m_scr[...]


def _mlstm_mixer(main, small, C0, n0, m0, bias, onorm, nch=4, cb=4):
    B, T, _ = main.shape
    L = min(ML_CHUNK, T)
    assert T % L == 0
    NC = T // L
    nch = min(nch, NC)
    cb = min(cb, NC)
    assert NC % nch == 0 and NC % cb == 0
    LA = L * nch
    H = ML_HEADS
    hl, st, kv, ks = pl.pallas_call(
        functools.partial(_mlstm_prep_body, L, nch),
        grid=(B, NC // nch),
        in_specs=[pl.BlockSpec((None, LA, 1024), lambda b, c: (b, c, 0)),
                  pl.BlockSpec((None, LA, 1024), lambda b, c: (b, c, 1)),
                  pl.BlockSpec((None, LA, LANES), lambda b, c: (b, c, 0)),
                  pl.BlockSpec((1, LANES), lambda b, c: (0, 0))],
        out_specs=[pl.BlockSpec((None, LA, 1024), lambda b, c: (b, c, 0)),
                   pl.BlockSpec((None, LA, LANES), lambda b, c: (b, c, 0)),
                   pl.BlockSpec((None, nch, H, ML_DQK, ML_DV), lambda b, c: (b, c, 0, 0, 0)),
                   pl.BlockSpec((None, nch, H, ML_DQK), lambda b, c: (b, c, 0, 0))],
        out_shape=[jax.ShapeDtypeStruct((B, T, 1024), F32),
                   jax.ShapeDtypeStruct((B, T, LANES), F32),
                   jax.ShapeDtypeStruct((B, NC, H, ML_DQK, ML_DV), F32),
                   jax.ShapeDtypeStruct((B, NC, H, ML_DQK), F32)],
        compiler_params=_cparams(("parallel", "parallel")),
        name="mlstm_prep",
    )(main, main, small, bias)
    NS = NC // cb
    m0b = jnp.broadcast_to(m0[:, :, None], (B, H, LANES))
    st_C = pl.BlockSpec((None, H, ML_DQK, ML_DV), lambda b, c: (b, 0, 0, 0))
    st_n = pl.BlockSpec((None, H, ML_DQK), lambda b, c: (b, 0, 0))
    st_m = pl.BlockSpec((None, H, LANES), lambda b, c: (b, 0, 0))
    return pl.pallas_call(
        functools.partial(_mlstm_scan_body, L, cb, NS),
        grid=(B, NS),
        in_specs=[pl.BlockSpec((None, cb * L, H * ML_DQK), lambda b, c: (b, c, 0)),
                  pl.BlockSpec((None, cb * L, 1024), lambda b, c: (b, c, 0)),
                  pl.BlockSpec((None, cb * L, LANES), lambda b, c: (b, c, 0)),
                  pl.BlockSpec((None, cb, H, ML_DQK, ML_DV), lambda b, c: (b, c, 0, 0, 0)),
                  pl.BlockSpec((None, cb, H, ML_DQK), lambda b, c: (b, c, 0, 0)),
                  pl.BlockSpec((None, cb * L, 1024), lambda b, c: (b, c, 2)),
                  st_C, st_n, st_m,
                  pl.BlockSpec((1, ML_DV), lambda b, c: (0, 0))],
        out_specs=[pl.BlockSpec((None, cb * L, 1024), lambda b, c: (b, c, 0)), st_C, st_n, st_m],
        out_shape=[jax.ShapeDtypeStruct((B, T, 1024), BF16),
                   jax.ShapeDtypeStruct((B, H, ML_DQK, ML_DV), F32),
                   jax.ShapeDtypeStruct((B, H, ML_DQK), F32),
                   jax.ShapeDtypeStruct((B, H, LANES), F32)],
        scratch_shapes=[pltpu.VMEM((H, ML_DQK, ML_DV), F32),
                        pltpu.VMEM((H, ML_DQK), F32),
                        pltpu.VMEM((H, LANES), F32)],
        compiler_params=_cparams(("parallel", "arbitrary")),
        name="mlstm_scan",
    )(main, hl, st, kv, ks, main, C0, n0, m0b, onorm.reshape(1, ML_DV))


FOX_BIAS_LANE = FOX_DH


def _fox_prep_body(q_ref, k_ref, v_ref, small_ref, bf_ref, qn_ref, kn_ref,
                   qh_ref, k32_ref, v32_ref, lf_ref):
    r = lax.broadcasted_iota(jnp.int32, (LANES, LANES), 0) // FOX_DH
    cc = lax.broadcasted_iota(jnp.int32, (LANES, LANES), 1) // FOX_DH
    avg = jnp.where(r == cc, 1.0 / FOX_DH, 0.0).astype(BF16)

    def head_rms(x, gain):
        parts = []
        for t in range(x.shape[1] // LANES):
            xt = x[:, t * LANES:(t + 1) * LANES]
            hi, lo = _split2(xt * xt)
            ms = (jnp.dot(hi, avg, preferred_element_type=F32)
                  + jnp.dot(lo, avg, preferred_element_type=F32))
            parts.append(xt * lax.rsqrt(ms + EPS))
        return jnp.concatenate(parts, axis=-1) * gain

    qn = head_rms(q_ref[...].astype(F32), qn_ref[...]) * (LOG2E * FOX_DH ** -0.5)
    lane = lax.broadcasted_iota(jnp.int32, (1, LANES), 1)
    ones3 = jnp.where(lane < FOX_BIAS_LANE + 3, 1.0, 0.0)
    for h in range(FOX_HEADS):
        xt = qn[:, (h // 2) * LANES:(h // 2 + 1) * LANES]
        if h % 2:
            xt = pltpu.roll(xt, FOX_DH, axis=1)
        qh_ref[h] = jnp.where(lane < FOX_DH, xt, ones3).astype(BF16)
    k32_ref[...] = head_rms(k_ref[...].astype(F32), kn_ref[...])
    v32_ref[...] = v_ref[...].astype(F32)
    lf_ref[...] = jax.nn.log_sigmoid(small_ref[...] + bf_ref[...])


def _fox_prep(main, small, bf_row, qn_row, kn_row, tm=512):
    B, T, _ = main.shape
    tm = min(tm, T)
    assert T % tm == 0
    blk = lambda j: pl.BlockSpec((None, tm, 1024), lambda b, i: (b, i, j))
    row = lambda n: pl.BlockSpec((1, n), lambda b, i: (0, 0))
    o1024 = pl.BlockSpec((None, tm, 1024), lambda b, i: (b, i, 0))
    return pl.pallas_call(
        _fox_prep_body,
        grid=(B, T // tm),
        in_specs=[blk(0), blk(1), blk(2),
                  pl.BlockSpec((None, tm, LANES), lambda b, i: (b, i, 0)),
                  row(LANES), row(1024), row(1024)],
        out_specs=[pl.BlockSpec((None, FOX_HEADS, tm, LANES), lambda b, i: (b, 0, i, 0)),
                   o1024, o1024,
                   pl.BlockSpec((None, tm, LANES), lambda b, i: (b, i, 0))],
        out_shape=[jax.ShapeDtypeStruct((B, FOX_HEADS, T, LANES), BF16),
                   jax.ShapeDtypeStruct((B, T, 1024), F32),
                   jax.ShapeDtypeStruct((B, T, 1024), F32),
                   jax.ShapeDtypeStruct((B, T, LANES), F32)],
        compiler_params=_cparams(("parallel", "parallel")),
        name="fox_prep",
    )(main, main, main, small, bf_row, qn_row, kn_row)


def _fox_kv_body(tm, n_past, *refs):
    if n_past:
        kp_ref, vp_ref, lp_ref, k_ref, v_ref, lf_ref, wk_ref, wv_ref, kh_ref, vh_ref, carry = refs
    else:
        k_ref, v_ref, lf_ref, wk_ref, wv_ref, kh_ref, vh_ref, carry = refs
    i = pl.program_id(1)

    @pl.when(i == 0)
    def _():
        carry[...] = jnp.zeros_like(carry)

    k = k_ref[...]
    v = v_ref[...]
    lf = lf_ref[...]
    if n_past:
        past = i < n_past
        k = jnp.where(past, kp_ref[...], k)
        v = jnp.where(past, vp_ref[...], v)
        lf = jnp.where(past, lp_ref[...], lf)
    incl, _, _ = _tri_masks(tm)
    tril_b = jnp.where(incl, 1.0, 0.0).astype(BF16)
    F = _cumsum_rows(tril_b, lf) + carry[0:1, :]
    carry[0:1, :] = F[tm - 1:tm, :]
    nb = -LOG2E * F
    b1 = nb.astype(BF16).astype(F32)
    r1 = nb - b1
    b2 = r1.astype(BF16).astype(F32)
    b3 = r1 - b2
    lane = lax.broadcasted_iota(jnp.int32, (1, LANES), 1)
    ones1 = jnp.where(lane == FOX_BIAS_LANE, 1.0, 0.0)
    parts = jnp.where(lane < FOX_HEADS, b1,
                      jnp.where(lane < 2 * FOX_HEADS, pltpu.roll(b2, FOX_HEADS, axis=1),
                                jnp.where(lane < 3 * FOX_HEADS, pltpu.roll(b3, 2 * FOX_HEADS, axis=1), 0.0)))
    parts_b = parts.astype(BF16)
    kb = k.astype(BF16)
    vb = v.astype(BF16)
    for h in range(FOX_HEADS):
        tile = slice((h // 2) * LANES, (h // 2 + 1) * LANES)
        kh_ref[h] = jnp.dot(jnp.concatenate([kb[:, tile], parts_b], axis=-1), wk_ref[h],
                            preferred_element_type=F32).astype(BF16)
        vh_ref[h] = (jnp.dot(vb[:, tile], wv_ref[h % 2], preferred_element_type=F32) + ones1).astype(BF16)


def _fox_kv(k_new, v_new, lf_new, past=None, tm=512):
    B, T, _ = k_new.shape
    if past is None:
        tm = min(tm, T)
        assert T % tm == 0
        n_past, n_new = 0, T // tm
        args, specs = [], []
    else:
        k_past, v_past, lf_past = past
        P = k_past.shape[1]
        tm = min(tm, P)
        assert P % tm == 0 and T <= tm
        n_past, n_new = P // tm, 1
        pad = lambda a: jnp.pad(a, ((0, 0), (0, tm - T), (0, 0)))
        k_new, v_new, lf_new = pad(k_new), pad(v_new), pad(lf_new)
        clamp = lambda b, i: (b, jnp.minimum(i, n_past - 1), 0)
        args = [k_past, v_past, lf_past]
        specs = [pl.BlockSpec((None, tm, 1024), clamp), pl.BlockSpec((None, tm, 1024), clamp),
                 pl.BlockSpec((None, tm, LANES), clamp)]
    new_idx = lambda b, i: (b, jnp.maximum(i - n_past, 0), 0)
    N = (n_past + n_new) * tm
    hm = pl.BlockSpec((None, FOX_HEADS, tm, LANES), lambda b, i: (b, 0, i, 0))
    wk = np.zeros((FOX_HEADS, 2 * LANES, LANES), np.float32)
    wv = np.zeros((2, LANES, LANES), np.float32)
    for h in range(FOX_HEADS):
        for d in range(FOX_DH):
            wk[h, FOX_DH * (h % 2) + d, d] = 1.0
            wv[h % 2, FOX_DH * (h % 2) + d, d] = 1.0
        for g in range(3):
            wk[h, LANES + g * FOX_HEADS + h, FOX_BIAS_LANE + g] = 1.0
    wk, wv = jnp.asarray(wk, BF16), jnp.asarray(wv, BF16)
    return pl.pallas_call(
        functools.partial(_fox_kv_body, tm, n_past),
        grid=(B, n_past + n_new),
        in_specs=specs + [pl.BlockSpec((None, tm, 1024), new_idx),
                          pl.BlockSpec((None, tm, 1024), new_idx),
                          pl.BlockSpec((None, tm, LANES), new_idx),
                          pl.BlockSpec((FOX_HEADS, 2 * LANES, LANES), lambda b, i: (0, 0, 0)),
                          pl.BlockSpec((2, LANES, LANES), lambda b, i: (0, 0, 0))],
        out_specs=[hm, hm],
        out_shape=[jax.ShapeDtypeStruct((B, FOX_HEADS, N, LANES), BF16),
                   jax.ShapeDtypeStruct((B, FOX_HEADS, N, LANES), BF16)],
        scratch_shapes=[pltpu.VMEM((8, LANES), F32)],
        compiler_params=_cparams(("parallel", "arbitrary")),
        name="fox_kv",
    )(*args, k_new, v_new, lf_new, wk, wv)


def _flash_body(P, tq, tk, rq, hp, qi_tab, kj_tab, q_ref, k_ref, v_ref, gate_ref, o_ref, m_scr, acc_scr):
    s_idx = pl.program_id(2)
    qi = qi_tab[s_idx]
    kj = kj_tab[s_idx]
    last_kj = (P + (qi + 1) * tq - 1) // tk
    aligned = tq == tk and P % tk == 0

    @pl.when(kj == 0)
    def _():
        m_scr[...] = jnp.full(m_scr.shape, NEG_BIG, F32)
        acc_scr[...] = jnp.zeros_like(acc_scr)

    def step(masked):
        units = [(hh, r) for r in range(tq // rq) for hh in range(2 * hp)]

        def logits(u):
            hh, r = u
            rows = slice(r * rq, (r + 1) * rq)
            nk = min(tk, (r + 1) * rq) if (masked and aligned) else tk
            s = lax.dot_general(q_ref[hh, rows, :], k_ref[hh, 0:nk, :], (((1,), (1,)), ((), ())),
                                preferred_element_type=F32)
            if masked:
                qpos = P + qi * tq + r * rq + lax.broadcasted_iota(jnp.int32, (rq, nk), 0)
                kpos = kj * tk + lax.broadcasted_iota(jnp.int32, (rq, nk), 1)
                s = jnp.where(kpos <= qpos, s, NEG_BIG)
            return s, nk

        def probs(u, s):
            hh, r = u
            rows = slice(r * rq, (r + 1) * rq)
            m_prev = m_scr[hh, rows, :][:, 0:1]
            m_new = jnp.maximum(m_prev, jnp.max(s, axis=-1, keepdims=True))
            m_scr[hh, rows, :] = jnp.broadcast_to(m_new, (rq, LANES))
            return jnp.exp2(m_prev - m_new), jnp.exp2(s - m_new).astype(BF16)

        def accumulate(u, alpha, p, nk):
            hh, r = u
            rows = slice(r * rq, (r + 1) * rq)
            acc_scr[hh, rows, :] = alpha * acc_scr[hh, rows, :] + jnp.dot(
                p, v_ref[hh, 0:nk, :], preferred_element_type=F32)

        n = len(units)
        pend_s = {0: logits(units[0])}
        pend_p = {}
        for i in range(n):
            if i + 1 < n:
                pend_s[i + 1] = logits(units[i + 1])
            s, nk = pend_s.pop(i)
            pend_p[i] = probs(units[i], s) + (nk,)
            if i >= 1:
                accumulate(units[i - 1], *pend_p.pop(i - 1))
        accumulate(units[n - 1], *pend_p.pop(n - 1))

    needs_mask = (kj + 1) * tk - 1 > P + qi * tq

    @pl.when(needs_mask)
    def _():
        step(True)

    @pl.when(jnp.logical_not(needs_mask))
    def _():
        step(False)

    @pl.when(kj == last_kj)
    def _():
        lane = lax.broadcasted_iota(jnp.int32, (1, LANES), 1)
        for pp in range(hp):
            a0 = acc_scr[2 * pp]
            a1 = acc_scr[2 * pp + 1]
            o0 = a0 * (1.0 / a0[:, FOX_BIAS_LANE:FOX_BIAS_LANE + 1])
            o1 = a1 * (1.0 / a1[:, FOX_BIAS_LANE:FOX_BIAS_LANE + 1])
            o = jnp.where(lane < FOX_DH, o0, pltpu.roll(o1, FOX_DH, axis=1))
            g = gate_ref[:, pp * LANES:(pp + 1) * LANES].astype(F32)
            o_ref[:, pp * LANES:(pp + 1) * LANES] = (o * _sigmoid(g)).astype(BF16)


def _fox_flash(qh, kh, vh, gate_src, P, tq, tk, rq, hp=1):
    B, _, T, _ = qh.shape
    N = kh.shape[2]
    assert T % tq == 0 and N % tk == 0 and tq % rq == 0 and (FOX_HEADS // 2) % hp == 0
    nq = T // tq
    pairs = [(qi, kj) for qi in range(nq) for kj in range((P + (qi + 1) * tq - 1) // tk + 1)]
    qi_tab = jnp.asarray(np.array([p[0] for p in pairs], np.int32))
    kj_tab = jnp.asarray(np.array([p[1] for p in pairs], np.int32))
    HP = FOX_HEADS // (2 * hp)
    gate_blk0 = 3 * (1024 // (LANES * hp))
    grid_spec = pltpu.PrefetchScalarGridSpec(
        num_scalar_prefetch=2,
        grid=(B, HP, len(pairs)),
        in_specs=[pl.BlockSpec((None, 2 * hp, tq, LANES), lambda b, p, s, qt, kt: (b, p, qt[s], 0)),
                  pl.BlockSpec((None, 2 * hp, tk, LANES), lambda b, p, s, qt, kt: (b, p, kt[s], 0)),
                  pl.BlockSpec((None, 2 * hp, tk, LANES), lambda b, p, s, qt, kt: (b, p, kt[s], 0)),
                  pl.BlockSpec((None, tq, LANES * hp), lambda b, p, s, qt, kt: (b, qt[s], gate_blk0 + p))],
        out_specs=pl.BlockSpec((None, tq, LANES * hp), lambda b, p, s, qt, kt: (b, qt[s], p)),
        scratch_shapes=[pltpu.VMEM((2 * hp, tq, LANES), F32),
                        pltpu.VMEM((2 * hp, tq, LANES), F32)],
    )
    return pl.pallas_call(
        functools.partial(_flash_body, P, tq, tk, rq, hp),
        grid_spec=grid_spec,
        out_shape=jax.ShapeDtypeStruct((B, T, 1024), BF16),
        compiler_params=_cparams(("parallel", "parallel", "arbitrary")),
        name="fox_flash",
    )(qi_tab, kj_tab, qh, kh, vh, gate_src)


def _pad_lanes(a, offset=0):
    n = a.shape[-1]
    pads = [(0, 0)] * (a.ndim - 1) + [(offset, LANES - n - offset)]
    return jnp.pad(a, pads)


def _trunk(x, gdn_conv, gdn_S, ml_C, ml_n, ml_m, fox_k, fox_v, fox_lf, mem_k, mem_v, W):
    B, T, D = x.shape
    M = B * T
    x2 = x.reshape(M, D)
    conv_o, S_o, C_o, n_o, m_o, k_o, v_o, lf_o = [], [], [], [], [], [], [], []
    for layer in range(DEPTH):
        kind, j = layer % N_MIXERS, layer // N_MIXERS
        gains = W["norm_gains"][layer]
        if kind == 0:
            main, small = _prenorm_mm(x2, gains[0], W["gdn_w_in"], j, GDN_MAIN, W["gdn_w_small"], out_dtype=F32)
            o, buf, S = _gdn_mixer(main.reshape(B, T, GDN_MAIN), small.reshape(B, T, LANES),
                                   gdn_conv[j], gdn_S[j], W["gdn_wc_t"][j], W["gdn_par"][j],
                                   W["gdn_onorm"][j])
            conv_o.append(buf)
            S_o.append(S)
            w_out = W["gdn_w_out"]
        elif kind == 1:
            main, small = _prenorm_mm(x2, gains[0], W["ml_w_in"], j, ML_MAIN, W["ml_w_small"], tn=ML_MAIN // 2)
            o, C, n, m = _mlstm_mixer(main.reshape(B, T, ML_MAIN), small.reshape(B, T, LANES),
                                      ml_C[j], ml_n[j], ml_m[j], W["ml_bias"][j], W["ml_onorm"][j])
            C_o.append(C)
            n_o.append(n)
            m_o.append(m[:, :, 0])
            w_out = W["ml_w_out"]
        else:
            main, small = _prenorm_mm(x2, gains[0], W["fox_w_in"], j, FOX_MAIN, W["fox_w_small"], tn=FOX_MAIN // 2)
            main3 = main.reshape(B, T, FOX_MAIN)
            qh, k32, v32, lfp = _fox_prep(main3, small.reshape(B, T, LANES), W["fox_bf"][j],
                                          W["fox_qn"][j], W["fox_kn"][j])
            P = fox_k[j].shape[1]
            if P == 0:
                kh, vh = _fox_kv(k32, v32, lfp)
                tq, tk, rq, hp = min(1024, T), min(1024, T), 256, 1
            else:
                kh, vh = _fox_kv(k32, v32, lfp, past=(fox_k[j].reshape(B, P, 1024), fox_v[j].reshape(B, P, 1024),
                                                      _pad_lanes(fox_lf[j])), tm=256)
                tq, tk, rq, hp = T, kh.shape[2], T, 4
            o = _fox_flash(qh, kh, vh, main3, P, tq, tk, rq, hp)
            k_o.append(k32.reshape(B, T, FOX_HEADS, FOX_DH))
            v_o.append(v32.reshape(B, T, FOX_HEADS, FOX_DH))
            lf_o.append(lfp[:, :, :FOX_HEADS])
            w_out = W["fox_w_out"]
        x2 = _memattn(o, w_out, j, gains[1], x2.reshape(B, T, D), gains[2], W["x_w_q"],
                      mem_k[layer], mem_v[layer], W["x_w_o"], layer, gains[3]).reshape(M, D)
        x2 = _ffn(x2, gains[4], W["ffn_w_gu"], W["ffn_w_down"], layer, gains[5])
    return (x2.reshape(B, T, D), jnp.stack(conv_o), jnp.stack(S_o), jnp.stack(C_o), jnp.stack(n_o),
            jnp.stack(m_o), jnp.stack(k_o), jnp.stack(v_o), jnp.stack(lf_o))


def kernel(x_prompt, x_sample, state_gdn_conv, state_gdn_S, state_mlstm_C, state_mlstm_n, state_mlstm_m, cache_fox_k, cache_fox_v, cache_fox_logf, cache_mem_k, cache_mem_v, mem_prompt, norm_gains, gdn_w_in, gdn_w_conv, gdn_a_log, gdn_dt_bias, gdn_onorm, gdn_w_out, ml_w_in, ml_b_i, ml_b_f, ml_onorm, ml_w_out, fox_w_in, fox_b_f, fox_qk_norm, fox_w_out, x_w_q, x_w_kv, x_w_o, ffn_w_gu, ffn_w_down):
    B = x_prompt.shape[0]
    DB = x_sample.shape[0]
    n_a, n_b, n_c = gdn_w_in.shape[0], ml_w_in.shape[0], fox_w_in.shape[0]
    W = {
        "norm_gains": norm_gains,
        "gdn_w_in": gdn_w_in.astype(BF16),
        "gdn_w_small": _pad_lanes(gdn_w_in[:, :, GDN_MAIN:]).astype(BF16),
        "gdn_wc_t": jnp.transpose(gdn_w_conv, (0, 2, 1)),
        "gdn_par": jnp.stack([_pad_lanes(gdn_a_log, GDN_HEADS), _pad_lanes(gdn_dt_bias, GDN_HEADS)], axis=1),
        "gdn_onorm": gdn_onorm,
        "gdn_w_out": gdn_w_out.astype(BF16),
        "ml_w_in": ml_w_in.astype(BF16),
        "ml_w_small": _pad_lanes(ml_w_in[:, :, ML_MAIN:]).astype(BF16),
        "ml_bias": _pad_lanes(jnp.concatenate([ml_b_i, ml_b_f], axis=-1))[:, None, :],
        "ml_onorm": ml_onorm,
        "ml_w_out": ml_w_out.astype(BF16),
        "fox_w_in": fox_w_in.astype(BF16),
        "fox_w_small": _pad_lanes(fox_w_in[:, :, FOX_MAIN:]).astype(BF16),
        "fox_bf": _pad_lanes(fox_b_f)[:, None, :],
        "fox_qn": jnp.tile(fox_qk_norm[:, 0, :], (1, FOX_HEADS))[:, None, :],
        "fox_kn": jnp.tile(fox_qk_norm[:, 1, :], (1, FOX_HEADS))[:, None, :],
        "fox_w_out": fox_w_out.astype(BF16),
        "x_w_q": x_w_q.astype(BF16),
        "x_w_o": x_w_o.astype(BF16),
        "ffn_w_gu": ffn_w_gu.astype(BF16),
        "ffn_w_down": ffn_w_down.astype(BF16),
    }

    mem2 = mem_prompt.reshape(B * N_MEM, D_MODEL)
    x_w_kv_b = x_w_kv.astype(BF16)
    kv = [_prenorm_mm(mem2, norm_gains[l, 6], x_w_kv_b, l, 2 * D_MODEL, out_dtype=F32).reshape(B, N_MEM, 2 * D_MODEL)
          for l in range(DEPTH)]
    p_mem_k = jnp.stack([a[:, :, :D_MODEL].reshape(B, N_MEM, X_HEADS, X_DH) for a in kv])
    p_mem_v = jnp.stack([a[:, :, D_MODEL:].reshape(B, N_MEM, X_HEADS, X_DH) for a in kv])

    cmk = cache_mem_k.reshape(DEPTH * DB, N_MEM, D_MODEL)
    cmv = cache_mem_v.reshape(DEPTH * DB, N_MEM, D_MODEL)
    zeros = lambda *s: jnp.zeros(s, F32)
    prompt = _trunk(
        x_prompt,
        zeros(n_a, B, CONV_W - 1, GDN_QKV), zeros(n_a, B, GDN_HEADS, GDN_DK, GDN_DV),
        zeros(n_b, B, ML_HEADS, ML_DQK, ML_DV), zeros(n_b, B, ML_HEADS, ML_DQK), zeros(n_b, B, ML_HEADS),
        zeros(n_c, B, 0, FOX_HEADS, FOX_DH), zeros(n_c, B, 0, FOX_HEADS, FOX_DH), zeros(n_c, B, 0, FOX_HEADS),
        [(a, 0, 0) for a in kv], [(a, 0, 1) for a in kv], W)
    sample = _trunk(
        x_sample, state_gdn_conv, state_gdn_S, state_mlstm_C, state_mlstm_n, state_mlstm_m,
        cache_fox_k, cache_fox_v, cache_fox_logf,
        [(cmk, l * DB, 0) for l in range(DEPTH)], [(cmv, l * DB, 0) for l in range(DEPTH)], W)
    (y_p, p_conv, p_S, p_C, p_n, p_m, p_k, p_v, p_lf) = prompt
    (y_s, s_conv, s_S, s_C, s_n, s_m, s_k, s_v, s_lf) = sample
    return (y_p, y_s, p_conv, p_S, p_C, p_n, p_m, p_k, p_v, p_lf, p_mem_k, p_mem_v,
            s_conv, s_S, s_C, s_n, s_m, s_k, s_v, s_lf)
```

```python
import functools

import numpy as np
import jax
import jax.numpy as jnp
from jax import lax
from jax.experimental import pallas as pl
from jax.experimental.pallas import tpu as pltpu

F32 = jnp.float32
BF16 = jnp.bfloat16

D_MODEL = 1024
DEPTH = 4
N_MIXERS = 3
EPS = 1e-6
CONV_W = 4

GDN_HEADS = 8
GDN_DK = 128
GDN_DV = 128
GDN_QKV = 3072
GDN_MAIN = 4096

ML_HEADS = 4
ML_DQK = 128
ML_DV = 256
ML_MAIN = 3072
GATE_CAP = 15.0

FOX_HEADS = 16
FOX_DH = 64
FOX_MAIN = 4096

N_MEM = 256
X_HEADS = 4
X_DH = 256
D_FF = 2816

LANES = 128
NEG_BIG = -1e30
LOG2E = 1.4426950408889634

VMEM_LIMIT = 48 * 1024 * 1024


def _wspec(li, blk, imap):
    return pl.BlockSpec((None,) + blk, lambda *g: (li,) + imap(*g))


def _cparams(sem):
    return pltpu.CompilerParams(dimension_semantics=sem, vmem_limit_bytes=VMEM_LIMIT)


def _rms(x, g):
    return x * lax.rsqrt(jnp.mean(x * x, axis=-1, keepdims=True) + EPS) * g


def _sigmoid(x):
    return 1.0 / (1.0 + jnp.exp(-x))


def _dot(a, b):
    return jnp.dot(a.astype(BF16), b.astype(BF16), preferred_element_type=F32)


def _dot_nt(a, b):
    return lax.dot_general(a.astype(BF16), b.astype(BF16), (((1,), (1,)), ((), ())),
                           preferred_element_type=F32)


def _dot_tn(a, b):
    return lax.dot_general(a.astype(BF16), b.astype(BF16), (((0,), (0,)), ((), ())),
                           preferred_element_type=F32)


def _split2(x):
    hi = x.astype(BF16)
    lo = (x - hi.astype(F32)).astype(BF16)
    return hi, lo


def _dot_x3(a, b):
    ah, al = _split2(a)
    bh, bl = _split2(b)
    d = lambda u, v: jnp.dot(u, v, preferred_element_type=F32)
    return d(ah, bh) + (d(ah, bl) + d(al, bh))


def _cumsum_rows(tril_b, x):
    h = x.astype(BF16)
    r = x - h.astype(F32)
    m = r.astype(BF16)
    l = (r - m.astype(F32)).astype(BF16)
    d = lambda v: jnp.dot(tril_b, v, preferred_element_type=F32)
    return d(h) + (d(m) + d(l))


def _tri_masks(L):
    row = lax.broadcasted_iota(jnp.int32, (L, L), 0)
    col = lax.broadcasted_iota(jnp.int32, (L, L), 1)
    return row >= col, row > col, row == col


def _prenorm_mm_body(has_small, *refs):
    if has_small:
        x_ref, g_ref, w_ref, ws_ref, o_ref, os_ref, xn_ref = refs
    else:
        x_ref, g_ref, w_ref, o_ref, xn_ref = refs

    @pl.when(pl.program_id(1) == 0)
    def _():
        xb = _rms(x_ref[...], g_ref[...]).astype(BF16)
        xn_ref[...] = xb
        if has_small:
            os_ref[...] = jnp.dot(xb, ws_ref[...], preferred_element_type=F32)

    o_ref[...] = jnp.dot(xn_ref[...], w_ref[...], preferred_element_type=F32).astype(o_ref.dtype)


def _prenorm_mm(x, gain, w_main, li, n_out, w_small=None, out_dtype=BF16, tm=1024, tn=1024):
    M, D = x.shape
    N = n_out
    tm = min(tm, M)
    tn = min(tn, N)
    assert M % tm == 0 and N % tn == 0
    has_small = w_small is not None
    in_specs = [pl.BlockSpec((tm, D), lambda i, j: (i, 0)),
                pl.BlockSpec((1, D), lambda i, j: (0, 0)),
                _wspec(li, (D, tn), lambda i, j: (0, j))]
    out_specs = [pl.BlockSpec((tm, tn), lambda i, j: (i, j))]
    out_shape = [jax.ShapeDtypeStruct((M, N), out_dtype)]
    args = [x, gain.reshape(1, D), w_main]
    if has_small:
        in_specs.append(_wspec(li, (D, LANES), lambda i, j: (0, 0)))
        out_specs.append(pl.BlockSpec((tm, LANES), lambda i, j: (i, 0)))
        out_shape.append(jax.ShapeDtypeStruct((M, LANES), F32))
        args.append(w_small)
    res = pl.pallas_call(
        functools.partial(_prenorm_mm_body, has_small),
        grid=(M // tm, N // tn),
        in_specs=in_specs, out_specs=out_specs, out_shape=out_shape,
        scratch_shapes=[pltpu.VMEM((tm, D), BF16)],
        compiler_params=_cparams(("parallel", "arbitrary")),
        name="prenorm_mm",
    )(*args)
    return res if has_small else res[0]


def _memattn_body(nb, o_ref, wout_ref, g1_ref, x_ref, g2_ref, wq_ref, mk_ref, mv_ref, wo_ref, g3_ref, y_ref):
    tq, D = x_ref.shape[1], x_ref.shape[2]
    R = nb * tq
    o_in = o_ref[...].reshape(R, o_ref.shape[2])
    x = x_ref[...].reshape(R, D)
    x = x + _rms(jnp.dot(o_in, wout_ref[...], preferred_element_type=F32), g1_ref[...])
    h = _rms(x, g2_ref[...]).astype(BF16)
    q = jnp.dot(h, wq_ref[...], preferred_element_type=F32).astype(BF16)
    rows = []
    for bi in range(nb):
        mk = mk_ref[bi].astype(BF16)
        mv = mv_ref[bi].astype(BF16)
        qb = q[bi * tq:(bi + 1) * tq]
        outs = []
        for hd in range(X_HEADS):
            sl = slice(hd * X_DH, (hd + 1) * X_DH)
            s = _dot_nt(qb[:, sl], mk[:, sl]) * (X_DH ** -0.5)
            p = jnp.exp(s - jnp.max(s, axis=-1, keepdims=True))
            p = p * (1.0 / jnp.sum(p, axis=-1, keepdims=True))
            outs.append(jnp.dot(p.astype(BF16), mv[:, sl], preferred_element_type=F32))
        rows.append(jnp.concatenate(outs, axis=-1).astype(BF16))
    o = rows[0] if nb == 1 else jnp.concatenate(rows, axis=0)
    y = jnp.dot(o, wo_ref[...], preferred_element_type=F32)
    y_ref[...] = (x + _rms(y, g3_ref[...])).reshape(nb, tq, D)


def _memattn(o, w_out, lo, g1, x, g2, wq, mem_k, mem_v, wo, lx, g3, tq=1024, rows_per_step=256):
    B, T, D = x.shape
    tq = min(tq, T)
    assert T % tq == 0
    nb = max(1, min(B, rows_per_step // tq)) if tq == T else 1
    assert B % nb == 0
    HD = X_HEADS * X_DH
    K = o.shape[-1]
    (mk_arr, koff, kcol), (mv_arr, voff, vcol) = mem_k, mem_v
    assert koff % nb == 0 and voff % nb == 0
    return pl.pallas_call(
        functools.partial(_memattn_body, nb),
        grid=(B // nb, T // tq),
        in_specs=[pl.BlockSpec((nb, tq, K), lambda b, i: (b, i, 0)),
                  _wspec(lo, (K, D), lambda b, i: (0, 0)),
                  pl.BlockSpec((1, D), lambda b, i: (0, 0)),
                  pl.BlockSpec((nb, tq, D), lambda b, i: (b, i, 0)),
                  pl.BlockSpec((1, D), lambda b, i: (0, 0)),
                  _wspec(lx, (D, HD), lambda b, i: (0, 0)),
                  pl.BlockSpec((nb, N_MEM, HD), lambda b, i: (koff // nb + b, 0, kcol)),
                  pl.BlockSpec((nb, N_MEM, HD), lambda b, i: (voff // nb + b, 0, vcol)),
                  _wspec(lx, (HD, D), lambda b, i: (0, 0)),
                  pl.BlockSpec((1, D), lambda b, i: (0, 0))],
        out_specs=pl.BlockSpec((nb, tq, D), lambda b, i: (b, i, 0)),
        out_shape=jax.ShapeDtypeStruct((B, T, D), F32),
        compiler_params=_cparams(("parallel", "parallel")),
        name="memattn",
    )(o, w_out, g1.reshape(1, D), x, g2.reshape(1, D), wq, mk_arr, mv_arr, wo, g3.reshape(1, D))


def _ffn_body(nf, x_ref, g4_ref, wg_ref, wu_ref, wd_ref, g5_ref, y_ref, xn_ref, acc_ref):
    f = pl.program_id(1)

    @pl.when(f == 0)
    def _():
        xn_ref[...] = _rms(x_ref[...], g4_ref[...]).astype(BF16)
        acc_ref[...] = jnp.zeros_like(acc_ref)

    xn = xn_ref[...]
    g = jnp.dot(xn, wg_ref[...], preferred_element_type=F32)
    u = jnp.dot(xn, wu_ref[...], preferred_element_type=F32)
    a = (g * _sigmoid(g)) * u
    acc_ref[...] += jnp.dot(a.astype(BF16), wd_ref[...], preferred_element_type=F32)

    @pl.when(f == nf - 1)
    def _():
        y_ref[...] = x_ref[...] + _rms(acc_ref[...], g5_ref[...])


def _ffn(x, g4, w_gu, w_down, li, g5, tm=1024, tf=1408):
    M, D = x.shape
    tm = min(tm, M)
    assert M % tm == 0 and D_FF % tf == 0
    nf = D_FF // tf
    return pl.pallas_call(
        functools.partial(_ffn_body, nf),
        grid=(M // tm, nf),
        in_specs=[pl.BlockSpec((tm, D), lambda i, f: (i, 0)),
                  pl.BlockSpec((1, D), lambda i, f: (0, 0)),
                  _wspec(li, (D, tf), lambda i, f: (0, f)),
                  _wspec(li, (D, tf), lambda i, f: (0, nf + f)),
                  _wspec(li, (tf, D), lambda i, f: (f, 0)),
                  pl.BlockSpec((1, D), lambda i, f: (0, 0))],
        out_specs=pl.BlockSpec((tm, D), lambda i, f: (i, 0)),
        out_shape=jax.ShapeDtypeStruct((M, D), F32),
        scratch_shapes=[pltpu.VMEM((tm, D), BF16), pltpu.VMEM((tm, D), F32)],
        compiler_params=_cparams(("parallel", "arbitrary")),
        name="ffn",
    )(x, g4.reshape(1, D), w_gu, w_gu, w_down, g5.reshape(1, D))


GDN_CHUNK = 64
GDN_BASE = 8


def _inverse_masks(L):
    row = lax.broadcasted_iota(jnp.int32, (L, L), 0)
    col = lax.broadcasted_iota(jnp.int32, (L, L), 1)
    base = row // GDN_BASE == col // GDN_BASE
    offs = []
    s = GDN_BASE
    while s < L:
        same_pair = row // (2 * s) == col // (2 * s)
        offs.append(jnp.logical_and(same_pair, (row // s) - (col // s) == 1))
        s *= 2
    return base, row == col, offs


def _unit_lower_inverse(As, masks, mm):
    base, diag, offs = masks
    eye = jnp.where(diag, 1.0, 0.0)
    Ps = [jnp.where(base, -A, 0.0) for A in As]
    Ts = [eye + P for P in Ps]
    k = 1
    while k < GDN_BASE // 2:
        Ps = [mm(P, P) for P in Ps]
        Ts = [T + mm(T, P) for T, P in zip(Ts, Ps)]
        k *= 2
    for off in offs:
        Ms = [_dot(T, jnp.where(off, A, 0.0)) for T, A in zip(Ts, As)]
        Ts = [T - _dot(M, T) for T, M in zip(Ts, Ms)]
    return Ts


def _gdn_prep_body(L, nch, qkv_ref, prev_ref, small_ref, buf_ref, wc_ref, par_ref,
                   wk_ref, uv_ref, qt_ref, kt_ref, at_ref, egl_ref, xbuf):
    c = pl.program_id(1)
    LA = L * nch

    @pl.when(c == 0)
    def _():
        xbuf[0:8, :] = jnp.zeros((8, GDN_QKV), F32)
        xbuf[5:8, :] = buf_ref[...]

    @pl.when(c > 0)
    def _():
        xbuf[0:8, :] = prev_ref[8:16, :].astype(F32)

    xbuf[8:8 + LA, :] = qkv_ref[...].astype(F32)
    xb = xbuf[...]
    z = xb * wc_ref[0:1, :]
    for j in range(1, CONV_W):
        z = pltpu.roll(z, 1, axis=0) + xb * wc_ref[j:j + 1, :]
    y = z[8:, :]
    y = y * _sigmoid(y)

    small = small_ref[...]
    beta_all = _sigmoid(small)
    g_all = -jnp.exp(par_ref[0:1, :]) * jax.nn.softplus(small + par_ref[1:2, :])
    incl, strict, _ = _tri_masks(L)
    tril_b = jnp.where(incl, 1.0, 0.0).astype(BF16)
    masks = _inverse_masks(L)
    H = GDN_HEADS

    units = []
    for ci in range(nch):
        rows = slice(ci * L, (ci + 1) * L)
        G_all = _cumsum_rows(tril_b, g_all[rows])
        GT = G_all.T
        egl_ref[ci] = jnp.broadcast_to(jnp.exp(G_all[L - 1:L, :]), (8, LANES))
        for h in range(H):
            qh = y[rows, h * GDN_DK:(h + 1) * GDN_DK]
            kh = y[rows, (H + h) * GDN_DK:(H + h + 1) * GDN_DK]
            vh = y[rows, 2 * H * GDN_DK + h * GDN_DV:2 * H * GDN_DK + (h + 1) * GDN_DV]
            qh = qh * lax.rsqrt(jnp.sum(qh * qh, axis=-1, keepdims=True) + EPS) * (GDN_DK ** -0.5)
            kh = kh * lax.rsqrt(jnp.sum(kh * kh, axis=-1, keepdims=True) + EPS)
            bc = beta_all[rows, h:h + 1]
            Gc = G_all[:, H + h:H + h + 1]
            Gr = GT[H + h:H + h + 1, :]
            eG = jnp.exp(Gc)
            dec = jnp.exp(jnp.where(incl, Gc - Gr, NEG_BIG))
            qkk = _dot_nt(jnp.concatenate([qh, kh], axis=0), kh)
            A = jnp.where(strict, bc * qkk[L:] * dec, 0.0)
            rhs = jnp.concatenate([bc * vh, (bc * eG) * kh], axis=-1)
            cols = slice(h * GDN_DV, (h + 1) * GDN_DV)
            qt_ref[rows, cols] = (eG * qh).astype(BF16)
            kt_ref[rows, cols] = (kh * jnp.exp(Gc[L - 1:L, :] - Gc)).astype(BF16)
            at_ref[rows, cols] = jnp.concatenate(
                [qkk[:L] * dec, jnp.zeros((L, GDN_DV - L), F32)], axis=-1).astype(BF16)
            units.append((rows, cols, A, rhs))

    Ts = _unit_lower_inverse([u[2] for u in units], masks, _dot)
    for (rows, cols, _, rhs), T in zip(units, Ts):
        sol = _dot(T, rhs)
        uv_ref[rows, cols] = sol[:, :GDN_DV]
        wk_ref[rows, cols] = sol[:, GDN_DV:].astype(BF16)


def _gdn_scan_body(L, cb, NS, wk_ref, uv_ref, qt_ref, kt_ref, at_ref, egl_ref, gate_ref, S0_ref, on_ref,
                   o_ref, Sout_ref, S_scr):
    c = pl.program_id(1)

    @pl.when(c == 0)
    def _():
        S_scr[...] = S0_ref[...]

    onorm = on_ref[...]
    H = GDN_HEADS
    for ci in range(cb):
        rows = slice(ci * L, (ci + 1) * L)
        egl = egl_ref[ci]
        hs = [slice(h * GDN_DV, (h + 1) * GDN_DV) for h in range(H)]
        Ss = [S_scr[h] for h in range(H)]
        Xs = [jnp.dot(jnp.concatenate([wk_ref[rows, hs[h]], qt_ref[rows, hs[h]]], axis=0),
                      Ss[h].astype(BF16), preferred_element_type=F32) for h in range(H)]
        Us = [(uv_ref[rows, hs[h]] - Xs[h][:L]).astype(BF16) for h in range(H)]
        os = [Xs[h][L:] + jnp.dot(at_ref[rows, h * GDN_DV:h * GDN_DV + L], Us[h],
                                  preferred_element_type=F32) for h in range(H)]
        for h in range(H):
            S_scr[h] = egl[0:1, H + h:H + h + 1] * Ss[h] + lax.dot_general(
                kt_ref[rows, hs[h]], Us[h], (((0,), (0,)), ((), ())), preferred_element_type=F32)
        for h in range(H):
            gh = gate_ref[rows, hs[h]].astype(F32)
            o_ref[rows, hs[h]] = (_rms(os[h], onorm) * (gh * _sigmoid(gh))).astype(BF16)

    @pl.when(c == NS - 1)
    def _():
        Sout_ref[...] = S_scr[...]


def _gdn_mixer(main, small, conv_buf, S0, wc_t, par, onorm, nch=4, cb=8):
    B, T, _ = main.shape
    L = min(GDN_CHUNK, T)
    assert T % L == 0
    NC = T // L
    nch = min(nch, NC)
    cb = min(cb, NC)
    assert NC % nch == 0 and NC % cb == 0
    LA = L * nch
    tok = lambda n, j: pl.BlockSpec((None, n, 1024), lambda b, c: (b, c, j))
    sds = lambda dt: jax.ShapeDtypeStruct((B, T, 1024), dt)
    wk, uv, qt, kt, at, egl = pl.pallas_call(
        functools.partial(_gdn_prep_body, L, nch),
        grid=(B, NC // nch),
        in_specs=[pl.BlockSpec((None, LA, GDN_QKV), lambda b, c: (b, c, 0)),
                  pl.BlockSpec((None, 16, GDN_QKV), lambda b, c: (b, jnp.maximum(c * (LA // 16) - 1, 0), 0)),
                  pl.BlockSpec((None, LA, LANES), lambda b, c: (b, c, 0)),
                  pl.BlockSpec((None, CONV_W - 1, GDN_QKV), lambda b, c: (b, 0, 0)),
                  pl.BlockSpec((CONV_W, GDN_QKV), lambda b, c: (0, 0)),
                  pl.BlockSpec((2, LANES), lambda b, c: (0, 0))],
        out_specs=[tok(LA, 0), tok(LA, 0), tok(LA, 0), tok(LA, 0), tok(LA, 0),
                   pl.BlockSpec((None, nch, 8, LANES), lambda b, c: (b, c, 0, 0))],
        out_shape=[sds(BF16), sds(F32), sds(BF16), sds(BF16), sds(BF16),
                   jax.ShapeDtypeStruct((B, NC, 8, LANES), F32)],
        scratch_shapes=[pltpu.VMEM((LA + 8, GDN_QKV), F32)],
        compiler_params=_cparams(("parallel", "parallel")),
        name="gdn_prep",
    )(main, main, small, conv_buf, wc_t, par)
    NS = NC // cb
    state = pl.BlockSpec((None, GDN_HEADS, GDN_DK, GDN_DV), lambda b, c: (b, 0, 0, 0))
    o, S = pl.pallas_call(
        functools.partial(_gdn_scan_body, L, cb, NS),
        grid=(B, NS),
        in_specs=[tok(cb * L, 0), tok(cb * L, 0), tok(cb * L, 0), tok(cb * L, 0), tok(cb * L, 0),
                  pl.BlockSpec((None, cb, 8, LANES), lambda b, c: (b, c, 0, 0)),
                  tok(cb * L, 3), state,
                  pl.BlockSpec((1, GDN_DV), lambda b, c: (0, 0))],
        out_specs=[tok(cb * L, 0), state],
        out_shape=[sds(BF16), jax.ShapeDtypeStruct((B, GDN_HEADS, GDN_DK, GDN_DV), F32)],
        scratch_shapes=[pltpu.VMEM((GDN_HEADS, GDN_DK, GDN_DV), F32)],
        compiler_params=_cparams(("parallel", "arbitrary")),
        name="gdn_scan",
    )(wk, uv, qt, kt, at, egl, main, S0, onorm.reshape(1, GDN_DV))
    new_buf = main[:, T - (CONV_W - 1):, :GDN_QKV].astype(F32)
    return o, new_buf, S


ML_CHUNK = 64


def _mlstm_gates(small_ref, bias_ref):
    sc = GATE_CAP * jnp.tanh((small_ref[...] + bias_ref[...]) / GATE_CAP)
    return sc, jax.nn.log_sigmoid(sc)


def _mlstm_prep_body(L, nch, qk_ref, v_ref, small_ref, bias_ref, hl_ref, st_ref, kv_ref, ks_ref):
    ig_all, lf_all = _mlstm_gates(small_ref, bias_ref)
    incl, _, _ = _tri_masks(L)
    tril_b = jnp.where(incl, 1.0, 0.0).astype(BF16)
    lane = lax.broadcasted_iota(jnp.int32, (1, LANES), 1)
    H = ML_HEADS
    units = [(ci, h) for ci in range(nch) for h in range(H)]
    rows_of = lambda ci: slice(ci * L, (ci + 1) * L)
    q_of = lambda ci, h: qk_ref[rows_of(ci), h * ML_DQK:(h + 1) * ML_DQK]
    k_of = lambda ci, h: qk_ref[rows_of(ci), (H + h) * ML_DQK:(H + h + 1) * ML_DQK].astype(F32) * (ML_DQK ** -0.5)
    v_of = lambda ci, h: v_ref[rows_of(ci), h * ML_DV:(h + 1) * ML_DV]

    gates = []
    for ci in range(nch):
        b_all = _cumsum_rows(tril_b, lf_all[rows_of(ci)])
        igc = ig_all[rows_of(ci)]
        gates.append((b_all, igc, b_all.T, igc.T))
    qk = [_dot_nt(q_of(ci, h), k_of(ci, h)) for ci, h in units]
    Wl, dmaxs = [], []
    for (ci, h), s in zip(units, qk):
        b_all, igc, bT, igT = gates[ci]
        D = jnp.where(incl, b_all[:, H + h:H + h + 1] + (igT[h:h + 1, :] - bT[H + h:H + h + 1, :]), NEG_BIG)
        dmax = jnp.max(D, axis=-1, keepdims=True)
        dmaxs.append(dmax)
        Wl.append(jnp.exp(D - dmax) * s)
    for (ci, h), w in zip(units, Wl):
        hl_ref[rows_of(ci), h * ML_DV:(h + 1) * ML_DV] = _dot(w, v_of(ci, h))
    wks = []
    for (ci, h), dmax in zip(units, dmaxs):
        b_all, igc, _, _ = gates[ci]
        b_c = b_all[:, H + h:H + h + 1]
        wks.append(jnp.exp(b_c[L - 1:L, :] - b_c + igc[:, h:h + 1] - dmax[L - 1:L, :]) * k_of(ci, h))
    for (ci, h), wk in zip(units, wks):
        kv_ref[ci, h] = _dot_tn(wk, v_of(ci, h))
        ks_ref[ci, h:h + 1, :] = jnp.sum(wk, axis=0, keepdims=True)
    for ci in range(nch):
        stats = jnp.zeros((L, LANES), F32)
        for h in range(H):
            u = ci * H + h
            stats = jnp.where(lane == h, gates[ci][0][:, H + h:H + h + 1],
                              jnp.where(lane == H + h, dmaxs[u],
                                        jnp.where(lane == 2 * H + h, jnp.sum(Wl[u], axis=-1, keepdims=True), stats)))
        st_ref[rows_of(ci), :] = stats


def _mlstm_scan_body(L, cb, NS, q_ref, hl_ref, st_ref, kv_ref, ks_ref, op_ref, C0_ref, n0_ref, m0_ref, on_ref,
                     o_ref, Cout_ref, nout_ref, mout_ref, C_scr, n_scr, m_scr):
    c = pl.program_id(1)

    @pl.when(c == 0)
    def _():
        C_scr[...] = C0_ref[...]
        n_scr[...] = n0_ref[...]
        m_scr[...] = m0_ref[...]

    onorm = on_ref[...]
    H = ML_HEADS
    units = [(ci, h) for ci in range(cb) for h in range(H)]
    rows_of = lambda ci: slice(ci * L, (ci + 1) * L)
    sts = [st_ref[rows_of(ci), :] for ci in range(cb)]
    pre = {}
    for h in range(H):
        m_prev = m_scr[h:h + 1, 0:1]
        n_prev = n_scr[h:h + 1, :]
        C = C_scr[h]
        for ci in range(cb):
            b_c = sts[ci][:, h:h + 1]
            dmax = sts[ci][:, H + h:H + h + 1]
            mt = jnp.maximum(b_c + m_prev, dmax)
            pre[(ci, h)] = (C, n_prev, m_prev, mt)
            mL = mt[L - 1:L, :]
            fL = jnp.exp(b_c[L - 1:L, :] + m_prev - mL)
            gL = jnp.exp(dmax[L - 1:L, :] - mL)
            C = fL * C + gL * kv_ref[ci, h]
            n_prev = fL * n_prev + gL * ks_ref[ci, h:h + 1, :]
            m_prev = mL
        C_scr[h] = C
        n_scr[h:h + 1, :] = n_prev
        m_scr[h:h + 1, :] = jnp.broadcast_to(m_prev, (1, LANES))

    qs = {u: q_ref[rows_of(u[0]), u[1] * ML_DQK:(u[1] + 1) * ML_DQK].astype(F32) for u in units}
    qC = {u: _dot(qs[u], pre[u][0]) for u in units}
    hs = {}
    for u in units:
        ci, h = u
        _, n_prev, m_prev, mt = pre[u]
        b_c = sts[ci][:, h:h + 1]
        dmax = sts[ci][:, H + h:H + h + 1]
        sw = sts[ci][:, 2 * H + h:2 * H + h + 1]
        e_i = jnp.exp(b_c + m_prev - mt)
        e_l = jnp.exp(dmax - mt)
        num = e_i * qC[u] + e_l * hl_ref[rows_of(ci), h * ML_DV:(h + 1) * ML_DV]
        den = e_i * jnp.sum(qs[u] * n_prev, axis=-1, keepdims=True) + e_l * sw
        hs[u] = num / jnp.maximum(jnp.abs(den), jnp.exp(-mt))
    for u in units:
        ci, h = u
        og = op_ref[rows_of(ci), h * ML_DV:(h + 1) * ML_DV].astype(F32)
        o_ref[rows_of(ci), h * ML_DV:(h + 1) * ML_DV] = (_sigmoid(og) * _rms(hs[u], onorm)).astype(BF16)

    @pl.when(c == NS - 1)
    def _():
        Cout_ref[...] = C_scr[...]
        nout_ref[...] = n_scr[...]
        mout_ref[...] = m_scr[...]


def _mlstm_mixer(main, small, C0, n0, m0, bias, onorm, nch=4, cb=4):
    B, T, _ = main.shape
    L = min(ML_CHUNK, T)
    assert T % L == 0
    NC = T // L
    nch = min(nch, NC)
    cb = min(cb, NC)
    assert NC % nch == 0 and NC % cb == 0
    LA = L * nch
    H = ML_HEADS
    hl, st, kv, ks = pl.pallas_call(
        functools.partial(_mlstm_prep_body, L, nch),
        grid=(B, NC // nch),
        in_specs=[pl.BlockSpec((None, LA, 1024), lambda b, c: (b, c, 0)),
                  pl.BlockSpec((None, LA, 1024), lambda b, c: (b, c, 1)),
                  pl.BlockSpec((None, LA, LANES), lambda b, c: (b, c, 0)),
                  pl.BlockSpec((1, LANES), lambda b, c: (0, 0))],
        out_specs=[pl.BlockSpec((None, LA, 1024), lambda b, c: (b, c, 0)),
                   pl.BlockSpec((None, LA, LANES), lambda b, c: (b, c, 0)),
                   pl.BlockSpec((None, nch, H, ML_DQK, ML_DV), lambda b, c: (b, c, 0, 0, 0)),
                   pl.BlockSpec((None, nch, H, ML_DQK), lambda b, c: (b, c, 0, 0))],
        out_shape=[jax.ShapeDtypeStruct((B, T, 1024), F32),
                   jax.ShapeDtypeStruct((B, T, LANES), F32),
                   jax.ShapeDtypeStruct((B, NC, H, ML_DQK, ML_DV), F32),
                   jax.ShapeDtypeStruct((B, NC, H, ML_DQK), F32)],
        compiler_params=_cparams(("parallel", "parallel")),
        name="mlstm_prep",
    )(main, main, small, bias)
    NS = NC // cb
    m0b = jnp.broadcast_to(m0[:, :, None], (B, H, LANES))
    st_C = pl.BlockSpec((None, H, ML_DQK, ML_DV), lambda b, c: (b, 0, 0, 0))
    st_n = pl.BlockSpec((None, H, ML_DQK), lambda b, c: (b, 0, 0))
    st_m = pl.BlockSpec((None, H, LANES), lambda b, c: (b, 0, 0))
    return pl.pallas_call(
        functools.partial(_mlstm_scan_body, L, cb, NS),
        grid=(B, NS),
        in_specs=[pl.BlockSpec((None, cb * L, H * ML_DQK), lambda b, c: (b, c, 0)),
                  pl.BlockSpec((None, cb * L, 1024), lambda b, c: (b, c, 0)),
                  pl.BlockSpec((None, cb * L, LANES), lambda b, c: (b, c, 0)),
                  pl.BlockSpec((None, cb, H, ML_DQK, ML_DV), lambda b, c: (b, c, 0, 0, 0)),
                  pl.BlockSpec((None, cb, H, ML_DQK), lambda b, c: (b, c, 0, 0)),
                  pl.BlockSpec((None, cb * L, 1024), lambda b, c: (b, c, 2)),
                  st_C, st_n, st_m,
                  pl.BlockSpec((1, ML_DV), lambda b, c: (0, 0))],
        out_specs=[pl.BlockSpec((None, cb * L, 1024), lambda b, c: (b, c, 0)), st_C, st_n, st_m],
        out_shape=[jax.ShapeDtypeStruct((B, T, 1024), BF16),
                   jax.ShapeDtypeStruct((B, H, ML_DQK, ML_DV), F32),
                   jax.ShapeDtypeStruct((B, H, ML_DQK), F32),
                   jax.ShapeDtypeStruct((B, H, LANES), F32)],
        scratch_shapes=[pltpu.VMEM((H, ML_DQK, ML_DV), F32),
                        pltpu.VMEM((H, ML_DQK), F32),
                        pltpu.VMEM((H, LANES), F32)],
        compiler_params=_cparams(("parallel", "arbitrary")),
        name="mlstm_scan",
    )(main, hl, st, kv, ks, main, C0, n0, m0b, onorm.reshape(1, ML_DV))


FOX_BIAS_LANE = FOX_DH


def _fox_prep_body(q_ref, k_ref, v_ref, small_ref, bf_ref, qn_ref, kn_ref,
                   qh_ref, k32_ref, v32_ref, lf_ref):
    r = lax.broadcasted_iota(jnp.int32, (LANES, LANES), 0) // FOX_DH
    cc = lax.broadcasted_iota(jnp.int32, (LANES, LANES), 1) // FOX_DH
    avg = jnp.where(r == cc, 1.0 / FOX_DH, 0.0).astype(BF16)

    def head_rms(x, gain):
        parts = []
        for t in range(x.shape[1] // LANES):
            xt = x[:, t * LANES:(t + 1) * LANES]
            hi, lo = _split2(xt * xt)
            ms = (jnp.dot(hi, avg, preferred_element_type=F32)
                  + jnp.dot(lo, avg, preferred_element_type=F32))
            parts.append(xt * lax.rsqrt(ms + EPS))
        return jnp.concatenate(parts, axis=-1) * gain

    qn = head_rms(q_ref[...].astype(F32), qn_ref[...]) * (LOG2E * FOX_DH ** -0.5)
    lane = lax.broadcasted_iota(jnp.int32, (1, LANES), 1)
    ones3 = jnp.where(lane < FOX_BIAS_LANE + 3, 1.0, 0.0)
    for h in range(FOX_HEADS):
        xt = qn[:, (h // 2) * LANES:(h // 2 + 1) * LANES]
        if h % 2:
            xt = pltpu.roll(xt, FOX_DH, axis=1)
        qh_ref[h] = jnp.where(lane < FOX_DH, xt, ones3).astype(BF16)
    k32_ref[...] = head_rms(k_ref[...].astype(F32), kn_ref[...])
    v32_ref[...] = v_ref[...].astype(F32)
    lf_ref[...] = jax.nn.log_sigmoid(small_ref[...] + bf_ref[...])


def _fox_prep(main, small, bf_row, qn_row, kn_row, tm=512):
    B, T, _ = main.shape
    tm = min(tm, T)
    assert T % tm == 0
    blk = lambda j: pl.BlockSpec((None, tm, 1024), lambda b, i: (b, i, j))
    row = lambda n: pl.BlockSpec((1, n), lambda b, i: (0, 0))
    o1024 = pl.BlockSpec((None, tm, 1024), lambda b, i: (b, i, 0))
    return pl.pallas_call(
        _fox_prep_body,
        grid=(B, T // tm),
        in_specs=[blk(0), blk(1), blk(2),
                  pl.BlockSpec((None, tm, LANES), lambda b, i: (b, i, 0)),
                  row(LANES), row(1024), row(1024)],
        out_specs=[pl.BlockSpec((None, FOX_HEADS, tm, LANES), lambda b, i: (b, 0, i, 0)),
                   o1024, o1024,
                   pl.BlockSpec((None, tm, LANES), lambda b, i: (b, i, 0))],
        out_shape=[jax.ShapeDtypeStruct((B, FOX_HEADS, T, LANES), BF16),
                   jax.ShapeDtypeStruct((B, T, 1024), F32),
                   jax.ShapeDtypeStruct((B, T, 1024), F32),
                   jax.ShapeDtypeStruct((B, T, LANES), F32)],
        compiler_params=_cparams(("parallel", "parallel")),
        name="fox_prep",
    )(main, main, main, small, bf_row, qn_row, kn_row)


def _fox_kv_body(tm, n_past, *refs):
    if n_past:
        kp_ref, vp_ref, lp_ref, k_ref, v_ref, lf_ref, wk_ref, wv_ref, kh_ref, vh_ref, carry = refs
    else:
        k_ref, v_ref, lf_ref, wk_ref, wv_ref, kh_ref, vh_ref, carry = refs
    i = pl.program_id(1)

    @pl.when(i == 0)
    def _():
        carry[...] = jnp.zeros_like(carry)

    k = k_ref[...]
    v = v_ref[...]
    lf = lf_ref[...]
    if n_past:
        past = i < n_past
        k = jnp.where(past, kp_ref[...], k)
        v = jnp.where(past, vp_ref[...], v)
        lf = jnp.where(past, lp_ref[...], lf)
    incl, _, _ = _tri_masks(tm)
    tril_b = jnp.where(incl, 1.0, 0.0).astype(BF16)
    F = _cumsum_rows(tril_b, lf) + carry[0:1, :]
    carry[0:1, :] = F[tm - 1:tm, :]
    nb = -LOG2E * F
    b1 = nb.astype(BF16).astype(F32)
    r1 = nb - b1
    b2 = r1.astype(BF16).astype(F32)
    b3 = r1 - b2
    lane = lax.broadcasted_iota(jnp.int32, (1, LANES), 1)
    ones1 = jnp.where(lane == FOX_BIAS_LANE, 1.0, 0.0)
    parts = jnp.where(lane < FOX_HEADS, b1,
                      jnp.where(lane < 2 * FOX_HEADS, pltpu.roll(b2, FOX_HEADS, axis=1),
                                jnp.where(lane < 3 * FOX_HEADS, pltpu.roll(b3, 2 * FOX_HEADS, axis=1), 0.0)))
    parts_b = parts.astype(BF16)
    kb = k.astype(BF16)
    vb = v.astype(BF16)
    for h in range(FOX_HEADS):
        tile = slice((h // 2) * LANES, (h // 2 + 1) * LANES)
        kh_ref[h] = jnp.dot(jnp.concatenate([kb[:, tile], parts_b], axis=-1), wk_ref[h],
                            preferred_element_type=F32).astype(BF16)
        vh_ref[h] = (jnp.dot(vb[:, tile], wv_ref[h % 2], preferred_element_type=F32) + ones1).astype(BF16)


def _fox_kv(k_new, v_new, lf_new, past=None, tm=512):
    B, T, _ = k_new.shape
    if past is None:
        tm = min(tm, T)
        assert T % tm == 0
        n_past, n_new = 0, T // tm
        args, specs = [], []
    else:
        k_past, v_past, lf_past = past
        P = k_past.shape[1]
        tm = min(tm, P)
        assert P % tm == 0 and T <= tm
        n_past, n_new = P // tm, 1
        pad = lambda a: jnp.pad(a, ((0, 0), (0, tm - T), (0, 0)))
        k_new, v_new, lf_new = pad(k_new), pad(v_new), pad(lf_new)
        clamp = lambda b, i: (b, jnp.minimum(i, n_past - 1), 0)
        args = [k_past, v_past, lf_past]
        specs = [pl.BlockSpec((None, tm, 1024), clamp), pl.BlockSpec((None, tm, 1024), clamp),
                 pl.BlockSpec((None, tm, LANES), clamp)]
    new_idx = lambda b, i: (b, jnp.maximum(i - n_past, 0), 0)
    N = (n_past + n_new) * tm
    hm = pl.BlockSpec((None, FOX_HEADS, tm, LANES), lambda b, i: (b, 0, i, 0))
    wk = np.zeros((FOX_HEADS, 2 * LANES, LANES), np.float32)
    wv = np.zeros((2, LANES, LANES), np.float32)
    for h in range(FOX_HEADS):
        for d in range(FOX_DH):
            wk[h, FOX_DH * (h % 2) + d, d] = 1.0
            wv[h % 2, FOX_DH * (h % 2) + d, d] = 1.0
        for g in range(3):
            wk[h, LANES + g * FOX_HEADS + h, FOX_BIAS_LANE + g] = 1.0
    wk, wv = jnp.asarray(wk, BF16), jnp.asarray(wv, BF16)
    return pl.pallas_call(
        functools.partial(_fox_kv_body, tm, n_past),
        grid=(B, n_past + n_new),
        in_specs=specs + [pl.BlockSpec((None, tm, 1024), new_idx),
                          pl.BlockSpec((None, tm, 1024), new_idx),
                          pl.BlockSpec((None, tm, LANES), new_idx),
                          pl.BlockSpec((FOX_HEADS, 2 * LANES, LANES), lambda b, i: (0, 0, 0)),
                          pl.BlockSpec((2, LANES, LANES), lambda b, i: (0, 0, 0))],
        out_specs=[hm, hm],
        out_shape=[jax.ShapeDtypeStruct((B, FOX_HEADS, N, LANES), BF16),
                   jax.ShapeDtypeStruct((B, FOX_HEADS, N, LANES), BF16)],
        scratch_shapes=[pltpu.VMEM((8, LANES), F32)],
        compiler_params=_cparams(("parallel", "arbitrary")),
        name="fox_kv",
    )(*args, k_new, v_new, lf_new, wk, wv)


def _flash_body(P, tq, tk, rq, hp, qi_tab, kj_tab, q_ref, k_ref, v_ref, gate_ref, o_ref, m_scr, acc_scr):
    s_idx = pl.program_id(2)
    qi = qi_tab[s_idx]
    kj = kj_tab[s_idx]
    last_kj = (P + (qi + 1) * tq - 1) // tk
    aligned = tq == tk and P % tk == 0

    @pl.when(kj == 0)
    def _():
        m_scr[...] = jnp.full(m_scr.shape, NEG_BIG, F32)
        acc_scr[...] = jnp.zeros_like(acc_scr)

    def step(masked):
        units = [(hh, r) for r in range(tq // rq) for hh in range(2 * hp)]

        def logits(u):
            hh, r = u
            rows = slice(r * rq, (r + 1) * rq)
            nk = min(tk, (r + 1) * rq) if (masked and aligned) else tk
            s = lax.dot_general(q_ref[hh, rows, :], k_ref[hh, 0:nk, :], (((1,), (1,)), ((), ())),
                                preferred_element_type=F32)
            if masked:
                qpos = P + qi * tq + r * rq + lax.broadcasted_iota(jnp.int32, (rq, nk), 0)
                kpos = kj * tk + lax.broadcasted_iota(jnp.int32, (rq, nk), 1)
                s = jnp.where(kpos <= qpos, s, NEG_BIG)
            return s, nk

        def probs(u, s):
            hh, r = u
            rows = slice(r * rq, (r + 1) * rq)
            m_prev = m_scr[hh, rows, :][:, 0:1]
            m_new = jnp.maximum(m_prev, jnp.max(s, axis=-1, keepdims=True))
            m_scr[hh, rows, :] = jnp.broadcast_to(m_new, (rq, LANES))
            return jnp.exp2(m_prev - m_new), jnp.exp2(s - m_new).astype(BF16)

        def accumulate(u, alpha, p, nk):
            hh, r = u
            rows = slice(r * rq, (r + 1) * rq)
            acc_scr[hh, rows, :] = alpha * acc_scr[hh, rows, :] + jnp.dot(
                p, v_ref[hh, 0:nk, :], preferred_element_type=F32)

        n = len(units)
        pend_s = {0: logits(units[0])}
        pend_p = {}
        for i in range(n):
            if i + 1 < n:
                pend_s[i + 1] = logits(units[i + 1])
            s, nk = pend_s.pop(i)
            pend_p[i] = probs(units[i], s) + (nk,)
            if i >= 1:
                accumulate(units[i - 1], *pend_p.pop(i - 1))
        accumulate(units[n - 1], *pend_p.pop(n - 1))

    needs_mask = (kj + 1) * tk - 1 > P + qi * tq

    @pl.when(needs_mask)
    def _():
        step(True)

    @pl.when(jnp.logical_not(needs_mask))
    def _():
        step(False)

    @pl.when(kj == last_kj)
    def _():
        lane = lax.broadcasted_iota(jnp.int32, (1, LANES), 1)
        for pp in range(hp):
            a0 = acc_scr[2 * pp]
            a1 = acc_scr[2 * pp + 1]
            o0 = a0 * (1.0 / a0[:, FOX_BIAS_LANE:FOX_BIAS_LANE + 1])
            o1 = a1 * (1.0 / a1[:, FOX_BIAS_LANE:FOX_BIAS_LANE + 1])
            o = jnp.where(lane < FOX_DH, o0, pltpu.roll(o1, FOX_DH, axis=1))
            g = gate_ref[:, pp * LANES:(pp + 1) * LANES].astype(F32)
            o_ref[:, pp * LANES:(pp + 1) * LANES] = (o * _sigmoid(g)).astype(BF16)


def _fox_flash(qh, kh, vh, gate_src, P, tq, tk, rq, hp=1):
    B, _, T, _ = qh.shape
    N = kh.shape[2]
    assert T % tq == 0 and N % tk == 0 and tq % rq == 0 and (FOX_HEADS // 2) % hp == 0
    nq = T // tq
    pairs = [(qi, kj) for qi in range(nq) for kj in range((P + (qi + 1) * tq - 1) // tk + 1)]
    qi_tab = jnp.asarray(np.array([p[0] for p in pairs], np.int32))
    kj_tab = jnp.asarray(np.array([p[1] for p in pairs], np.int32))
    HP = FOX_HEADS // (2 * hp)
    gate_blk0 = 3 * (1024 // (LANES * hp))
    grid_spec = pltpu.PrefetchScalarGridSpec(
        num_scalar_prefetch=2,
        grid=(B, HP, len(pairs)),
        in_specs=[pl.BlockSpec((None, 2 * hp, tq, LANES), lambda b, p, s, qt, kt: (b, p, qt[s], 0)),
                  pl.BlockSpec((None, 2 * hp, tk, LANES), lambda b, p, s, qt, kt: (b, p, kt[s], 0)),
                  pl.BlockSpec((None, 2 * hp, tk, LANES), lambda b, p, s, qt, kt: (b, p, kt[s], 0)),
                  pl.BlockSpec((None, tq, LANES * hp), lambda b, p, s, qt, kt: (b, qt[s], gate_blk0 + p))],
        out_specs=pl.BlockSpec((None, tq, LANES * hp), lambda b, p, s, qt, kt: (b, qt[s], p)),
        scratch_shapes=[pltpu.VMEM((2 * hp, tq, LANES), F32),
                        pltpu.VMEM((2 * hp, tq, LANES), F32)],
    )
    return pl.pallas_call(
        functools.partial(_flash_body, P, tq, tk, rq, hp),
        grid_spec=grid_spec,
        out_shape=jax.ShapeDtypeStruct((B, T, 1024), BF16),
        compiler_params=_cparams(("parallel", "parallel", "arbitrary")),
        name="fox_flash",
    )(qi_tab, kj_tab, qh, kh, vh, gate_src)


def _pad_lanes(a, offset=0):
    n = a.shape[-1]
    pads = [(0, 0)] * (a.ndim - 1) + [(offset, LANES - n - offset)]
    return jnp.pad(a, pads)


def _trunk(x, gdn_conv, gdn_S, ml_C, ml_n, ml_m, fox_k, fox_v, fox_lf, mem_k, mem_v, W):
    B, T, D = x.shape
    M = B * T
    x2 = x.reshape(M, D)
    conv_o, S_o, C_o, n_o, m_o, k_o, v_o, lf_o = [], [], [], [], [], [], [], []
    for layer in range(DEPTH):
        kind, j = layer % N_MIXERS, layer // N_MIXERS
        gains = W["norm_gains"][layer]
        if kind == 0:
            main, small = _prenorm_mm(x2, gains[0], W["gdn_w_in"], j, GDN_MAIN, W["gdn_w_small"], out_dtype=F32, tn=GDN_MAIN // 2)
            o, buf, S = _gdn_mixer(main.reshape(B, T, GDN_MAIN), small.reshape(B, T, LANES),
                                   gdn_conv[j], gdn_S[j], W["gdn_wc_t"][j], W["gdn_par"][j],
                                   W["gdn_onorm"][j])
            conv_o.append(buf)
            S_o.append(S)
            w_out = W["gdn_w_out"]
        elif kind == 1:
            main, small = _prenorm_mm(x2, gains[0], W["ml_w_in"], j, ML_MAIN, W["ml_w_small"], tn=ML_MAIN // 2)
            o, C, n, m = _mlstm_mixer(main.reshape(B, T, ML_MAIN), small.reshape(B, T, LANES),
                                      ml_C[j], ml_n[j], ml_m[j], W["ml_bias"][j], W["ml_onorm"][j])
            C_o.append(C)
            n_o.append(n)
            m_o.append(m[:, :, 0])
            w_out = W["ml_w_out"]
        else:
            main, small = _prenorm_mm(x2, gains[0], W["fox_w_in"], j, FOX_MAIN, W["fox_w_small"], tn=FOX_MAIN // 2)
            main3 = main.reshape(B, T, FOX_MAIN)
            qh, k32, v32, lfp = _fox_prep(main3, small.reshape(B, T, LANES), W["fox_bf"][j],
                                          W["fox_qn"][j], W["fox_kn"][j])
            P = fox_k[j].shape[1]
            if P == 0:
                kh, vh = _fox_kv(k32, v32, lfp)
                tq, tk, rq, hp = min(1024, T), min(1024, T), 256, 1
            else:
                kh, vh = _fox_kv(k32, v32, lfp, past=(fox_k[j].reshape(B, P, 1024), fox_v[j].reshape(B, P, 1024),
                                                      _pad_lanes(fox_lf[j])), tm=256)
                tq, tk, rq, hp = T, kh.shape[2], T, 4
            o = _fox_flash(qh, kh, vh, main3, P, tq, tk, rq, hp)
            k_o.append(k32.reshape(B, T, FOX_HEADS, FOX_DH))
            v_o.append(v32.reshape(B, T, FOX_HEADS, FOX_DH))
            lf_o.append(lfp[:, :, :FOX_HEADS])
            w_out = W["fox_w_out"]
        x2 = _memattn(o, w_out, j, gains[1], x2.reshape(B, T, D), gains[2], W["x_w_q"],
                      mem_k[layer], mem_v[layer], W["x_w_o"], layer, gains[3]).reshape(M, D)
        x2 = _ffn(x2, gains[4], W["ffn_w_gu"], W["ffn_w_down"], layer, gains[5])
    return (x2.reshape(B, T, D), jnp.stack(conv_o), jnp.stack(S_o), jnp.stack(C_o), jnp.stack(n_o),
            jnp.stack(m_o), jnp.stack(k_o), jnp.stack(v_o), jnp.stack(lf_o))


def kernel(x_prompt, x_sample, state_gdn_conv, state_gdn_S, state_mlstm_C, state_mlstm_n, state_mlstm_m, cache_fox_k, cache_fox_v, cache_fox_logf, cache_mem_k, cache_mem_v, mem_prompt, norm_gains, gdn_w_in, gdn_w_conv, gdn_a_log, gdn_dt_bias, gdn_onorm, gdn_w_out, ml_w_in, ml_b_i, ml_b_f, ml_onorm, ml_w_out, fox_w_in, fox_b_f, fox_qk_norm, fox_w_out, x_w_q, x_w_kv, x_w_o, ffn_w_gu, ffn_w_down):
    B = x_prompt.shape[0]
    DB = x_sample.shape[0]
    n_a, n_b, n_c = gdn_w_in.shape[0], ml_w_in.shape[0], fox_w_in.shape[0]
    W = {
        "norm_gains": norm_gains,
        "gdn_w_in": gdn_w_in.astype(BF16),
        "gdn_w_small": _pad_lanes(gdn_w_in[:, :, GDN_MAIN:]).astype(BF16),
        "gdn_wc_t": jnp.transpose(gdn_w_conv, (0, 2, 1)),
        "gdn_par": jnp.stack([_pad_lanes(gdn_a_log, GDN_HEADS), _pad_lanes(gdn_dt_bias, GDN_HEADS)], axis=1),
        "gdn_onorm": gdn_onorm,
        "gdn_w_out": gdn_w_out.astype(BF16),
        "ml_w_in": ml_w_in.astype(BF16),
        "ml_w_small": _pad_lanes(ml_w_in[:, :, ML_MAIN:]).astype(BF16),
        "ml_bias": _pad_lanes(jnp.concatenate([ml_b_i, ml_b_f], axis=-1))[:, None, :],
        "ml_onorm": ml_onorm,
        "ml_w_out": ml_w_out.astype(BF16),
        "fox_w_in": fox_w_in.astype(BF16),
        "fox_w_small": _pad_lanes(fox_w_in[:, :, FOX_MAIN:]).astype(BF16),
        "fox_bf": _pad_lanes(fox_b_f)[:, None, :],
        "fox_qn": jnp.tile(fox_qk_norm[:, 0, :], (1, FOX_HEADS))[:, None, :],
        "fox_kn": jnp.tile(fox_qk_norm[:, 1, :], (1, FOX_HEADS))[:, None, :],
        "fox_w_out": fox_w_out.astype(BF16),
        "x_w_q": x_w_q.astype(BF16),
        "x_w_o": x_w_o.astype(BF16),
        "ffn_w_gu": ffn_w_gu.astype(BF16),
        "ffn_w_down": ffn_w_down.astype(BF16),
    }

    mem2 = mem_prompt.reshape(B * N_MEM, D_MODEL)
    x_w_kv_b = x_w_kv.astype(BF16)
    kv = [_prenorm_mm(mem2, norm_gains[l, 6], x_w_kv_b, l, 2 * D_MODEL, out_dtype=F32).reshape(B, N_MEM, 2 * D_MODEL)
          for l in range(DEPTH)]
    p_mem_k = jnp.stack([a[:, :, :D_MODEL].reshape(B, N_MEM, X_HEADS, X_DH) for a in kv])
    p_mem_v = jnp.stack([a[:, :, D_MODEL:].reshape(B, N_MEM, X_HEADS, X_DH) for a in kv])

    cmk = cache_mem_k.reshape(DEPTH * DB, N_MEM, D_MODEL)
    cmv = cache_mem_v.reshape(DEPTH * DB, N_MEM, D_MODEL)
    zeros = lambda *s: jnp.zeros(s, F32)
    prompt = _trunk(
        x_prompt,
        zeros(n_a, B, CONV_W - 1, GDN_QKV), zeros(n_a, B, GDN_HEADS, GDN_DK, GDN_DV),
        zeros(n_b, B, ML_HEADS, ML_DQK, ML_DV), zeros(n_b, B, ML_HEADS, ML_DQK), zeros(n_b, B, ML_HEADS),
        zeros(n_c, B, 0, FOX_HEADS, FOX_DH), zeros(n_c, B, 0, FOX_HEADS, FOX_DH), zeros(n_c, B, 0, FOX_HEADS),
        [(a, 0, 0) for a in kv], [(a, 0, 1) for a in kv], W)
    sample = _trunk(
        x_sample, state_gdn_conv, state_gdn_S, state_mlstm_C, state_mlstm_n, state_mlstm_m,
        cache_fox_k, cache_fox_v, cache_fox_logf,
        [(cmk, l * DB, 0) for l in range(DEPTH)], [(cmv, l * DB, 0) for l in range(DEPTH)], W)
    (y_p, p_conv, p_S, p_C, p_n, p_m, p_k, p_v, p_lf) = prompt
    (y_s, s_conv, s_S, s_C, s_n, s_m, s_k, s_v, s_lf) = sample
    return (y_p, y_s, p_conv, p_S, p_C, p_n, p_m, p_k, p_v, p_lf, p_mem_k, p_mem_v,
            s_conv, s_S, s_C, s_n, s_m, s_k, s_v, s_lf)
```
